```python
import math
import jax, jax.numpy as jnp
from jax import lax
import numpy as np

D_MODEL = 1024
BATCH = 16
SEQ = 2048
DEPTH = 1

HEAD_DIM = 64
SWA_Q_HEADS = 8
SWA_KV_HEADS = 2
SWA_WINDOW = 128
DIL_HEADS = 4
DIL_PAIRS = ((128, 1), (512, 4), (2048, 16))
MEM_HEADS = 4
MEM_LEN = 256

BLOCK = 128
ROPE_THETA = 10000.0
LN_EPS = 1e-5
RMS_EPS = 1e-6
DEEPNORM_ALPHA = (2 * DEPTH) ** 0.25
DEEPNORM_BETA = (8 * DEPTH) ** -0.25

W_A = SWA_Q_HEADS * HEAD_DIM
W_KV_A = SWA_KV_HEADS * HEAD_DIM
W_B = DIL_HEADS * HEAD_DIM
W_C = MEM_HEADS * HEAD_DIM
D_MIX = W_A + W_B + W_C
IN_SPLITS = (W_A, W_KV_A, W_KV_A, W_B, W_B, W_B, W_C, D_MIX)
D_IN = sum(IN_SPLITS)

kernel_name = "hymba_swa_sink_dilated_mem_deepnorm"


def rope_tables(seq_len):
    pos = jnp.arange(seq_len, dtype=jnp.float32)
    inv = ROPE_THETA ** (-jnp.arange(0, HEAD_DIM, 2, dtype=jnp.float32) / HEAD_DIM)
    ang = pos[:, None] * inv[None, :]
    ang = jnp.concatenate([ang, ang], axis=-1)
    return jnp.cos(ang), jnp.sin(ang)


def apply_rope(t, cos, sin):
    tf = t.astype(jnp.float32)
    half = HEAD_DIM // 2
    rot = jnp.concatenate([-tf[..., half:], tf[..., :half]], axis=-1)
    return (tf * cos[None, :, None, :] + rot * sin[None, :, None, :]).astype(t.dtype)


def banded_causal_attention(q, k, v, max_dist, sinks=None):
    N, L, H, D = q.shape
    Hkv = k.shape[2]
    G = H // Hkv
    n_blk = -(-L // BLOCK)
    pad = n_blk * BLOCK - L
    if pad:
        q, k, v = [jnp.pad(t, ((0, 0), (0, pad), (0, 0), (0, 0))) for t in (q, k, v)]
    qb = q.reshape(N, n_blk, BLOCK, Hkv, G, D)
    kb = k.reshape(N, n_blk, BLOCK, Hkv, D)
    vb = v.reshape(N, n_blk, BLOCK, Hkv, D)

    def with_prev(t):
        prev = jnp.pad(t, ((0, 0), (1, 0), (0, 0), (0, 0), (0, 0)))[:, :-1]
        return jnp.concatenate([prev, t], axis=2)

    kk, vv = with_prev(kb), with_prev(vb)
    s = jnp.einsum('nbqhgd,nbkhd->nbhgqk', qb, kk,
                   preferred_element_type=jnp.float32) * (D ** -0.5)
    qi = jnp.arange(BLOCK)[:, None]
    kj = jnp.arange(2 * BLOCK)[None, :]
    dist = qi + BLOCK - kj
    band = (dist >= 0) & (dist <= max_dist)
    blk = jnp.arange(n_blk)[:, None, None]
    mask = band[None] & ((blk > 0) | (kj >= BLOCK)[None])
    s = jnp.where(mask[None, :, None, None], s, -jnp.inf)
    m = s.max(axis=-1)
    if sinks is not None:
        sink = sinks.astype(jnp.float32).reshape(Hkv, G)[None, None, :, :, None]
        m = jnp.maximum(m, sink)
    p = jnp.exp(s - m[..., None])
    denom = p.sum(axis=-1)
    if sinks is not None:
        denom = denom + jnp.exp(sink - m)
    o = jnp.einsum('nbhgqk,nbkhd->nbqhgd', (p / denom[..., None]).astype(v.dtype), vv)
    lse = m + jnp.log(denom)
    o = o.reshape(N, n_blk * BLOCK, H, D)[:, :L]
    lse = lse.transpose(0, 1, 4, 2, 3).reshape(N, n_blk * BLOCK, H)[:, :L]
    return o, lse


def dilated_attention(q, k, v):
    B, S, H, D = q.shape
    outs, lses = [], []
    for window, dil in DIL_PAIRS:
        Ls = S // dil

        def to_stream(t):
            return t.reshape(B, Ls, dil, H, D).transpose(0, 2, 1, 3, 4).reshape(B * dil, Ls, H, D)

        o, lse = banded_causal_attention(to_stream(q), to_stream(k), to_stream(v), window // dil)
        outs.append(o.reshape(B, dil, Ls, H, D).transpose(0, 2, 1, 3, 4).reshape(B, S, H, D))
        lses.append(lse.reshape(B, dil, Ls, H).transpose(0, 2, 1, 3).reshape(B, S, H))
    w = jax.nn.softmax(jnp.stack(lses), axis=0)
    o = jnp.einsum('pbsh,pbshd->bshd', w, jnp.stack(outs).astype(jnp.float32))
    return o.astype(q.dtype)


def memory_attention(q, mk, mv):
    s = jnp.einsum('bshd,bmhd->bhsm', q, mk, preferred_element_type=jnp.float32) * (HEAD_DIM ** -0.5)
    p = jax.nn.softmax(s, axis=-1)
    return jnp.einsum('bhsm,bmhd->bshd', p.astype(mv.dtype), mv)


def rms_normalize(t):
    tf = t.astype(jnp.float32)
    return tf * lax.rsqrt(jnp.mean(tf * tf, axis=-1, keepdims=True) + RMS_EPS)


def layer_norm(t, gain, bias):
    tf = t.astype(jnp.float32)
    mu = jnp.mean(tf, axis=-1, keepdims=True)
    var = jnp.mean(jnp.square(tf - mu), axis=-1, keepdims=True)
    return (tf - mu) * lax.rsqrt(var + LN_EPS) * gain.astype(jnp.float32) + bias.astype(jnp.float32)


def setup_inputs(seed: int = 0) -> dict:
    key = jax.random.key(seed)
    ks = jax.random.split(key, 10)
    x = jax.random.normal(ks[0], (BATCH, SEQ, D_MODEL), jnp.float32)
    mem = jax.random.normal(ks[1], (BATCH, MEM_LEN, D_MODEL), jnp.float32)
    w_in = jax.random.normal(ks[2], (DEPTH, D_MODEL, D_IN), jnp.float32) * D_MODEL ** -0.5
    b_in = 0.02 * jax.random.normal(ks[3], (DEPTH, D_IN), jnp.float32)
    w_mem = jax.random.normal(ks[4], (DEPTH, D_MODEL, 2 * W_C), jnp.float32) * D_MODEL ** -0.5
    attn_sinks = 0.5 * jax.random.normal(ks[5], (DEPTH, SWA_Q_HEADS), jnp.float32)
    g_branch = 1.0 + 0.05 * jax.random.normal(ks[6], (DEPTH, D_MIX), jnp.float32)
    w_out = (jax.random.normal(ks[7], (DEPTH, D_MIX, D_MODEL), jnp.float32)
             * D_MIX ** -0.5 * DEEPNORM_BETA)
    ln_gain = 1.0 + 0.05 * jax.random.normal(ks[8], (DEPTH, D_MODEL), jnp.float32)
    ln_bias = 0.02 * jax.random.normal(ks[9], (DEPTH, D_MODEL), jnp.float32)
    return {"x": x, "mem": mem, "w_in": w_in, "b_in": b_in, "w_mem": w_mem,
            "attn_sinks": attn_sinks, "g_branch": g_branch, "w_out": w_out,
            "ln_gain": ln_gain, "ln_bias": ln_bias}


def reference(x, mem, w_in, b_in, w_mem, attn_sinks, g_branch, w_out, ln_gain, ln_bias):
    B, S, _ = x.shape
    M = mem.shape[1]
    cos, sin = rope_tables(S)
    split_idx = [int(i) for i in np.cumsum(IN_SPLITS)[:-1]]

    def heads(t, n):
        return t.reshape(t.shape[0], t.shape[1], n, HEAD_DIM)

    for l in range(DEPTH):
        h = jnp.einsum('bsd,de->bse', x, w_in[l]) + b_in[l]
        qa, ka, va, qb, kb, vb, qc, z = jnp.split(h, split_idx, axis=-1)

        oa, _ = banded_causal_attention(apply_rope(heads(qa, SWA_Q_HEADS), cos, sin),
                                        apply_rope(heads(ka, SWA_KV_HEADS), cos, sin),
                                        heads(va, SWA_KV_HEADS),
                                        SWA_WINDOW - 1, attn_sinks[l])
        ob = dilated_attention(apply_rope(heads(qb, DIL_HEADS), cos, sin),
                               apply_rope(heads(kb, DIL_HEADS), cos, sin),
                               heads(vb, DIL_HEADS))
        mkv = jnp.einsum('bmd,de->bme', mem, w_mem[l])
        mk, mv = jnp.split(mkv, 2, axis=-1)
        oc = memory_attention(heads(qc, MEM_HEADS), heads(mk, MEM_HEADS), heads(mv, MEM_HEADS))

        y = jnp.concatenate([rms_normalize(oa.reshape(B, S, W_A)),
                             rms_normalize(ob.reshape(B, S, W_B)),
                             rms_normalize(oc.reshape(B, S, W_C))], axis=-1)
        y = y * g_branch[l].astype(jnp.float32) * jax.nn.silu(z.astype(jnp.float32))
        y = jnp.einsum('bse,ed->bsd', y.astype(x.dtype), w_out[l])

        x = layer_norm(DEEPNORM_ALPHA * x + y, ln_gain[l], ln_bias[l]).astype(x.dtype)
    return x
```

```python
import functools

import numpy as np
import jax
import jax.numpy as jnp
from jax import lax
from jax.experimental import pallas as pl
from jax.experimental.pallas import tpu as pltpu

D_MODEL = 1024
HEAD_DIM = 64
SWA_Q_HEADS = 8
SWA_KV_HEADS = 2
SWA_WINDOW = 128
DIL_HEADS = 4
DIL_PAIRS = ((128, 1), (512, 4), (2048, 16))
MEM_HEADS = 4
BLOCK = 128
ROPE_THETA = 10000.0
LN_EPS = 1e-5
RMS_EPS = 1e-6
DEPTH = 1
DEEPNORM_ALPHA = (2 * DEPTH) ** 0.25

W_A = SWA_Q_HEADS * HEAD_DIM
W_KV_A = SWA_KV_HEADS * HEAD_DIM
W_B = DIL_HEADS * HEAD_DIM
W_C = MEM_HEADS * HEAD_DIM
D_MIX = W_A + W_B + W_C
D_IN = W_A + 2 * W_KV_A + 3 * W_B + W_C + D_MIX

V7X_LANES = 128
V7X_VMEM_BYTES = 64 * 2**20
VMEM_HEADROOM_BYTES = 8 * 2**20

PAIR = 2 * HEAD_DIM
assert PAIR == V7X_LANES
SCALE = HEAD_DIM ** -0.5
MASKED = -1e30

OFF_QA = 0
OFF_KA = OFF_QA + W_A
OFF_VA = OFF_KA + W_KV_A
OFF_QB = OFF_VA + W_KV_A
OFF_KB = OFF_QB + W_B
OFF_VB = OFF_KB + W_B
OFF_QC = OFF_VB + W_B
OFF_Z = OFF_QC + W_C
assert OFF_Z + D_MIX == D_IN

ROW_TILE = 512

f32 = jnp.float32
bf16 = jnp.bfloat16


def _vmem_limit(pipelined_bytes, scratch_bytes=0):
    need = 2 * pipelined_bytes + scratch_bytes + VMEM_HEADROOM_BYTES
    return int(min(need, V7X_VMEM_BYTES - VMEM_HEADROOM_BYTES))


def _nbytes(shape, dtype):
    return int(np.prod(shape)) * jnp.dtype(dtype).itemsize


def _lane_is_first_head():
    return lax.broadcasted_iota(jnp.int32, (1, V7X_LANES), 1) < HEAD_DIM


def _in_proj_kernel(x_ref, w_ref, b_ref, cos_ref, sin_ref, g_ref,
                    qa_ref, ka_ref, va_ref, qb_ref, kb_ref, vb_ref, qc_ref, gate_ref):
    xb = x_ref[0].astype(bf16)
    cos = cos_ref[...]
    sin = sin_ref[...]
    lane = lax.broadcasted_iota(jnp.int32, (1, V7X_LANES), 1)
    first_half = (lane % HEAD_DIM) < (HEAD_DIM // 2)
    first_head = lane < HEAD_DIM

    def proj(off, width):
        acc = jnp.dot(xb, w_ref[:, off:off + width], preferred_element_type=f32)
        return acc + b_ref[:, off:off + width]

    def rope(a):
        rot = jnp.where(first_half,
                        pltpu.roll(a, V7X_LANES - HEAD_DIM // 2, 1),
                        pltpu.roll(a, HEAD_DIM // 2, 1))
        return a * cos + rot * sin

    def groups(a):
        return [a[:, i:i + V7X_LANES] for i in range(0, a.shape[1], V7X_LANES)]

    def both_halves(a):
        swapped = pltpu.roll(a, HEAD_DIM, 1)
        return jnp.where(first_head, a, swapped), jnp.where(first_head, swapped, a)

    for c in range(0, W_A, 2 * V7X_LANES):
        for i, a in enumerate(groups(proj(OFF_QA + c, 2 * V7X_LANES))):
            lo = c + i * V7X_LANES
            qa_ref[0, :, lo:lo + V7X_LANES] = (rope(a) * SCALE).astype(bf16)
    k_a, v_a = groups(proj(OFF_KA, 2 * W_KV_A))
    for ref, a in ((ka_ref, rope(k_a)), (va_ref, v_a)):
        kv0, kv1 = both_halves(a)
        ref[0, :, 0:V7X_LANES] = kv0.astype(bf16)
        ref[0, :, V7X_LANES:2 * V7X_LANES] = kv1.astype(bf16)
    for i, a in enumerate(groups(proj(OFF_QB, W_B))):
        qb_ref[0, :, i * V7X_LANES:(i + 1) * V7X_LANES] = (rope(a) * SCALE).astype(bf16)
    for i, a in enumerate(groups(proj(OFF_KB, W_B))):
        kb_ref[0, :, i * V7X_LANES:(i + 1) * V7X_LANES] = rope(a).astype(bf16)
    vb_ref[0] = proj(OFF_VB, W_B).astype(bf16)
    qc_ref[0] = (proj(OFF_QC, W_C) * SCALE).astype(bf16)
    for c in range(0, D_MIX, 2 * V7X_LANES):
        z = proj(OFF_Z + c, 2 * V7X_LANES)
        gate = z * jax.nn.sigmoid(z) * g_ref[:, c:c + 2 * V7X_LANES]
        gate_ref[0, :, c:c + 2 * V7X_LANES] = gate.astype(bf16)


def _in_proj(x, w_in, b_in, cos2, sin2, g_branch):
    B, S, _ = x.shape
    tm = ROW_TILE
    widths = (W_A, 2 * W_KV_A, 2 * W_KV_A, W_B, W_B, W_B, W_C, D_MIX)
    row_block = lambda width: pl.BlockSpec((1, tm, width), lambda b, i: (b, i, 0))
    const = lambda shape: pl.BlockSpec(shape, lambda b, i: (0,) * len(shape))
    pipelined = (_nbytes((tm, D_MODEL), f32) + _nbytes((D_MODEL, D_IN), bf16)
                 + sum(_nbytes((tm, w), bf16) for w in widths) + 2 * _nbytes((tm, V7X_LANES), f32))
    return pl.pallas_call(
        _in_proj_kernel,
        grid=(B, S // tm),
        in_specs=[row_block(D_MODEL), const((D_MODEL, D_IN)), const((1, D_IN)),
                  pl.BlockSpec((tm, V7X_LANES), lambda b, i: (i, 0)),
                  pl.BlockSpec((tm, V7X_LANES), lambda b, i: (i, 0)),
                  const((1, D_MIX))],
        out_specs=[row_block(w) for w in widths],
        out_shape=[jax.ShapeDtypeStruct((B, S, w), bf16) for w in widths],
        compiler_params=pltpu.CompilerParams(
            dimension_semantics=("arbitrary", "arbitrary"),
            vmem_limit_bytes=_vmem_limit(pipelined, _nbytes((tm, D_IN), f32))),
        name="in_proj",
    )(x, w_in, b_in, cos2, sin2, g_branch)


def _pair_attention(q2, k2, v2, bias, sinks=None):
    m_rows = q2.shape[0]
    first_head = _lane_is_first_head()
    zero = jnp.zeros_like(q2)
    stacked = jnp.concatenate([jnp.where(first_head, q2, zero),
                               jnp.where(first_head, zero, q2)], axis=0)
    s = lax.dot_general(stacked, k2, (((1,), (1,)), ((), ())),
                        preferred_element_type=f32)
    probs, inv_l, lses = [], [], []
    for h in range(2):
        sh = s[h * m_rows:(h + 1) * m_rows]
        if bias is not None:
            sh = sh + bias
        m = jnp.max(sh, axis=-1, keepdims=True)
        if sinks is not None:
            m = jnp.maximum(m, sinks[h])
        p = jnp.exp(sh - m)
        l = jnp.sum(p, axis=-1, keepdims=True)
        if sinks is not None:
            l = l + jnp.exp(sinks[h] - m)
        probs.append(p.astype(bf16))
        inv_l.append(1.0 / l)
        lses.append(m + jnp.log(l))
    o = jnp.dot(jnp.concatenate(probs, axis=0), v2, preferred_element_type=f32)
    out = jnp.where(first_head, o[:m_rows] * inv_l[0], o[m_rows:] * inv_l[1])
    lse = jnp.where(first_head, lses[0], lses[1])
    return out, lse


def _band_bias(n_keys, max_dist):
    qi = np.arange(BLOCK)[:, None]
    kj = np.arange(n_keys)[None, :]
    dist = qi + (n_keys - BLOCK) - kj
    return np.where((dist >= 0) & (dist <= max_dist), 0.0, MASKED).astype(np.float32)


def _swa_kernel(sink_ref, q_ref, k_ref, v_ref, bias_first_ref, bias_ref, o_ref):
    n_blk = q_ref.shape[1] // BLOCK
    groups_per_kv = (SWA_Q_HEADS // SWA_KV_HEADS) // 2

    def block(r0, k0, n_keys, bias):
        q = q_ref[0, pl.ds(r0, BLOCK), :]
        k = k_ref[0, pl.ds(k0, n_keys), :]
        v = v_ref[0, pl.ds(k0, n_keys), :]
        for g in range(W_A // PAIR):
            kv = (g // groups_per_kv) * PAIR
            out, _ = _pair_attention(q[:, g * PAIR:(g + 1) * PAIR],
                                     k[:, kv:kv + PAIR], v[:, kv:kv + PAIR], bias,
                                     sinks=(sink_ref[2 * g], sink_ref[2 * g + 1]))
            o_ref[0, pl.ds(r0, BLOCK), g * PAIR:(g + 1) * PAIR] = out.astype(bf16)

    block(0, 0, BLOCK, bias_first_ref[...])

    def body(i, carry):
        r0 = pl.multiple_of(i * BLOCK, BLOCK)
        block(r0, pl.multiple_of(r0 - BLOCK, BLOCK), 2 * BLOCK, bias_ref[...])
        return carry

    lax.fori_loop(1, n_blk, body, 0)


def _swa(sinks, qa, ka, va):
    B, S, _ = qa.shape
    bias_first = jnp.asarray(_band_bias(BLOCK, SWA_WINDOW - 1))
    bias = jnp.asarray(_band_bias(2 * BLOCK, SWA_WINDOW - 1))
    seq = lambda width: pl.BlockSpec((1, S, width), lambda b: (b, 0, 0))
    const = lambda shape: pl.BlockSpec(shape, lambda b: (0,) * len(shape))
    pipelined = (_nbytes((S, 2 * W_A), bf16) + 2 * _nbytes((S, 2 * W_KV_A), bf16)
                 + _nbytes((BLOCK, 3 * BLOCK), f32))
    return pl.pallas_call(
        _swa_kernel,
        grid=(B,),
        in_specs=[pl.BlockSpec(memory_space=pltpu.SMEM),
                  seq(W_A), seq(2 * W_KV_A), seq(2 * W_KV_A),
                  const((BLOCK, BLOCK)), const((BLOCK, 2 * BLOCK))],
        out_specs=seq(W_A),
        out_shape=jax.ShapeDtypeStruct((B, S, W_A), bf16),
        compiler_params=pltpu.CompilerParams(
            dimension_semantics=("arbitrary",), vmem_limit_bytes=_vmem_limit(pipelined)),
        name="swa_attention",
    )(sinks, qa, ka, va, bias_first, bias)


def _dilated_kernel(q1_ref, k1_ref, v1_ref, q4_ref, k4_ref, v4_ref, q16_ref, k16_ref, v16_ref,
                    bias_first_ref, bias_ref, o_ref, o4_ref, l4_ref, o16_ref, l16_ref):
    seq_len = q1_ref.shape[1]
    n_pairs = W_B // PAIR
    bias_first = bias_first_ref[...]

    dil = DIL_PAIRS[2][1]
    for j in range(dil):
        for g in range(n_pairs):
            c = j * W_B + g * PAIR
            out, lse = _pair_attention(q16_ref[0, :, c:c + PAIR], k16_ref[0, :, c:c + PAIR],
                                       v16_ref[0, :, c:c + PAIR], bias_first)
            o16_ref[g, pl.ds(j, BLOCK, stride=dil), :] = out
            l16_ref[g, pl.ds(j, BLOCK, stride=dil), :] = lse

    dil = DIL_PAIRS[1][1]
    for j in range(dil):
        def block4(u0, k0, n_keys, bias, j=j, dil=dil):
            for g in range(n_pairs):
                c = j * W_B + g * PAIR
                out, lse = _pair_attention(q4_ref[0, pl.ds(u0, BLOCK), c:c + PAIR],
                                           k4_ref[0, pl.ds(k0, n_keys), c:c + PAIR],
                                           v4_ref[0, pl.ds(k0, n_keys), c:c + PAIR], bias)
                o4_ref[g, pl.ds(u0 * dil + j, BLOCK, stride=dil), :] = out
                l4_ref[g, pl.ds(u0 * dil + j, BLOCK, stride=dil), :] = lse

        block4(0, 0, BLOCK, bias_first)

        def body4(i, carry, block4=block4):
            u0 = pl.multiple_of(i * BLOCK, BLOCK)
            block4(u0, pl.multiple_of(u0 - BLOCK, BLOCK), 2 * BLOCK, bias_ref[...])
            return carry

        lax.fori_loop(1, seq_len // dil // BLOCK, body4, 0)

    def block1(r0, k0, n_keys, bias):
        for g in range(n_pairs):
            c = g * PAIR
            o1, l1 = _pair_attention(q1_ref[0, pl.ds(r0, BLOCK), c:c + PAIR],
                                     k1_ref[0, pl.ds(k0, n_keys), c:c + PAIR],
                                     v1_ref[0, pl.ds(k0, n_keys), c:c + PAIR], bias)
            o4 = o4_ref[g, pl.ds(r0, BLOCK), :]
            l4 = l4_ref[g, pl.ds(r0, BLOCK), :]
            o16 = o16_ref[g, pl.ds(r0, BLOCK), :]
            l16 = l16_ref[g, pl.ds(r0, BLOCK), :]
            top = jnp.maximum(jnp.maximum(l1, l4), l16)
            e1 = jnp.exp(l1 - top)
            e4 = jnp.exp(l4 - top)
            e16 = jnp.exp(l16 - top)
            merged = (e1 * o1 + e4 * o4 + e16 * o16) / (e1 + e4 + e16)
            o_ref[0, pl.ds(r0, BLOCK), c:c + PAIR] = merged.astype(bf16)

    block1(0, 0, BLOCK, bias_first)

    def body1(i, carry):
        r0 = pl.multiple_of(i * BLOCK, BLOCK)
        block1(r0, pl.multiple_of(r0 - BLOCK, BLOCK), 2 * BLOCK, bias_ref[...])
        return carry

    lax.fori_loop(1, seq_len // BLOCK, body1, 0)


def _dilated(qb, kb, vb):
    B, S, _ = qb.shape
    assert all(w // r == BLOCK for w, r in DIL_PAIRS) and S // DIL_PAIRS[2][1] == BLOCK
    bias_first = jnp.asarray(_band_bias(BLOCK, BLOCK))
    bias = jnp.asarray(_band_bias(2 * BLOCK, BLOCK))
    views = []
    for _, r in DIL_PAIRS:
        views += [t.reshape(B, S // r, r * W_B) for t in (qb, kb, vb)]
    view_specs = []
    for _, r in DIL_PAIRS:
        view_specs += [pl.BlockSpec((1, S // r, r * W_B), lambda b: (b, 0, 0))] * 3
    const = lambda shape: pl.BlockSpec(shape, lambda b: (0,) * len(shape))
    slab = (W_B // PAIR, S, V7X_LANES)
    pipelined = 10 * _nbytes((S, W_B), bf16) + _nbytes((BLOCK, 3 * BLOCK), f32)
    return pl.pallas_call(
        _dilated_kernel,
        grid=(B,),
        in_specs=view_specs + [const((BLOCK, BLOCK)), const((BLOCK, 2 * BLOCK))],
        out_specs=pl.BlockSpec((1, S, W_B), lambda b: (b, 0, 0)),
        out_shape=jax.ShapeDtypeStruct((B, S, W_B), bf16),
        scratch_shapes=[pltpu.VMEM(slab, f32) for _ in range(4)],
        compiler_params=pltpu.CompilerParams(
            dimension_semantics=("arbitrary",),
            vmem_limit_bytes=_vmem_limit(pipelined, 4 * _nbytes(slab, f32))),
        name="dilated_attention",
    )(*views, bias_first, bias)


def _mem_kernel(q_ref, mem_ref, w_ref, o_ref):
    mkv = jnp.dot(mem_ref[0].astype(bf16), w_ref[...], preferred_element_type=f32).astype(bf16)
    mk, mv = mkv[:, :W_C], mkv[:, W_C:]

    def body(i, carry):
        r0 = pl.multiple_of(i * BLOCK, BLOCK)
        for g in range(W_C // PAIR):
            c = g * PAIR
            out, _ = _pair_attention(q_ref[0, pl.ds(r0, BLOCK), c:c + PAIR],
                                     mk[:, c:c + PAIR], mv[:, c:c + PAIR], None)
            o_ref[0, pl.ds(r0, BLOCK), c:c + PAIR] = out.astype(bf16)
        return carry

    lax.fori_loop(0, q_ref.shape[1] // BLOCK, body, 0)


def _mem_attention(qc, mem, w_mem):
    B, S, _ = qc.shape
    M = mem.shape[1]
    pipelined = (2 * _nbytes((S, W_C), bf16) + _nbytes((M, D_MODEL), f32)
                 + _nbytes((D_MODEL, 2 * W_C), bf16))
    return pl.pallas_call(
        _mem_kernel,
        grid=(B,),
        in_specs=[pl.BlockSpec((1, S, W_C), lambda b: (b, 0, 0)),
                  pl.BlockSpec((1, M, D_MODEL), lambda b: (b, 0, 0)),
                  pl.BlockSpec((D_MODEL, 2 * W_C), lambda b: (0, 0))],
        out_specs=pl.BlockSpec((1, S, W_C), lambda b: (b, 0, 0)),
        out_shape=jax.ShapeDtypeStruct((B, S, W_C), bf16),
        compiler_params=pltpu.CompilerParams(
            dimension_semantics=("arbitrary",), vmem_limit_bytes=_vmem_limit(pipelined)),
        name="memory_attention",
    )(qc, mem, w_mem)


def _out_kernel(oa_ref, ob_ref, oc_ref, gate_ref, x_ref, w_ref, gain_ref, bias_ref, out_ref):
    def rms_normalize(t):
        t = t.astype(f32)
        return t * lax.rsqrt(jnp.mean(t * t, axis=-1, keepdims=True) + RMS_EPS)

    y = jnp.concatenate([rms_normalize(oa_ref[0]), rms_normalize(ob_ref[0]),
                         rms_normalize(oc_ref[0])], axis=-1)
    y = (y * gate_ref[0].astype(f32)).astype(bf16)
    r = DEEPNORM_ALPHA * x_ref[0] + jnp.dot(y, w_ref[...], preferred_element_type=f32)
    mu = jnp.mean(r, axis=-1, keepdims=True)
    d = r - mu
    var = jnp.mean(d * d, axis=-1, keepdims=True)
    out_ref[0] = d * lax.rsqrt(var + LN_EPS) * gain_ref[...] + bias_ref[...]


def _out_proj(oa, ob, oc, gate, x, w_out, ln_gain, ln_bias):
    B, S, _ = x.shape
    tm = ROW_TILE
    row_block = lambda width: pl.BlockSpec((1, tm, width), lambda b, i: (b, i, 0))
    const = lambda shape: pl.BlockSpec(shape, lambda b, i: (0,) * len(shape))
    pipelined = (2 * _nbytes((tm, D_MIX), bf16) + 2 * _nbytes((tm, D_MODEL), f32)
                 + _nbytes((D_MIX, D_MODEL), bf16))
    return pl.pallas_call(
        _out_kernel,
        grid=(B, S // tm),
        in_specs=[row_block(W_A), row_block(W_B), row_block(W_C), row_block(D_MIX),
                  row_block(D_MODEL), const((D_MIX, D_MODEL)),
                  const((1, D_MODEL)), const((1, D_MODEL))],
        out_specs=row_block(D_MODEL),
        out_shape=jax.ShapeDtypeStruct((B, S, D_MODEL), f32),
        compiler_params=pltpu.CompilerParams(
            dimension_semantics=("arbitrary", "arbitrary"),
            vmem_limit_bytes=_vmem_limit(pipelined, 3 * _nbytes((tm, D_MODEL), f32))),
        name="out_proj",
    )(oa, ob, oc, gate, x, w_out, ln_gain, ln_bias)


def _rope_tables(seq_len):
    pos = jnp.arange(seq_len, dtype=f32)
    inv = ROPE_THETA ** (-jnp.arange(0, HEAD_DIM, 2, dtype=f32) / HEAD_DIM)
    ang = pos[:, None] * inv[None, :]
    cos, sin = jnp.cos(ang), jnp.sin(ang)
    cos2 = jnp.concatenate([cos, cos, cos, cos], axis=-1)
    sin2 = jnp.concatenate([-sin, sin, -sin, sin], axis=-1)
    return cos2, sin2


def kernel(x, mem, w_in, b_in, w_mem, attn_sinks, g_branch, w_out, ln_gain, ln_bias):
    assert w_in.shape[0] == DEPTH
    S = x.shape[1]
    cos2, sin2 = _rope_tables(S)
    for l in range(DEPTH):
        qa, ka, va, qb, kb, vb, qc, gate = _in_proj(
            x, w_in[l].astype(bf16), b_in[l][None, :], cos2, sin2, g_branch[l][None, :])
        oa = _swa(attn_sinks[l], qa, ka, va)
        ob = _dilated(qb, kb, vb)
        oc = _mem_attention(qc, mem, w_mem[l].astype(bf16))
        x = _out_proj(oa, ob, oc, gate, x, w_out[l].astype(bf16),
                      ln_gain[l][None, :], ln_bias[l][None, :])
    return x
```

```python
import functools

import numpy as np
import jax
import jax.numpy as jnp
from jax import lax
from jax.experimental import pallas as pl
from jax.experimental.pallas import tpu as pltpu

D_MODEL = 1024
HEAD_DIM = 64
SWA_Q_HEADS = 8
SWA_KV_HEADS = 2
SWA_WINDOW = 128
DIL_HEADS = 4
DIL_PAIRS = ((128, 1), (512, 4), (2048, 16))
MEM_HEADS = 4
BLOCK = 128
ROPE_THETA = 10000.0
LN_EPS = 1e-5
RMS_EPS = 1e-6
DEPTH = 1
DEEPNORM_ALPHA = (2 * DEPTH) ** 0.25

W_A = SWA_Q_HEADS * HEAD_DIM
W_KV_A = SWA_KV_HEADS * HEAD_DIM
W_B = DIL_HEADS * HEAD_DIM
W_C = MEM_HEADS * HEAD_DIM
D_MIX = W_A + W_B + W_C
D_IN = W_A + 2 * W_KV_A + 3 * W_B + W_C + D_MIX

V7X_LANES = 128
V7X_VMEM_BYTES = 64 * 2**20
VMEM_HEADROOM_BYTES = 8 * 2**20

PAIR = 2 * HEAD_DIM
assert PAIR == V7X_LANES
SCALE = HEAD_DIM ** -0.5
MASKED = -1e30

OFF_QA = 0
OFF_KA = OFF_QA + W_A
OFF_VA = OFF_KA + W_KV_A
OFF_QB = OFF_VA + W_KV_A
OFF_KB = OFF_QB + W_B
OFF_VB = OFF_KB + W_B
OFF_QC = OFF_VB + W_B
OFF_Z = OFF_QC + W_C
assert OFF_Z + D_MIX == D_IN

ROW_TILE = 512

f32 = jnp.float32
bf16 = jnp.bfloat16


def _vmem_limit(pipelined_bytes, scratch_bytes=0):
    need = 2 * pipelined_bytes + scratch_bytes + VMEM_HEADROOM_BYTES
    return int(min(need, V7X_VMEM_BYTES - VMEM_HEADROOM_BYTES))


def _nbytes(shape, dtype):
    return int(np.prod(shape)) * jnp.dtype(dtype).itemsize


def _lane_is_first_head():
    return lax.broadcasted_iota(jnp.int32, (1, V7X_LANES), 1) < HEAD_DIM


def _in_proj_kernel(x_ref, w_ref, b_ref, cos_ref, sin_ref, g_ref,
                    qa_ref, ka_ref, va_ref, qc_ref, gate_ref,
                    q1_ref, k1_ref, v1_ref, q4_ref, k4_ref, v4_ref, q16_ref, k16_ref, v16_ref,
                    stage_ref):
    tm = x_ref.shape[1]
    xb = x_ref[0].astype(bf16)
    cos = cos_ref[...]
    sin = sin_ref[...]
    lane = lax.broadcasted_iota(jnp.int32, (1, V7X_LANES), 1)
    first_half = (lane % HEAD_DIM) < (HEAD_DIM // 2)
    first_head = lane < HEAD_DIM

    def proj(off, width):
        acc = jnp.dot(xb, w_ref[:, off:off + width], preferred_element_type=f32)
        return acc + b_ref[:, off:off + width]

    def rope(a):
        rot = jnp.where(first_half,
                        pltpu.roll(a, V7X_LANES - HEAD_DIM // 2, 1),
                        pltpu.roll(a, HEAD_DIM // 2, 1))
        return a * cos + rot * sin

    def groups(a):
        return [a[:, i:i + V7X_LANES] for i in range(0, a.shape[1], V7X_LANES)]

    def both_halves(a):
        swapped = pltpu.roll(a, HEAD_DIM, 1)
        return jnp.where(first_head, a, swapped), jnp.where(first_head, swapped, a)

    for c in range(0, W_A, 2 * V7X_LANES):
        for i, a in enumerate(groups(proj(OFF_QA + c, 2 * V7X_LANES))):
            lo = c + i * V7X_LANES
            qa_ref[0, :, lo:lo + V7X_LANES] = (rope(a) * SCALE).astype(bf16)
    k_a, v_a = groups(proj(OFF_KA, 2 * W_KV_A))
    for ref, a in ((ka_ref, rope(k_a)), (va_ref, v_a)):
        kv0, kv1 = both_halves(a)
        ref[0, :, 0:V7X_LANES] = kv0.astype(bf16)
        ref[0, :, V7X_LANES:2 * V7X_LANES] = kv1.astype(bf16)
    mixer_b = ((OFF_QB, lambda a: rope(a) * SCALE, (q1_ref, q4_ref, q16_ref)),
               (OFF_KB, rope, (k1_ref, k4_ref, k16_ref)),
               (OFF_VB, lambda a: a, (v1_ref, v4_ref, v16_ref)))
    for t, (off, fn, (nat_ref, r4_ref, r16_ref)) in enumerate(mixer_b):
        for g, a in enumerate(groups(proj(off, W_B))):
            a = fn(a)
            nat_ref[0, :, g * PAIR:(g + 1) * PAIR] = a.astype(bf16)
            slab = t * (W_B // PAIR) + g
            stage_ref[slab] = a
            for view_ref, dil in ((r4_ref, DIL_PAIRS[1][1]), (r16_ref, DIL_PAIRS[2][1])):
                for j in range(dil):
                    stream = stage_ref[slab, pl.ds(j, tm // dil, stride=dil), :]
                    c = j * W_B + g * PAIR
                    view_ref[0, :, c:c + PAIR] = stream.astype(bf16)
    qc_ref[0] = (proj(OFF_QC, W_C) * SCALE).astype(bf16)
    for c in range(0, D_MIX, 2 * V7X_LANES):
        z = proj(OFF_Z + c, 2 * V7X_LANES)
        gate = z * jax.nn.sigmoid(z) * g_ref[:, c:c + 2 * V7X_LANES]
        gate_ref[0, :, c:c + 2 * V7X_LANES] = gate.astype(bf16)


def _in_proj(x, w_in, b_in, cos2, sin2, g_branch):
    B, S, _ = x.shape
    tm = ROW_TILE
    widths = (W_A, 2 * W_KV_A, 2 * W_KV_A, W_C, D_MIX)
    out_blocks = [(tm, w) for w in widths]
    for _, r in DIL_PAIRS:
        out_blocks += [(tm // r, r * W_B)] * 3
    row_block = lambda rows, width: pl.BlockSpec((1, rows, width), lambda b, i: (b, i, 0))
    const = lambda shape: pl.BlockSpec(shape, lambda b, i: (0,) * len(shape))
    stage = (3 * (W_B // PAIR), tm, V7X_LANES)
    pipelined = (_nbytes((tm, D_MODEL), f32) + _nbytes((D_MODEL, D_IN), bf16)
                 + sum(_nbytes(blk, bf16) for blk in out_blocks) + 2 * _nbytes((tm, V7X_LANES), f32))
    return pl.pallas_call(
        _in_proj_kernel,
        grid=(B, S // tm),
        in_specs=[row_block(tm, D_MODEL), const((D_MODEL, D_IN)), const((1, D_IN)),
                  pl.BlockSpec((tm, V7X_LANES), lambda b, i: (i, 0)),
                  pl.BlockSpec((tm, V7X_LANES), lambda b, i: (i, 0)),
                  const((1, D_MIX))],
        out_specs=[row_block(*blk) for blk in out_blocks],
        out_shape=[jax.ShapeDtypeStruct((B, S * rows // tm, w), bf16) for rows, w in out_blocks],
        scratch_shapes=[pltpu.VMEM(stage, f32)],
        compiler_params=pltpu.CompilerParams(
            dimension_semantics=("arbitrary", "arbitrary"),
            vmem_limit_bytes=_vmem_limit(pipelined, _nbytes(stage, f32) + _nbytes((tm, D_IN), f32))),
        name="in_proj",
    )(x, w_in, b_in, cos2, sin2, g_branch)


def _pair_attention(q2, k2, v2, bias, sinks=None):
    m_rows = q2.shape[0]
    first_head = _lane_is_first_head()
    zero = jnp.zeros_like(q2)
    stacked = jnp.concatenate([jnp.where(first_head, q2, zero),
                               jnp.where(first_head, zero, q2)], axis=0)
    s = lax.dot_general(stacked, k2, (((1,), (1,)), ((), ())),
                        preferred_element_type=f32)
    probs, inv_l, lses = [], [], []
    for h in range(2):
        sh = s[h * m_rows:(h + 1) * m_rows]
        if bias is not None:
            sh = sh + bias
        m = jnp.max(sh, axis=-1, keepdims=True)
        if sinks is not None:
            m = jnp.maximum(m, sinks[h])
        p = jnp.exp(sh - m)
        l = jnp.sum(p, axis=-1, keepdims=True)
        if sinks is not None:
            l = l + jnp.exp(sinks[h] - m)
        probs.append(p.astype(bf16))
        inv_l.append(1.0 / l)
        lses.append(m + jnp.log(l))
    o = jnp.dot(jnp.concatenate(probs, axis=0), v2, preferred_element_type=f32)
    out = jnp.where(first_head, o[:m_rows] * inv_l[0], o[m_rows:] * inv_l[1])
    lse = jnp.where(first_head, lses[0], lses[1])
    return out, lse


def _band_bias(n_keys, max_dist):
    qi = np.arange(BLOCK)[:, None]
    kj = np.arange(n_keys)[None, :]
    dist = qi + (n_keys - BLOCK) - kj
    return np.where((dist >= 0) & (dist <= max_dist), 0.0, MASKED).astype(np.float32)


def _swa_kernel(sink_ref, q_ref, k_ref, v_ref, bias_first_ref, bias_ref, o_ref):
    n_blk = q_ref.shape[1] // BLOCK
    groups_per_kv = (SWA_Q_HEADS // SWA_KV_HEADS) // 2

    def block(r0, k0, n_keys, bias):
        q = q_ref[0, pl.ds(r0, BLOCK), :]
        k = k_ref[0, pl.ds(k0, n_keys), :]
        v = v_ref[0, pl.ds(k0, n_keys), :]
        for g in range(W_A // PAIR):
            kv = (g // groups_per_kv) * PAIR
            out, _ = _pair_attention(q[:, g * PAIR:(g + 1) * PAIR],
                                     k[:, kv:kv + PAIR], v[:, kv:kv + PAIR], bias,
                                     sinks=(sink_ref[2 * g], sink_ref[2 * g + 1]))
            o_ref[0, pl.ds(r0, BLOCK), g * PAIR:(g + 1) * PAIR] = out.astype(bf16)

    block(0, 0, BLOCK, bias_first_ref[...])

    def body(i, carry):
        r0 = pl.multiple_of(i * BLOCK, BLOCK)
        block(r0, pl.multiple_of(r0 - BLOCK, BLOCK), 2 * BLOCK, bias_ref[...])
        return carry

    lax.fori_loop(1, n_blk, body, 0)


def _swa(sinks, qa, ka, va):
    B, S, _ = qa.shape
    bias_first = jnp.asarray(_band_bias(BLOCK, SWA_WINDOW - 1))
    bias = jnp.asarray(_band_bias(2 * BLOCK, SWA_WINDOW - 1))
    seq = lambda width: pl.BlockSpec((1, S, width), lambda b: (b, 0, 0))
    const = lambda shape: pl.BlockSpec(shape, lambda b: (0,) * len(shape))
    pipelined = (_nbytes((S, 2 * W_A), bf16) + 2 * _nbytes((S, 2 * W_KV_A), bf16)
                 + _nbytes((BLOCK, 3 * BLOCK), f32))
    return pl.pallas_call(
        _swa_kernel,
        grid=(B,),
        in_specs=[pl.BlockSpec(memory_space=pltpu.SMEM),
                  seq(W_A), seq(2 * W_KV_A), seq(2 * W_KV_A),
                  const((BLOCK, BLOCK)), const((BLOCK, 2 * BLOCK))],
        out_specs=seq(W_A),
        out_shape=jax.ShapeDtypeStruct((B, S, W_A), bf16),
        compiler_params=pltpu.CompilerParams(
            dimension_semantics=("arbitrary",), vmem_limit_bytes=_vmem_limit(pipelined)),
        name="swa_attention",
    )(sinks, qa, ka, va, bias_first, bias)


def _dilated_kernel(q1_ref, k1_ref, v1_ref, q4_ref, k4_ref, v4_ref, q16_ref, k16_ref, v16_ref,
                    bias_first_ref, bias_ref, o_ref, o4_ref, l4_ref, o16_ref, l16_ref):
    seq_len = q1_ref.shape[1]
    n_pairs = W_B // PAIR
    bias_first = bias_first_ref[...]

    dil = DIL_PAIRS[2][1]
    for j in range(dil):
        for g in range(n_pairs):
            c = j * W_B + g * PAIR
            out, lse = _pair_attention(q16_ref[0, :, c:c + PAIR], k16_ref[0, :, c:c + PAIR],
                                       v16_ref[0, :, c:c + PAIR], bias_first)
            o16_ref[g, pl.ds(j, BLOCK, stride=dil), :] = out
            l16_ref[g, pl.ds(j, BLOCK, stride=dil), :] = lse

    dil = DIL_PAIRS[1][1]
    for j in range(dil):
        def block4(u0, k0, n_keys, bias, j=j, dil=dil):
            for g in range(n_pairs):
                c = j * W_B + g * PAIR
                out, lse = _pair_attention(q4_ref[0, pl.ds(u0, BLOCK), c:c + PAIR],
                                           k4_ref[0, pl.ds(k0, n_keys), c:c + PAIR],
                                           v4_ref[0, pl.ds(k0, n_keys), c:c + PAIR], bias)
                o4_ref[g, pl.ds(u0 * dil + j, BLOCK, stride=dil), :] = out
                l4_ref[g, pl.ds(u0 * dil + j, BLOCK, stride=dil), :] = lse

        block4(0, 0, BLOCK, bias_first)

        def body4(i, carry, block4=block4):
            u0 = pl.multiple_of(i * BLOCK, BLOCK)
            block4(u0, pl.multiple_of(u0 - BLOCK, BLOCK), 2 * BLOCK, bias_ref[...])
            return carry

        lax.fori_loop(1, seq_len // dil // BLOCK, body4, 0)

    def block1(r0, k0, n_keys, bias):
        for g in range(n_pairs):
            c = g * PAIR
            o1, l1 = _pair_attention(q1_ref[0, pl.ds(r0, BLOCK), c:c + PAIR],
                                     k1_ref[0, pl.ds(k0, n_keys), c:c + PAIR],
                                     v1_ref[0, pl.ds(k0, n_keys), c:c + PAIR], bias)
            o4 = o4_ref[g, pl.ds(r0, BLOCK), :]
            l4 = l4_ref[g, pl.ds(r0, BLOCK), :]
            o16 = o16_ref[g, pl.ds(r0, BLOCK), :]
            l16 = l16_ref[g, pl.ds(r0, BLOCK), :]
            top = jnp.maximum(jnp.maximum(l1, l4), l16)
            e1 = jnp.exp(l1 - top)
            e4 = jnp.exp(l4 - top)
            e16 = jnp.exp(l16 - top)
            merged = (e1 * o1 + e4 * o4 + e16 * o16) / (e1 + e4 + e16)
            o_ref[0, pl.ds(r0, BLOCK), c:c + PAIR] = merged.astype(bf16)

    block1(0, 0, BLOCK, bias_first)

    def body1(i, carry):
        r0 = pl.multiple_of(i * BLOCK, BLOCK)
        block1(r0, pl.multiple_of(r0 - BLOCK, BLOCK), 2 * BLOCK, bias_ref[...])
        return carry

    lax.fori_loop(1, seq_len // BLOCK, body1, 0)


def _dilated(views):
    B, S, _ = views[0].shape
    assert all(w // r == BLOCK for w, r in DIL_PAIRS) and S // DIL_PAIRS[2][1] == BLOCK
    bias_first = jnp.asarray(_band_bias(BLOCK, BLOCK))
    bias = jnp.asarray(_band_bias(2 * BLOCK, BLOCK))
    view_specs = []
    for _, r in DIL_PAIRS:
        view_specs += [pl.BlockSpec((1, S // r, r * W_B), lambda b: (b, 0, 0))] * 3
    const = lambda shape: pl.BlockSpec(shape, lambda b: (0,) * len(shape))
    slab = (W_B // PAIR, S, V7X_LANES)
    pipelined = 10 * _nbytes((S, W_B), bf16) + _nbytes((BLOCK, 3 * BLOCK), f32)
    return pl.pallas_call(
        _dilated_kernel,
        grid=(B,),
        in_specs=view_specs + [const((BLOCK, BLOCK)), const((BLOCK, 2 * BLOCK))],
        out_specs=pl.BlockSpec((1, S, W_B), lambda b: (b, 0, 0)),
        out_shape=jax.ShapeDtypeStruct((B, S, W_B), bf16),
        scratch_shapes=[pltpu.VMEM(slab, f32) for _ in range(4)],
        compiler_params=pltpu.CompilerParams(
            dimension_semantics=("arbitrary",),
            vmem_limit_bytes=_vmem_limit(pipelined, 4 * _nbytes(slab, f32))),
        name="dilated_attention",
    )(*views, bias_first, bias)


def _mem_kernel(q_ref, mem_ref, w_ref, o_ref):
    mkv = jnp.dot(mem_ref[0].astype(bf16), w_ref[...], preferred_element_type=f32).astype(bf16)
    mk, mv = mkv[:, :W_C], mkv[:, W_C:]

    def body(i, carry):
        r0 = pl.multiple_of(i * BLOCK, BLOCK)
        for g in range(W_C // PAIR):
            c = g * PAIR
            out, _ = _pair_attention(q_ref[0, pl.ds(r0, BLOCK), c:c + PAIR],
                                     mk[:, c:c + PAIR], mv[:, c:c + PAIR], None)
            o_ref[0, pl.ds(r0, BLOCK), c:c + PAIR] = out.astype(bf16)
        return carry

    lax.fori_loop(0, q_ref.shape[1] // BLOCK, body, 0)


def _mem_attention(qc, mem, w_mem):
    B, S, _ = qc.shape
    M = mem.shape[1]
    pipelined = (2 * _nbytes((S, W_C), bf16) + _nbytes((M, D_MODEL), f32)
                 + _nbytes((D_MODEL, 2 * W_C), bf16))
    return pl.pallas_call(
        _mem_kernel,
        grid=(B,),
        in_specs=[pl.BlockSpec((1, S, W_C), lambda b: (b, 0, 0)),
                  pl.BlockSpec((1, M, D_MODEL), lambda b: (b, 0, 0)),
                  pl.BlockSpec((D_MODEL, 2 * W_C), lambda b: (0, 0))],
        out_specs=pl.BlockSpec((1, S, W_C), lambda b: (b, 0, 0)),
        out_shape=jax.ShapeDtypeStruct((B, S, W_C), bf16),
        compiler_params=pltpu.CompilerParams(
            dimension_semantics=("arbitrary",), vmem_limit_bytes=_vmem_limit(pipelined)),
        name="memory_attention",
    )(qc, mem, w_mem)


def _out_kernel(oa_ref, ob_ref, oc_ref, gate_ref, x_ref, w_ref, gain_ref, bias_ref, out_ref):
    def rms_normalize(t):
        t = t.astype(f32)
        return t * lax.rsqrt(jnp.mean(t * t, axis=-1, keepdims=True) + RMS_EPS)

    y = jnp.concatenate([rms_normalize(oa_ref[0]), rms_normalize(ob_ref[0]),
                         rms_normalize(oc_ref[0])], axis=-1)
    y = (y * gate_ref[0].astype(f32)).astype(bf16)
    r = DEEPNORM_ALPHA * x_ref[0] + jnp.dot(y, w_ref[...], preferred_element_type=f32)
    mu = jnp.mean(r, axis=-1, keepdims=True)
    d = r - mu
    var = jnp.mean(d * d, axis=-1, keepdims=True)
    out_ref[0] = d * lax.rsqrt(var + LN_EPS) * gain_ref[...] + bias_ref[...]


def _out_proj(oa, ob, oc, gate, x, w_out, ln_gain, ln_bias):
    B, S, _ = x.shape
    tm = ROW_TILE
    row_block = lambda width: pl.BlockSpec((1, tm, width), lambda b, i: (b, i, 0))
    const = lambda shape: pl.BlockSpec(shape, lambda b, i: (0,) * len(shape))
    pipelined = (2 * _nbytes((tm, D_MIX), bf16) + 2 * _nbytes((tm, D_MODEL), f32)
                 + _nbytes((D_MIX, D_MODEL), bf16))
    return pl.pallas_call(
        _out_kernel,
        grid=(B, S // tm),
        in_specs=[row_block(W_A), row_block(W_B), row_block(W_C), row_block(D_MIX),
                  row_block(D_MODEL), const((D_MIX, D_MODEL)),
                  const((1, D_MODEL)), const((1, D_MODEL))],
        out_specs=row_block(D_MODEL),
        out_shape=jax.ShapeDtypeStruct((B, S, D_MODEL), f32),
        compiler_params=pltpu.CompilerParams(
            dimension_semantics=("arbitrary", "arbitrary"),
            vmem_limit_bytes=_vmem_limit(pipelined, 3 * _nbytes((tm, D_MODEL), f32))),
        name="out_proj",
    )(oa, ob, oc, gate, x, w_out, ln_gain, ln_bias)


def _rope_tables(seq_len):
    pos = jnp.arange(seq_len, dtype=f32)
    inv = ROPE_THETA ** (-jnp.arange(0, HEAD_DIM, 2, dtype=f32) / HEAD_DIM)
    ang = pos[:, None] * inv[None, :]
    cos, sin = jnp.cos(ang), jnp.sin(ang)
    cos2 = jnp.concatenate([cos, cos, cos, cos], axis=-1)
    sin2 = jnp.concatenate([-sin, sin, -sin, sin], axis=-1)
    return cos2, sin2


def kernel(x, mem, w_in, b_in, w_mem, attn_sinks, g_branch, w_out, ln_gain, ln_bias):
    assert w_in.shape[0] == DEPTH
    S = x.shape[1]
    cos2, sin2 = _rope_tables(S)
    for l in range(DEPTH):
        qa, ka, va, qc, gate, *views_b = _in_proj(
            x, w_in[l].astype(bf16), b_in[l][None, :], cos2, sin2, g_branch[l][None, :])
        oa = _swa(attn_sinks[l], qa, ka, va)
        ob = _dilated(views_b)
        oc = _mem_attention(qc, mem, w_mem[l].astype(bf16))
        x = _out_proj(oa, ob, oc, gate, x, w_out[l].astype(bf16),
                      ln_gain[l][None, :], ln_bias[l][None, :])
    return x
```

```python
import functools

import numpy as np
import jax
import jax.numpy as jnp
from jax import lax
from jax.experimental import pallas as pl
from jax.experimental.pallas import tpu as pltpu

D_MODEL = 1024
HEAD_DIM = 64
SWA_Q_HEADS = 8
SWA_KV_HEADS = 2
SWA_WINDOW = 128
DIL_HEADS = 4
DIL_PAIRS = ((128, 1), (512, 4), (2048, 16))
MEM_HEADS = 4
BLOCK = 128
ROPE_THETA = 10000.0
LN_EPS = 1e-5
RMS_EPS = 1e-6
DEPTH = 1
DEEPNORM_ALPHA = (2 * DEPTH) ** 0.25

W_A = SWA_Q_HEADS * HEAD_DIM
W_KV_A = SWA_KV_HEADS * HEAD_DIM
W_B = DIL_HEADS * HEAD_DIM
W_C = MEM_HEADS * HEAD_DIM
D_MIX = W_A + W_B + W_C
D_IN = W_A + 2 * W_KV_A + 3 * W_B + W_C + D_MIX

V7X_LANES = 128
V7X_VMEM_BYTES = 64 * 2**20
VMEM_HEADROOM_BYTES = 8 * 2**20

PAIR = 2 * HEAD_DIM
assert PAIR == V7X_LANES
SCALE = HEAD_DIM ** -0.5
MASKED = -1e30

OFF_QA = 0
OFF_KA = OFF_QA + W_A
OFF_VA = OFF_KA + W_KV_A
OFF_QB = OFF_VA + W_KV_A
OFF_KB = OFF_QB + W_B
OFF_VB = OFF_KB + W_B
OFF_QC = OFF_VB + W_B
OFF_Z = OFF_QC + W_C
assert OFF_Z + D_MIX == D_IN

ROW_TILE = 512
SWA_BLOCKS_PER_STEP = 2
DIL_BLOCKS_PER_STEP = 4
MEM_BLOCKS_PER_STEP = 4

f32 = jnp.float32
bf16 = jnp.bfloat16


def _vmem_limit(pipelined_bytes, scratch_bytes=0):
    budget = V7X_VMEM_BYTES - VMEM_HEADROOM_BYTES
    assert 2 * pipelined_bytes + scratch_bytes + VMEM_HEADROOM_BYTES <= budget
    return budget


def _nbytes(shape, dtype):
    return int(np.prod(shape)) * jnp.dtype(dtype).itemsize


def _lane_is_first_head():
    return lax.broadcasted_iota(jnp.int32, (1, V7X_LANES), 1) < HEAD_DIM


def _in_proj_kernel(x_ref, w_ref, b_ref, cos_ref, sin_ref, g_ref,
                    qa_ref, ka_ref, va_ref, qc_ref, gate_ref,
                    q1_ref, k1_ref, v1_ref, q4_ref, k4_ref, v4_ref, q16_ref, k16_ref, v16_ref,
                    stage_ref):
    tm = x_ref.shape[1]
    xb = x_ref[0].astype(bf16)
    cos = cos_ref[...]
    sin = sin_ref[...]
    lane = lax.broadcasted_iota(jnp.int32, (1, V7X_LANES), 1)
    first_half = (lane % HEAD_DIM) < (HEAD_DIM // 2)
    first_head = lane < HEAD_DIM

    def proj(off, width):
        acc = jnp.dot(xb, w_ref[:, off:off + width], preferred_element_type=f32)
        return acc + b_ref[:, off:off + width]

    def rope(a):
        rot = jnp.where(first_half,
                        pltpu.roll(a, V7X_LANES - HEAD_DIM // 2, 1),
                        pltpu.roll(a, HEAD_DIM // 2, 1))
        return a * cos + rot * sin

    def groups(a):
        return [a[:, i:i + V7X_LANES] for i in range(0, a.shape[1], V7X_LANES)]

    def both_halves(a):
        swapped = pltpu.roll(a, HEAD_DIM, 1)
        return jnp.where(first_head, a, swapped), jnp.where(first_head, swapped, a)

    for c in range(0, W_A, 2 * V7X_LANES):
        for i, a in enumerate(groups(proj(OFF_QA + c, 2 * V7X_LANES))):
            lo = c + i * V7X_LANES
            qa_ref[0, :, lo:lo + V7X_LANES] = (rope(a) * SCALE).astype(bf16)
    k_a, v_a = groups(proj(OFF_KA, 2 * W_KV_A))
    for ref, a in ((ka_ref, rope(k_a)), (va_ref, v_a)):
        kv0, kv1 = both_halves(a)
        ref[0, :, 0:V7X_LANES] = kv0.astype(bf16)
        ref[0, :, V7X_LANES:2 * V7X_LANES] = kv1.astype(bf16)
    mixer_b = ((OFF_QB, lambda a: rope(a) * SCALE, (q1_ref, q4_ref, q16_ref)),
               (OFF_KB, rope, (k1_ref, k4_ref, k16_ref)),
               (OFF_VB, lambda a: a, (v1_ref, v4_ref, v16_ref)))
    for t, (off, fn, (nat_ref, r4_ref, r16_ref)) in enumerate(mixer_b):
        for g, a in enumerate(groups(proj(off, W_B))):
            a = fn(a)
            nat_ref[0, :, g * PAIR:(g + 1) * PAIR] = a.astype(bf16)
            slab = t * (W_B // PAIR) + g
            stage_ref[slab] = a
            for view_ref, dil in ((r4_ref, DIL_PAIRS[1][1]), (r16_ref, DIL_PAIRS[2][1])):
                for j in range(dil):
                    stream = stage_ref[slab, pl.ds(j, tm // dil, stride=dil), :]
                    c = j * W_B + g * PAIR
                    view_ref[0, :, c:c + PAIR] = stream.astype(bf16)
    qc_ref[0] = (proj(OFF_QC, W_C) * SCALE).astype(bf16)
    for c in range(0, D_MIX, 2 * V7X_LANES):
        z = proj(OFF_Z + c, 2 * V7X_LANES)
        gate = z * jax.nn.sigmoid(z) * g_ref[:, c:c + 2 * V7X_LANES]
        gate_ref[0, :, c:c + 2 * V7X_LANES] = gate.astype(bf16)


def _in_proj(x, w_in, b_in, cos2, sin2, g_branch):
    B, S, _ = x.shape
    tm = ROW_TILE
    widths = (W_A, 2 * W_KV_A, 2 * W_KV_A, W_C, D_MIX)
    out_blocks = [(tm, w) for w in widths]
    for _, r in DIL_PAIRS:
        out_blocks += [(tm // r, r * W_B)] * 3
    row_block = lambda rows, width: pl.BlockSpec((1, rows, width), lambda b, i: (b, i, 0))
    const = lambda shape: pl.BlockSpec(shape, lambda b, i: (0,) * len(shape))
    stage = (3 * (W_B // PAIR), tm, V7X_LANES)
    pipelined = (_nbytes((tm, D_MODEL), f32) + _nbytes((D_MODEL, D_IN), bf16)
                 + sum(_nbytes(blk, bf16) for blk in out_blocks) + 2 * _nbytes((tm, V7X_LANES), f32))
    return pl.pallas_call(
        _in_proj_kernel,
        grid=(B, S // tm),
        in_specs=[row_block(tm, D_MODEL), const((D_MODEL, D_IN)), const((1, D_IN)),
                  pl.BlockSpec((tm, V7X_LANES), lambda b, i: (i, 0)),
                  pl.BlockSpec((tm, V7X_LANES), lambda b, i: (i, 0)),
                  const((1, D_MIX))],
        out_specs=[row_block(*blk) for blk in out_blocks],
        out_shape=[jax.ShapeDtypeStruct((B, S * rows // tm, w), bf16) for rows, w in out_blocks],
        scratch_shapes=[pltpu.VMEM(stage, f32)],
        compiler_params=pltpu.CompilerParams(
            dimension_semantics=("arbitrary", "arbitrary"),
            vmem_limit_bytes=_vmem_limit(pipelined, _nbytes(stage, f32) + _nbytes((tm, D_IN), f32))),
        name="in_proj",
    )(x, w_in, b_in, cos2, sin2, g_branch)


def _pair_attention(q2, k2, v2, bias, sinks=None):
    m_rows = q2.shape[0]
    first_head = _lane_is_first_head()
    zero = jnp.zeros_like(q2)
    stacked = jnp.concatenate([jnp.where(first_head, q2, zero),
                               jnp.where(first_head, zero, q2)], axis=0)
    s = lax.dot_general(stacked, k2, (((1,), (1,)), ((), ())),
                        preferred_element_type=f32)
    probs, inv_l, lses = [], [], []
    for h in range(2):
        sh = s[h * m_rows:(h + 1) * m_rows]
        if bias is not None:
            sh = sh + bias
        m = jnp.max(sh, axis=-1, keepdims=True)
        if sinks is not None:
            m = jnp.maximum(m, sinks[h])
        p = jnp.exp(sh - m)
        l = jnp.sum(p, axis=-1, keepdims=True)
        if sinks is not None:
            l = l + jnp.exp(sinks[h] - m)
        probs.append(p.astype(bf16))
        inv_l.append(1.0 / l)
        lses.append(m + jnp.log(l))
    o = jnp.dot(jnp.concatenate(probs, axis=0), v2, preferred_element_type=f32)
    out = jnp.where(first_head, o[:m_rows] * inv_l[0], o[m_rows:] * inv_l[1])
    lse = jnp.where(first_head, lses[0], lses[1])
    return out, lse


def _window_bias(max_dist):
    qi = np.arange(BLOCK)[:, None]
    kj = np.arange(2 * BLOCK)[None, :]
    dist = np.stack([qi - kj, qi + BLOCK - kj])
    return np.where((dist >= 0) & (dist <= max_dist), 0.0, MASKED).astype(np.float32)


def _window(blk):
    r0 = pl.multiple_of(blk * BLOCK, BLOCK)
    k0 = pl.multiple_of(jnp.maximum(blk - 1, 0) * BLOCK, BLOCK)
    return r0, k0, jnp.minimum(blk, 1)


def _swa_kernel(sink_ref, q_ref, k_ref, v_ref, bias_ref, o_ref):
    n_blk = q_ref.shape[1] // BLOCK
    groups_per_kv = (SWA_Q_HEADS // SWA_KV_HEADS) // 2

    def body(i, carry):
        for u in range(SWA_BLOCKS_PER_STEP):
            r0, k0, first = _window(i * SWA_BLOCKS_PER_STEP + u)
            bias = bias_ref[first]
            q = q_ref[0, pl.ds(r0, BLOCK), :]
            k = k_ref[0, pl.ds(k0, 2 * BLOCK), :]
            v = v_ref[0, pl.ds(k0, 2 * BLOCK), :]
            for g in range(W_A // PAIR):
                kv = (g // groups_per_kv) * PAIR
                out, _ = _pair_attention(q[:, g * PAIR:(g + 1) * PAIR],
                                         k[:, kv:kv + PAIR], v[:, kv:kv + PAIR], bias,
                                         sinks=(sink_ref[2 * g], sink_ref[2 * g + 1]))
                o_ref[0, pl.ds(r0, BLOCK), g * PAIR:(g + 1) * PAIR] = out.astype(bf16)
        return carry

    lax.fori_loop(0, n_blk // SWA_BLOCKS_PER_STEP, body, 0)


def _swa(sinks, qa, ka, va):
    B, S, _ = qa.shape
    bias = jnp.asarray(_window_bias(SWA_WINDOW - 1))
    seq = lambda width: pl.BlockSpec((1, S, width), lambda b: (b, 0, 0))
    pipelined = (_nbytes((S, 2 * W_A), bf16) + 2 * _nbytes((S, 2 * W_KV_A), bf16)
                 + _nbytes(bias.shape, f32))
    return pl.pallas_call(
        _swa_kernel,
        grid=(B,),
        in_specs=[pl.BlockSpec(memory_space=pltpu.SMEM),
                  seq(W_A), seq(2 * W_KV_A), seq(2 * W_KV_A),
                  pl.BlockSpec(bias.shape, lambda b: (0, 0, 0))],
        out_specs=seq(W_A),
        out_shape=jax.ShapeDtypeStruct((B, S, W_A), bf16),
        compiler_params=pltpu.CompilerParams(
            dimension_semantics=("arbitrary",), vmem_limit_bytes=_vmem_limit(pipelined)),
        name="swa_attention",
    )(sinks, qa, ka, va, bias)


def _dilated_kernel(q1_ref, k1_ref, v1_ref, q4_ref, k4_ref, v4_ref, q16_ref, k16_ref, v16_ref,
                    bias_ref, o_ref, o4_ref, l4_ref, o16_ref, l16_ref):
    seq_len = q1_ref.shape[1]
    n_pairs = W_B // PAIR

    dil = DIL_PAIRS[2][1]
    causal = bias_ref[0, :, 0:BLOCK]
    for j in range(dil):
        for g in range(n_pairs):
            c = j * W_B + g * PAIR
            out, lse = _pair_attention(q16_ref[0, :, c:c + PAIR], k16_ref[0, :, c:c + PAIR],
                                       v16_ref[0, :, c:c + PAIR], causal)
            o16_ref[g, pl.ds(j, BLOCK, stride=dil), :] = out
            l16_ref[g, pl.ds(j, BLOCK, stride=dil), :] = lse

    dil = DIL_PAIRS[1][1]

    def body4(blk, carry):
        u0, k0, first = _window(blk)
        bias = bias_ref[first]
        for j in range(dil):
            for g in range(n_pairs):
                c = j * W_B + g * PAIR
                out, lse = _pair_attention(q4_ref[0, pl.ds(u0, BLOCK), c:c + PAIR],
                                           k4_ref[0, pl.ds(k0, 2 * BLOCK), c:c + PAIR],
                                           v4_ref[0, pl.ds(k0, 2 * BLOCK), c:c + PAIR], bias)
                o4_ref[g, pl.ds(u0 * dil + j, BLOCK, stride=dil), :] = out
                l4_ref[g, pl.ds(u0 * dil + j, BLOCK, stride=dil), :] = lse
        return carry

    lax.fori_loop(0, seq_len // dil // BLOCK, body4, 0)

    def body1(i, carry):
        for u in range(DIL_BLOCKS_PER_STEP):
            r0, k0, first = _window(i * DIL_BLOCKS_PER_STEP + u)
            bias = bias_ref[first]
            for g in range(n_pairs):
                c = g * PAIR
                o1, l1 = _pair_attention(q1_ref[0, pl.ds(r0, BLOCK), c:c + PAIR],
                                         k1_ref[0, pl.ds(k0, 2 * BLOCK), c:c + PAIR],
                                         v1_ref[0, pl.ds(k0, 2 * BLOCK), c:c + PAIR], bias)
                o4 = o4_ref[g, pl.ds(r0, BLOCK), :]
                l4 = l4_ref[g, pl.ds(r0, BLOCK), :]
                o16 = o16_ref[g, pl.ds(r0, BLOCK), :]
                l16 = l16_ref[g, pl.ds(r0, BLOCK), :]
                top = jnp.maximum(jnp.maximum(l1, l4), l16)
                e1 = jnp.exp(l1 - top)
                e4 = jnp.exp(l4 - top)
                e16 = jnp.exp(l16 - top)
                merged = (e1 * o1 + e4 * o4 + e16 * o16) / (e1 + e4 + e16)
                o_ref[0, pl.ds(r0, BLOCK), c:c + PAIR] = merged.astype(bf16)
        return carry

    lax.fori_loop(0, seq_len // BLOCK // DIL_BLOCKS_PER_STEP, body1, 0)


def _dilated(views):
    B, S, _ = views[0].shape
    assert all(w // r == BLOCK for w, r in DIL_PAIRS) and S // DIL_PAIRS[2][1] == BLOCK
    bias = jnp.asarray(_window_bias(BLOCK))
    view_specs = []
    for _, r in DIL_PAIRS:
        view_specs += [pl.BlockSpec((1, S // r, r * W_B), lambda b: (b, 0, 0))] * 3
    slab = (W_B // PAIR, S, V7X_LANES)
    pipelined = 10 * _nbytes((S, W_B), bf16) + _nbytes(bias.shape, f32)
    return pl.pallas_call(
        _dilated_kernel,
        grid=(B,),
        in_specs=view_specs + [pl.BlockSpec(bias.shape, lambda b: (0, 0, 0))],
        out_specs=pl.BlockSpec((1, S, W_B), lambda b: (b, 0, 0)),
        out_shape=jax.ShapeDtypeStruct((B, S, W_B), bf16),
        scratch_shapes=[pltpu.VMEM(slab, f32) for _ in range(4)],
        compiler_params=pltpu.CompilerParams(
            dimension_semantics=("arbitrary",),
            vmem_limit_bytes=_vmem_limit(pipelined, 4 * _nbytes(slab, f32))),
        name="dilated_attention",
    )(*views, bias)


def _mem_kernel(q_ref, mem_ref, w_ref, o_ref):
    mkv = jnp.dot(mem_ref[0].astype(bf16), w_ref[...], preferred_element_type=f32).astype(bf16)
    mk, mv = mkv[:, :W_C], mkv[:, W_C:]

    def body(i, carry):
        for u in range(MEM_BLOCKS_PER_STEP):
            r0 = pl.multiple_of((i * MEM_BLOCKS_PER_STEP + u) * BLOCK, BLOCK)
            for g in range(W_C // PAIR):
                c = g * PAIR
                out, _ = _pair_attention(q_ref[0, pl.ds(r0, BLOCK), c:c + PAIR],
                                         mk[:, c:c + PAIR], mv[:, c:c + PAIR], None)
                o_ref[0, pl.ds(r0, BLOCK), c:c + PAIR] = out.astype(bf16)
        return carry

    lax.fori_loop(0, q_ref.shape[1] // BLOCK // MEM_BLOCKS_PER_STEP, body, 0)


def _mem_attention(qc, mem, w_mem):
    B, S, _ = qc.shape
    M = mem.shape[1]
    pipelined = (2 * _nbytes((S, W_C), bf16) + _nbytes((M, D_MODEL), f32)
                 + _nbytes((D_MODEL, 2 * W_C), bf16))
    return pl.pallas_call(
        _mem_kernel,
        grid=(B,),
        in_specs=[pl.BlockSpec((1, S, W_C), lambda b: (b, 0, 0)),
                  pl.BlockSpec((1, M, D_MODEL), lambda b: (b, 0, 0)),
                  pl.BlockSpec((D_MODEL, 2 * W_C), lambda b: (0, 0))],
        out_specs=pl.BlockSpec((1, S, W_C), lambda b: (b, 0, 0)),
        out_shape=jax.ShapeDtypeStruct((B, S, W_C), bf16),
        compiler_params=pltpu.CompilerParams(
            dimension_semantics=("arbitrary",), vmem_limit_bytes=_vmem_limit(pipelined)),
        name="memory_attention",
    )(qc, mem, w_mem)


def _out_kernel(oa_ref, ob_ref, oc_ref, gate_ref, x_ref, w_ref, gain_ref, bias_ref, out_ref):
    def rms_normalize(t):
        t = t.astype(f32)
        return t * lax.rsqrt(jnp.mean(t * t, axis=-1, keepdims=True) + RMS_EPS)

    y = jnp.concatenate([rms_normalize(oa_ref[0]), rms_normalize(ob_ref[0]),
                         rms_normalize(oc_ref[0])], axis=-1)
    y = (y * gate_ref[0].astype(f32)).astype(bf16)
    r = DEEPNORM_ALPHA * x_ref[0] + jnp.dot(y, w_ref[...], preferred_element_type=f32)
    mu = jnp.mean(r, axis=-1, keepdims=True)
    d = r - mu
    var = jnp.mean(d * d, axis=-1, keepdims=True)
    out_ref[0] = d * lax.rsqrt(var + LN_EPS) * gain_ref[...] + bias_ref[...]


def _out_proj(oa, ob, oc, gate, x, w_out, ln_gain, ln_bias):
    B, S, _ = x.shape
    tm = ROW_TILE
    row_block = lambda width: pl.BlockSpec((1, tm, width), lambda b, i: (b, i, 0))
    const = lambda shape: pl.BlockSpec(shape, lambda b, i: (0,) * len(shape))
    pipelined = (2 * _nbytes((tm, D_MIX), bf16) + 2 * _nbytes((tm, D_MODEL), f32)
                 + _nbytes((D_MIX, D_MODEL), bf16))
    return pl.pallas_call(
        _out_kernel,
        grid=(B, S // tm),
        in_specs=[row_block(W_A), row_block(W_B), row_block(W_C), row_block(D_MIX),
                  row_block(D_MODEL), const((D_MIX, D_MODEL)),
                  const((1, D_MODEL)), const((1, D_MODEL))],
        out_specs=row_block(D_MODEL),
        out_shape=jax.ShapeDtypeStruct((B, S, D_MODEL), f32),
        compiler_params=pltpu.CompilerParams(
            dimension_semantics=("arbitrary", "arbitrary"),
            vmem_limit_bytes=_vmem_limit(pipelined, 3 * _nbytes((tm, D_MODEL), f32))),
        name="out_proj",
    )(oa, ob, oc, gate, x, w_out, ln_gain, ln_bias)


def _rope_tables(seq_len):
    pos = jnp.arange(seq_len, dtype=f32)
    inv = ROPE_THETA ** (-jnp.arange(0, HEAD_DIM, 2, dtype=f32) / HEAD_DIM)
    ang = pos[:, None] * inv[None, :]
    cos, sin = jnp.cos(ang), jnp.sin(ang)
    cos2 = jnp.concatenate([cos, cos, cos, cos], axis=-1)
    sin2 = jnp.concatenate([-sin, sin, -sin, sin], axis=-1)
    return cos2, sin2


def kernel(x, mem, w_in, b_in, w_mem, attn_sinks, g_branch, w_out, ln_gain, ln_bias):
    assert w_in.shape[0] == DEPTH
    S = x.shape[1]
    cos2, sin2 = _rope_tables(S)
    for l in range(DEPTH):
        qa, ka, va, qc, gate, *views_b = _in_proj(
            x, w_in[l].astype(bf16), b_in[l][None, :], cos2, sin2, g_branch[l][None, :])
        oa = _swa(attn_sinks[l], qa, ka, va)
        ob = _dilated(views_b)
        oc = _mem_attention(qc, mem, w_mem[l].astype(bf16))
        x = _out_proj(oa, ob, oc, gate, x, w_out[l].astype(bf16),
                      ln_gain[l][None, :], ln_bias[l][None, :])
    return x
```

```python
import functools

import numpy as np
import jax
import jax.numpy as jnp
from jax import lax
from jax.experimental import pallas as pl
from jax.experimental.pallas import tpu as pltpu

D_MODEL = 1024
HEAD_DIM = 64
SWA_Q_HEADS = 8
SWA_KV_HEADS = 2
SWA_WINDOW = 128
DIL_HEADS = 4
DIL_PAIRS = ((128, 1), (512, 4), (2048, 16))
MEM_HEADS = 4
BLOCK = 128
ROPE_THETA = 10000.0
LN_EPS = 1e-5
RMS_EPS = 1e-6
DEPTH = 1
DEEPNORM_ALPHA = (2 * DEPTH) ** 0.25

W_A = SWA_Q_HEADS * HEAD_DIM
W_KV_A = SWA_KV_HEADS * HEAD_DIM
W_B = DIL_HEADS * HEAD_DIM
W_C = MEM_HEADS * HEAD_DIM
D_MIX = W_A + W_B + W_C
D_IN = W_A + 2 * W_KV_A + 3 * W_B + W_C + D_MIX

V7X_LANES = 128
V7X_VMEM_BYTES = 64 * 2**20
VMEM_HEADROOM_BYTES = 8 * 2**20

PAIR = 2 * HEAD_DIM
assert PAIR == V7X_LANES
SCALE = HEAD_DIM ** -0.5
MASKED = -1e30

OFF_QA = 0
OFF_KA = OFF_QA + W_A
OFF_VA = OFF_KA + W_KV_A
OFF_QB = OFF_VA + W_KV_A
OFF_KB = OFF_QB + W_B
OFF_VB = OFF_KB + W_B
OFF_QC = OFF_VB + W_B
OFF_Z = OFF_QC + W_C
assert OFF_Z + D_MIX == D_IN

ROW_TILE = 512
DIL_BLOCKS_PER_STEP = 4

f32 = jnp.float32
bf16 = jnp.bfloat16


def _vmem_limit(pipelined_bytes, scratch_bytes=0):
    budget = V7X_VMEM_BYTES - VMEM_HEADROOM_BYTES
    assert 2 * pipelined_bytes + scratch_bytes + VMEM_HEADROOM_BYTES <= budget
    return budget


def _nbytes(shape, dtype):
    return int(np.prod(shape)) * jnp.dtype(dtype).itemsize


def _lane_is_first_head():
    return lax.broadcasted_iota(jnp.int32, (1, V7X_LANES), 1) < HEAD_DIM


def _in_proj_kernel(x_ref, w_ref, b_ref, cos_ref, sin_ref, g_ref,
                    qa_ref, ka_ref, va_ref, qc_ref, gate_ref,
                    q1_ref, k1_ref, v1_ref, q4_ref, k4_ref, v4_ref, q16_ref, k16_ref, v16_ref,
                    stage_ref):
    tm = x_ref.shape[1]
    xb = x_ref[0].astype(bf16)
    cos = cos_ref[...]
    sin = sin_ref[...]
    lane = lax.broadcasted_iota(jnp.int32, (1, V7X_LANES), 1)
    first_half = (lane % HEAD_DIM) < (HEAD_DIM // 2)
    first_head = lane < HEAD_DIM

    def proj(off, width):
        acc = jnp.dot(xb, w_ref[:, off:off + width], preferred_element_type=f32)
        return acc + b_ref[:, off:off + width]

    def rope(a):
        rot = jnp.where(first_half,
                        pltpu.roll(a, V7X_LANES - HEAD_DIM // 2, 1),
                        pltpu.roll(a, HEAD_DIM // 2, 1))
        return a * cos + rot * sin

    def groups(a):
        return [a[:, i:i + V7X_LANES] for i in range(0, a.shape[1], V7X_LANES)]

    def both_halves(a):
        swapped = pltpu.roll(a, HEAD_DIM, 1)
        return jnp.where(first_head, a, swapped), jnp.where(first_head, swapped, a)

    for c in range(0, W_A, 2 * V7X_LANES):
        for i, a in enumerate(groups(proj(OFF_QA + c, 2 * V7X_LANES))):
            lo = c + i * V7X_LANES
            qa_ref[0, :, lo:lo + V7X_LANES] = (rope(a) * SCALE).astype(bf16)
    k_a, v_a = groups(proj(OFF_KA, 2 * W_KV_A))
    for ref, a in ((ka_ref, rope(k_a)), (va_ref, v_a)):
        kv0, kv1 = both_halves(a)
        ref[0, :, 0:V7X_LANES] = kv0.astype(bf16)
        ref[0, :, V7X_LANES:2 * V7X_LANES] = kv1.astype(bf16)
    mixer_b = ((OFF_QB, lambda a: rope(a) * SCALE, (q1_ref, q4_ref, q16_ref)),
               (OFF_KB, rope, (k1_ref, k4_ref, k16_ref)),
               (OFF_VB, lambda a: a, (v1_ref, v4_ref, v16_ref)))
    for t, (off, fn, (nat_ref, r4_ref, r16_ref)) in enumerate(mixer_b):
        for g, a in enumerate(groups(proj(off, W_B))):
            a = fn(a)
            nat_ref[0, :, g * PAIR:(g + 1) * PAIR] = a.astype(bf16)
            slab = t * (W_B // PAIR) + g
            stage_ref[slab] = a
            for view_ref, dil in ((r4_ref, DIL_PAIRS[1][1]), (r16_ref, DIL_PAIRS[2][1])):
                for j in range(dil):
                    stream = stage_ref[slab, pl.ds(j, tm // dil, stride=dil), :]
                    c = j * W_B + g * PAIR
                    view_ref[0, :, c:c + PAIR] = stream.astype(bf16)
    qc_ref[0] = (proj(OFF_QC, W_C) * SCALE).astype(bf16)
    for c in range(0, D_MIX, 2 * V7X_LANES):
        z = proj(OFF_Z + c, 2 * V7X_LANES)
        gate = z * jax.nn.sigmoid(z) * g_ref[:, c:c + 2 * V7X_LANES]
        gate_ref[0, :, c:c + 2 * V7X_LANES] = gate.astype(bf16)


def _in_proj(x, w_in, b_in, cos2, sin2, g_branch):
    B, S, _ = x.shape
    tm = ROW_TILE
    widths = (W_A, 2 * W_KV_A, 2 * W_KV_A, W_C, D_MIX)
    out_blocks = [(tm, w) for w in widths]
    for _, r in DIL_PAIRS:
        out_blocks += [(tm // r, r * W_B)] * 3
    row_block = lambda rows, width: pl.BlockSpec((1, rows, width), lambda b, i: (b, i, 0))
    const = lambda shape: pl.BlockSpec(shape, lambda b, i: (0,) * len(shape))
    stage = (3 * (W_B // PAIR), tm, V7X_LANES)
    pipelined = (_nbytes((tm, D_MODEL), f32) + _nbytes((D_MODEL, D_IN), bf16)
                 + sum(_nbytes(blk, bf16) for blk in out_blocks) + 2 * _nbytes((tm, V7X_LANES), f32))
    return pl.pallas_call(
        _in_proj_kernel,
        grid=(B, S // tm),
        in_specs=[row_block(tm, D_MODEL), const((D_MODEL, D_IN)), const((1, D_IN)),
                  pl.BlockSpec((tm, V7X_LANES), lambda b, i: (i, 0)),
                  pl.BlockSpec((tm, V7X_LANES), lambda b, i: (i, 0)),
                  const((1, D_MIX))],
        out_specs=[row_block(*blk) for blk in out_blocks],
        out_shape=[jax.ShapeDtypeStruct((B, S * rows // tm, w), bf16) for rows, w in out_blocks],
        scratch_shapes=[pltpu.VMEM(stage, f32)],
        compiler_params=pltpu.CompilerParams(
            dimension_semantics=("arbitrary", "arbitrary"),
            vmem_limit_bytes=_vmem_limit(pipelined, _nbytes(stage, f32))),
        name="in_proj",
    )(x, w_in, b_in, cos2, sin2, g_branch)


def _pair_attention(q2, k2, v2, bias, sinks=None):
    m_rows = q2.shape[0]
    first_head = _lane_is_first_head()
    zero = jnp.zeros_like(q2)
    stacked = jnp.concatenate([jnp.where(first_head, q2, zero),
                               jnp.where(first_head, zero, q2)], axis=0)
    s = lax.dot_general(stacked, k2, (((1,), (1,)), ((), ())),
                        preferred_element_type=f32)
    probs, inv_l, lses = [], [], []
    for h in range(2):
        sh = s[h * m_rows:(h + 1) * m_rows]
        if bias is not None:
            sh = sh + bias
        m = jnp.max(sh, axis=-1, keepdims=True)
        if sinks is not None:
            m = jnp.maximum(m, sinks[h])
        p = jnp.exp(sh - m)
        l = jnp.sum(p, axis=-1, keepdims=True)
        if sinks is not None:
            l = l + jnp.exp(sinks[h] - m)
        probs.append(p.astype(bf16))
        inv_l.append(1.0 / l)
        lses.append(m + jnp.log(l))
    o = jnp.dot(jnp.concatenate(probs, axis=0), v2, preferred_element_type=f32)
    out = jnp.where(first_head, o[:m_rows] * inv_l[0], o[m_rows:] * inv_l[1])
    lse = jnp.where(first_head, lses[0], lses[1])
    return out, lse


def _window_bias(max_dist):
    qi = np.arange(BLOCK)[:, None]
    kj = np.arange(2 * BLOCK)[None, :]
    dist = np.stack([qi - kj, qi + BLOCK - kj])
    return np.where((dist >= 0) & (dist <= max_dist), 0.0, MASKED).astype(np.float32)


def _window(blk):
    r0 = pl.multiple_of(blk * BLOCK, BLOCK)
    k0 = pl.multiple_of(jnp.maximum(blk - 1, 0) * BLOCK, BLOCK)
    return r0, k0, jnp.minimum(blk, 1)


def _dilated_kernel(q1_ref, k1_ref, v1_ref, q4_ref, k4_ref, v4_ref, q16_ref, k16_ref, v16_ref,
                    bias_ref, o_ref, o4_ref, l4_ref, o16_ref, l16_ref):
    seq_len = q1_ref.shape[1]
    n_pairs = W_B // PAIR

    dil = DIL_PAIRS[2][1]
    causal = bias_ref[0, :, 0:BLOCK]
    for j in range(dil):
        for g in range(n_pairs):
            c = j * W_B + g * PAIR
            out, lse = _pair_attention(q16_ref[0, :, c:c + PAIR], k16_ref[0, :, c:c + PAIR],
                                       v16_ref[0, :, c:c + PAIR], causal)
            o16_ref[g, pl.ds(j, BLOCK, stride=dil), :] = out
            l16_ref[g, pl.ds(j, BLOCK, stride=dil), :] = lse

    dil = DIL_PAIRS[1][1]

    def body4(blk, carry):
        u0, k0, first = _window(blk)
        bias = bias_ref[first]
        for j in range(dil):
            for g in range(n_pairs):
                c = j * W_B + g * PAIR
                out, lse = _pair_attention(q4_ref[0, pl.ds(u0, BLOCK), c:c + PAIR],
                                           k4_ref[0, pl.ds(k0, 2 * BLOCK), c:c + PAIR],
                                           v4_ref[0, pl.ds(k0, 2 * BLOCK), c:c + PAIR], bias)
                o4_ref[g, pl.ds(u0 * dil + j, BLOCK, stride=dil), :] = out
                l4_ref[g, pl.ds(u0 * dil + j, BLOCK, stride=dil), :] = lse
        return carry

    lax.fori_loop(0, seq_len // dil // BLOCK, body4, 0)

    def body1(i, carry):
        for u in range(DIL_BLOCKS_PER_STEP):
            r0, k0, first = _window(i * DIL_BLOCKS_PER_STEP + u)
            bias = bias_ref[first]
            for g in range(n_pairs):
                c = g * PAIR
                o1, l1 = _pair_attention(q1_ref[0, pl.ds(r0, BLOCK), c:c + PAIR],
                                         k1_ref[0, pl.ds(k0, 2 * BLOCK), c:c + PAIR],
                                         v1_ref[0, pl.ds(k0, 2 * BLOCK), c:c + PAIR], bias)
                o4 = o4_ref[g, pl.ds(r0, BLOCK), :]
                l4 = l4_ref[g, pl.ds(r0, BLOCK), :]
                o16 = o16_ref[g, pl.ds(r0, BLOCK), :]
                l16 = l16_ref[g, pl.ds(r0, BLOCK), :]
                top = jnp.maximum(jnp.maximum(l1, l4), l16)
                e1 = jnp.exp(l1 - top)
                e4 = jnp.exp(l4 - top)
                e16 = jnp.exp(l16 - top)
                merged = (e1 * o1 + e4 * o4 + e16 * o16) / (e1 + e4 + e16)
                o_ref[0, pl.ds(r0, BLOCK), c:c + PAIR] = merged.astype(bf16)
        return carry

    lax.fori_loop(0, seq_len // BLOCK // DIL_BLOCKS_PER_STEP, body1, 0)


def _dilated(views):
    B, S, _ = views[0].shape
    assert all(w // r == BLOCK for w, r in DIL_PAIRS) and S // DIL_PAIRS[2][1] == BLOCK
    bias = jnp.asarray(_window_bias(BLOCK))
    view_specs = []
    for _, r in DIL_PAIRS:
        view_specs += [pl.BlockSpec((1, S // r, r * W_B), lambda b: (b, 0, 0))] * 3
    slab = (W_B // PAIR, S, V7X_LANES)
    pipelined = 10 * _nbytes((S, W_B), bf16) + _nbytes(bias.shape, f32)
    return pl.pallas_call(
        _dilated_kernel,
        grid=(B,),
        in_specs=view_specs + [pl.BlockSpec(bias.shape, lambda b: (0, 0, 0))],
        out_specs=pl.BlockSpec((1, S, W_B), lambda b: (b, 0, 0)),
        out_shape=jax.ShapeDtypeStruct((B, S, W_B), bf16),
        scratch_shapes=[pltpu.VMEM(slab, f32) for _ in range(4)],
        compiler_params=pltpu.CompilerParams(
            dimension_semantics=("arbitrary",),
            vmem_limit_bytes=_vmem_limit(pipelined, 4 * _nbytes(slab, f32))),
        name="dilated_attention",
    )(*views, bias)


def _tail_kernel(sink_ref, qa_ref, ka_ref, ka_prev_ref, va_ref, va_prev_ref, qc_ref, mem_ref,
                 w_mem_ref, ob_ref, gate_ref, x_ref, w_out_ref, gain_ref, ln_bias_ref, bias_ref,
                 out_ref, k_win_ref, v_win_ref, mk_ref, mv_ref, y_ref):
    tile = pl.program_id(1)
    tm = x_ref.shape[1]
    groups_per_kv = (SWA_Q_HEADS // SWA_KV_HEADS) // 2

    @pl.when(tile == 0)
    def _():
        mkv = jnp.dot(mem_ref[0].astype(bf16), w_mem_ref[...], preferred_element_type=f32)
        mk_ref[...] = mkv[:, :W_C].astype(bf16)
        mv_ref[...] = mkv[:, W_C:].astype(bf16)

    for win_ref, prev_ref, cur_ref in ((k_win_ref, ka_prev_ref, ka_ref),
                                       (v_win_ref, va_prev_ref, va_ref)):
        win_ref[0:BLOCK] = prev_ref[0]
        win_ref[BLOCK:BLOCK + tm] = cur_ref[0]

    for j in range(tm // BLOCK):
        rows = slice(j * BLOCK, (j + 1) * BLOCK)
        bias = bias_ref[jnp.minimum(tile, 1)] if j == 0 else bias_ref[1]
        k = k_win_ref[j * BLOCK:(j + 2) * BLOCK]
        v = v_win_ref[j * BLOCK:(j + 2) * BLOCK]
        for g in range(W_A // PAIR):
            kv = (g // groups_per_kv) * PAIR
            out, _ = _pair_attention(qa_ref[0, rows, g * PAIR:(g + 1) * PAIR],
                                     k[:, kv:kv + PAIR], v[:, kv:kv + PAIR], bias,
                                     sinks=(sink_ref[2 * g], sink_ref[2 * g + 1]))
            y_ref[rows, g * PAIR:(g + 1) * PAIR] = out
        for g in range(W_C // PAIR):
            c = g * PAIR
            out, _ = _pair_attention(qc_ref[0, rows, c:c + PAIR],
                                     mk_ref[:, c:c + PAIR], mv_ref[:, c:c + PAIR], None)
            y_ref[rows, W_A + W_B + c:W_A + W_B + c + PAIR] = out

    def rms_normalize(t):
        return t * lax.rsqrt(jnp.mean(t * t, axis=-1, keepdims=True) + RMS_EPS)

    y = jnp.concatenate([rms_normalize(y_ref[:, 0:W_A]),
                         rms_normalize(ob_ref[0].astype(f32)),
                         rms_normalize(y_ref[:, W_A + W_B:D_MIX])], axis=-1)
    y = (y * gate_ref[0].astype(f32)).astype(bf16)
    r = DEEPNORM_ALPHA * x_ref[0] + jnp.dot(y, w_out_ref[...], preferred_element_type=f32)
    mu = jnp.mean(r, axis=-1, keepdims=True)
    d = r - mu
    var = jnp.mean(d * d, axis=-1, keepdims=True)
    out_ref[0] = d * lax.rsqrt(var + LN_EPS) * gain_ref[...] + ln_bias_ref[...]


def _tail_bias():
    qi = np.arange(BLOCK)[:, None]
    kj = np.arange(2 * BLOCK)[None, :]
    dist = qi + BLOCK - kj
    band = (dist >= 0) & (dist <= SWA_WINDOW - 1)
    return np.where(np.stack([band & (kj >= BLOCK), band]), 0.0, MASKED).astype(np.float32)


def _tail(sinks, qa, ka, va, qc, mem, w_mem, ob, gate, x, w_out, ln_gain, ln_bias):
    B, S, _ = x.shape
    M = mem.shape[1]
    tm = ROW_TILE
    blocks_per_tile = tm // BLOCK
    bias = jnp.asarray(_tail_bias())
    row_block = lambda width: pl.BlockSpec((1, tm, width), lambda b, i: (b, i, 0))
    prev_block = lambda width: pl.BlockSpec(
        (1, BLOCK, width), lambda b, i: (b, jnp.maximum(i * blocks_per_tile - 1, 0), 0))
    const = lambda shape: pl.BlockSpec(shape, lambda b, i: (0,) * len(shape))
    kv_w = 2 * W_KV_A
    pipelined = (_nbytes((tm, W_A + 2 * kv_w + W_C + W_B + D_MIX), bf16)
                 + 2 * _nbytes((BLOCK, kv_w), bf16) + _nbytes((M, D_MODEL), f32)
                 + _nbytes((D_MODEL, 2 * W_C), bf16) + 2 * _nbytes((tm, D_MODEL), f32)
                 + _nbytes((D_MIX, D_MODEL), bf16) + _nbytes(bias.shape, f32))
    scratch = [pltpu.VMEM((BLOCK + tm, kv_w), bf16), pltpu.VMEM((BLOCK + tm, kv_w), bf16),
               pltpu.VMEM((M, W_C), bf16), pltpu.VMEM((M, W_C), bf16),
               pltpu.VMEM((tm, D_MIX), f32)]
    scratch_bytes = (2 * _nbytes((BLOCK + tm, kv_w), bf16) + 2 * _nbytes((M, W_C), bf16)
                     + 4 * _nbytes((tm, D_MIX), f32))
    return pl.pallas_call(
        _tail_kernel,
        grid=(B, S // tm),
        in_specs=[pl.BlockSpec(memory_space=pltpu.SMEM),
                  row_block(W_A), row_block(kv_w), prev_block(kv_w),
                  row_block(kv_w), prev_block(kv_w), row_block(W_C),
                  pl.BlockSpec((1, M, D_MODEL), lambda b, i: (b, 0, 0)),
                  const((D_MODEL, 2 * W_C)), row_block(W_B), row_block(D_MIX),
                  row_block(D_MODEL), const((D_MIX, D_MODEL)),
                  const((1, D_MODEL)), const((1, D_MODEL)), const(bias.shape)],
        out_specs=row_block(D_MODEL),
        out_shape=jax.ShapeDtypeStruct((B, S, D_MODEL), f32),
        scratch_shapes=scratch,
        compiler_params=pltpu.CompilerParams(
            dimension_semantics=("arbitrary", "arbitrary"),
            vmem_limit_bytes=_vmem_limit(pipelined, scratch_bytes)),
        name="mix_out",
    )(sinks, qa, ka, ka, va, va, qc, mem, w_mem, ob, gate, x, w_out, ln_gain, ln_bias, bias)


def _rope_tables(seq_len):
    pos = jnp.arange(seq_len, dtype=f32)
    inv = ROPE_THETA ** (-jnp.arange(0, HEAD_DIM, 2, dtype=f32) / HEAD_DIM)
    ang = pos[:, None] * inv[None, :]
    cos, sin = jnp.cos(ang), jnp.sin(ang)
    cos2 = jnp.concatenate([cos, cos, cos, cos], axis=-1)
    sin2 = jnp.concatenate([-sin, sin, -sin, sin], axis=-1)
    return cos2, sin2


def kernel(x, mem, w_in, b_in, w_mem, attn_sinks, g_branch, w_out, ln_gain, ln_bias):
    assert w_in.shape[0] == DEPTH
    S = x.shape[1]
    cos2, sin2 = _rope_tables(S)
    for l in range(DEPTH):
        qa, ka, va, qc, gate, *views_b = _in_proj(
            x, w_in[l].astype(bf16), b_in[l][None, :], cos2, sin2, g_branch[l][None, :])
        ob = _dilated(views_b)
        x = _tail(attn_sinks[l], qa, ka, va, qc, mem, w_mem[l].astype(bf16), ob, gate, x,
                  w_out[l].astype(bf16), ln_gain[l][None, :], ln_bias[l][None, :])
    return x
```

```python
import math

import numpy as np
import jax
import jax.numpy as jnp
from jax import lax
from jax.experimental import pallas as pl
from jax.experimental.pallas import tpu as pltpu

D_MODEL = 1024
HEAD_DIM = 64
SWA_Q_HEADS = 8
SWA_KV_HEADS = 2
SWA_WINDOW = 128
DIL_HEADS = 4
DIL_PAIRS = ((128, 1), (512, 4), (2048, 16))
MEM_HEADS = 4
BLOCK = 128
ROPE_THETA = 10000.0
LN_EPS = 1e-5
RMS_EPS = 1e-6
DEPTH = 1
DEEPNORM_ALPHA = (2 * DEPTH) ** 0.25

W_A = SWA_Q_HEADS * HEAD_DIM
W_KV_A = SWA_KV_HEADS * HEAD_DIM
W_B = DIL_HEADS * HEAD_DIM
W_C = MEM_HEADS * HEAD_DIM
D_MIX = W_A + W_B + W_C
D_IN = W_A + 2 * W_KV_A + 3 * W_B + W_C + D_MIX

V7X_LANES = 128
V7X_VMEM_BYTES = 64 * 2**20
VMEM_HEADROOM_BYTES = 8 * 2**20

PAIR = 2 * HEAD_DIM
assert PAIR == V7X_LANES
LOG2_E = math.log2(math.e)
Q_SCALE = HEAD_DIM ** -0.5 * LOG2_E
MASKED = -1e30

OFF_QA = 0
OFF_KA = OFF_QA + W_A
OFF_VA = OFF_KA + W_KV_A
OFF_QB = OFF_VA + W_KV_A
OFF_KB = OFF_QB + W_B
OFF_VB = OFF_KB + W_B
OFF_QC = OFF_VB + W_B
OFF_Z = OFF_QC + W_C
assert OFF_Z + D_MIX == D_IN

ROW_TILE = 512
DIL_BLOCKS_PER_STEP = 4
DIL16_STREAMS_PER_STEP = 8

f32 = jnp.float32
bf16 = jnp.bfloat16


def _vmem_limit(pipelined_bytes, scratch_bytes=0):
    budget = V7X_VMEM_BYTES - VMEM_HEADROOM_BYTES
    assert 2 * pipelined_bytes + scratch_bytes + VMEM_HEADROOM_BYTES <= budget
    return budget


def _nbytes(shape, dtype):
    return int(np.prod(shape)) * jnp.dtype(dtype).itemsize


def _lane_is_first_head():
    return lax.broadcasted_iota(jnp.int32, (1, V7X_LANES), 1) < HEAD_DIM


def _in_proj_kernel(x_ref, w_ref, b_ref, cos_ref, sin_ref, g_ref,
                    qa_ref, ka_ref, va_ref, qc_ref, gate_ref,
                    q1_ref, k1_ref, v1_ref, q4_ref, k4_ref, v4_ref, q16_ref, k16_ref, v16_ref,
                    stage_ref):
    tm = x_ref.shape[1]
    xb = x_ref[0].astype(bf16)
    cos = cos_ref[...]
    sin = sin_ref[...]
    lane = lax.broadcasted_iota(jnp.int32, (1, V7X_LANES), 1)
    first_half = (lane % HEAD_DIM) < (HEAD_DIM // 2)
    first_head = lane < HEAD_DIM

    def proj(off, width):
        acc = jnp.dot(xb, w_ref[:, off:off + width], preferred_element_type=f32)
        return acc + b_ref[:, off:off + width]

    def rope(a):
        rot = jnp.where(first_half,
                        pltpu.roll(a, V7X_LANES - HEAD_DIM // 2, 1),
                        pltpu.roll(a, HEAD_DIM // 2, 1))
        return a * cos + rot * sin

    def groups(a):
        return [a[:, i:i + V7X_LANES] for i in range(0, a.shape[1], V7X_LANES)]

    def both_halves(a):
        swapped = pltpu.roll(a, HEAD_DIM, 1)
        return jnp.where(first_head, a, swapped), jnp.where(first_head, swapped, a)

    for c in range(0, W_A, 2 * V7X_LANES):
        for i, a in enumerate(groups(proj(OFF_QA + c, 2 * V7X_LANES))):
            lo = c + i * V7X_LANES
            qa_ref[0, :, lo:lo + V7X_LANES] = (rope(a) * Q_SCALE).astype(bf16)
    k_a, v_a = groups(proj(OFF_KA, 2 * W_KV_A))
    for ref, a in ((ka_ref, rope(k_a)), (va_ref, v_a)):
        kv0, kv1 = both_halves(a)
        ref[0, :, 0:V7X_LANES] = kv0.astype(bf16)
        ref[0, :, V7X_LANES:2 * V7X_LANES] = kv1.astype(bf16)
    mixer_b = ((OFF_QB, lambda a: rope(a) * Q_SCALE, (q1_ref, q4_ref, q16_ref)),
               (OFF_KB, rope, (k1_ref, k4_ref, k16_ref)),
               (OFF_VB, lambda a: a, (v1_ref, v4_ref, v16_ref)))
    for t, (off, fn, (nat_ref, r4_ref, r16_ref)) in enumerate(mixer_b):
        for g, a in enumerate(groups(proj(off, W_B))):
            a = fn(a)
            nat_ref[0, :, g * PAIR:(g + 1) * PAIR] = a.astype(bf16)
            slab = t * (W_B // PAIR) + g
            stage_ref[slab] = a
            for view_ref, dil in ((r4_ref, DIL_PAIRS[1][1]), (r16_ref, DIL_PAIRS[2][1])):
                for j in range(dil):
                    stream = stage_ref[slab, pl.ds(j, tm // dil, stride=dil), :]
                    view_ref[0, j, :, g * PAIR:(g + 1) * PAIR] = stream.astype(bf16)
    qc_ref[0] = (proj(OFF_QC, W_C) * Q_SCALE).astype(bf16)
    for c in range(0, D_MIX, 2 * V7X_LANES):
        z = proj(OFF_Z + c, 2 * V7X_LANES)
        gate = z * jax.nn.sigmoid(z) * g_ref[:, c:c + 2 * V7X_LANES]
        gate_ref[0, :, c:c + 2 * V7X_LANES] = gate.astype(bf16)


def _in_proj(x, w_in, b_in, cos2, sin2, g_branch):
    B, S, _ = x.shape
    tm = ROW_TILE
    widths = (W_A, 2 * W_KV_A, 2 * W_KV_A, W_C, D_MIX, W_B, W_B, W_B)
    row_block = lambda width: pl.BlockSpec((1, tm, width), lambda b, i: (b, i, 0))
    out_specs = [row_block(w) for w in widths]
    out_shape = [jax.ShapeDtypeStruct((B, S, w), bf16) for w in widths]
    for _, r in DIL_PAIRS[1:]:
        out_specs += [pl.BlockSpec((1, r, tm // r, W_B), lambda b, i: (b, 0, i, 0))] * 3
        out_shape += [jax.ShapeDtypeStruct((B, r, S // r, W_B), bf16)] * 3
    const = lambda shape: pl.BlockSpec(shape, lambda b, i: (0,) * len(shape))
    stage = (3 * (W_B // PAIR), tm, V7X_LANES)
    pipelined = (_nbytes((tm, D_MODEL), f32) + _nbytes((D_MODEL, D_IN), bf16)
                 + _nbytes((tm, sum(widths) + 6 * W_B), bf16) + 2 * _nbytes((tm, V7X_LANES), f32))
    return pl.pallas_call(
        _in_proj_kernel,
        grid=(B, S // tm),
        in_specs=[row_block(D_MODEL), const((D_MODEL, D_IN)), const((1, D_IN)),
                  pl.BlockSpec((tm, V7X_LANES), lambda b, i: (i, 0)),
                  pl.BlockSpec((tm, V7X_LANES), lambda b, i: (i, 0)),
                  const((1, D_MIX))],
        out_specs=out_specs,
        out_shape=out_shape,
        scratch_shapes=[pltpu.VMEM(stage, f32)],
        compiler_params=pltpu.CompilerParams(
            dimension_semantics=("arbitrary", "arbitrary"),
            vmem_limit_bytes=_vmem_limit(pipelined, _nbytes(stage, f32))),
        name="in_proj",
    )(x, w_in, b_in, cos2, sin2, g_branch)


def _attend(chains, s_ref, p_ref):
    first_head = _lane_is_first_head()
    for c, (q2, k2, _, _) in enumerate(chains):
        zero = jnp.zeros_like(q2)
        stacked = jnp.concatenate([jnp.where(first_head, q2, zero),
                                   jnp.where(first_head, zero, q2)], axis=0)
        s_ref[c, :, 0:k2.shape[0]] = lax.dot_general(
            stacked, k2, (((1,), (1,)), ((), ())), preferred_element_type=f32)
    tops = []
    for c, (q2, k2, _, biases) in enumerate(chains):
        m_rows, n = q2.shape[0], k2.shape[0]
        pair_tops = []
        for h in range(2):
            rows = slice(h * m_rows, (h + 1) * m_rows)
            sh = s_ref[c, rows, 0:n]
            if biases is not None:
                sh = sh + biases[h]
            m = jnp.max(sh, axis=-1, keepdims=True)
            p_ref[c, rows, 0:n] = jnp.exp2(sh - m).astype(bf16)
            pair_tops.append(m)
        tops.append(pair_tops)
    results = []
    for c, (q2, k2, v2, _) in enumerate(chains):
        m_rows, n = q2.shape[0], k2.shape[0]
        v_ones = jnp.concatenate([v2, jnp.ones_like(v2)], axis=1)
        o = jnp.dot(p_ref[c, :, 0:n], v_ones, preferred_element_type=f32)
        acc = jnp.where(first_head, o[:m_rows, :PAIR], o[m_rows:, :PAIR])
        denom = jnp.where(first_head, o[:m_rows, PAIR:], o[m_rows:, PAIR:])
        lse = jnp.where(first_head, tops[c][0], tops[c][1]) + jnp.log(denom) * LOG2_E
        results.append((acc * (1.0 / denom), lse))
    return results


def _window_bias(max_dist):
    qi = np.arange(BLOCK)[:, None]
    kj = np.arange(2 * BLOCK)[None, :]
    dist = np.stack([qi - kj, qi + BLOCK - kj])
    return np.where((dist >= 0) & (dist <= max_dist), 0.0, MASKED).astype(np.float32)


def _window(blk):
    r0 = pl.multiple_of(blk * BLOCK, BLOCK)
    k0 = pl.multiple_of(jnp.maximum(blk - 1, 0) * BLOCK, BLOCK)
    return r0, k0, jnp.minimum(blk, 1)


def _dilated_kernel(q1_ref, k1_ref, v1_ref, q4_ref, k4_ref, v4_ref, q16_ref, k16_ref, v16_ref,
                    bias_ref, o_ref, o4_ref, l4_ref, o16_ref, l16_ref, s_ref, p_ref):
    seq_len = q1_ref.shape[1]
    n_pairs = W_B // PAIR

    dil16 = DIL_PAIRS[2][1]
    causal = bias_ref[0, :, 0:BLOCK]

    def body16(i, carry):
        streams = [i * DIL16_STREAMS_PER_STEP + u for u in range(DIL16_STREAMS_PER_STEP)]
        chains = [(q16_ref[0, j, :, g * PAIR:(g + 1) * PAIR],
                   k16_ref[0, j, :, g * PAIR:(g + 1) * PAIR],
                   v16_ref[0, j, :, g * PAIR:(g + 1) * PAIR], (causal, causal))
                  for j in streams for g in range(n_pairs)]
        results = _attend(chains, s_ref, p_ref)
        for n, (out, lse) in enumerate(results):
            j, g = streams[n // n_pairs], n % n_pairs
            o16_ref[g, pl.ds(j, BLOCK, stride=dil16), :] = out
            l16_ref[g, pl.ds(j, BLOCK, stride=dil16), :] = lse
        return carry

    lax.fori_loop(0, dil16 // DIL16_STREAMS_PER_STEP, body16, 0)

    dil4 = DIL_PAIRS[1][1]

    def body4(blk, carry):
        u0, k0, first = _window(blk)
        bias = bias_ref[first]
        chains = [(q4_ref[0, j, pl.ds(u0, BLOCK), g * PAIR:(g + 1) * PAIR],
                   k4_ref[0, j, pl.ds(k0, 2 * BLOCK), g * PAIR:(g + 1) * PAIR],
                   v4_ref[0, j, pl.ds(k0, 2 * BLOCK), g * PAIR:(g + 1) * PAIR], (bias, bias))
                  for j in range(dil4) for g in range(n_pairs)]
        results = _attend(chains, s_ref, p_ref)
        for n, (out, lse) in enumerate(results):
            j, g = n // n_pairs, n % n_pairs
            o4_ref[g, pl.ds(u0 * dil4 + j, BLOCK, stride=dil4), :] = out
            l4_ref[g, pl.ds(u0 * dil4 + j, BLOCK, stride=dil4), :] = lse
        return carry

    lax.fori_loop(0, seq_len // dil4 // BLOCK, body4, 0)

    def body1(i, carry):
        windows = [_window(i * DIL_BLOCKS_PER_STEP + u) for u in range(DIL_BLOCKS_PER_STEP)]
        chains = [(q1_ref[0, pl.ds(r0, BLOCK), g * PAIR:(g + 1) * PAIR],
                   k1_ref[0, pl.ds(k0, 2 * BLOCK), g * PAIR:(g + 1) * PAIR],
                   v1_ref[0, pl.ds(k0, 2 * BLOCK), g * PAIR:(g + 1) * PAIR],
                   (bias_ref[first], bias_ref[first]))
                  for r0, k0, first in windows for g in range(n_pairs)]
        results = _attend(chains, s_ref, p_ref)
        for n, (o1, l1) in enumerate(results):
            r0, g = windows[n // n_pairs][0], n % n_pairs
            o4 = o4_ref[g, pl.ds(r0, BLOCK), :]
            l4 = l4_ref[g, pl.ds(r0, BLOCK), :]
            o16 = o16_ref[g, pl.ds(r0, BLOCK), :]
            l16 = l16_ref[g, pl.ds(r0, BLOCK), :]
            top = jnp.maximum(jnp.maximum(l1, l4), l16)
            e1 = jnp.exp2(l1 - top)
            e4 = jnp.exp2(l4 - top)
            e16 = jnp.exp2(l16 - top)
            merged = (e1 * o1 + e4 * o4 + e16 * o16) / (e1 + e4 + e16)
            o_ref[0, pl.ds(r0, BLOCK), g * PAIR:(g + 1) * PAIR] = merged.astype(bf16)
        return carry

    lax.fori_loop(0, seq_len // BLOCK // DIL_BLOCKS_PER_STEP, body1, 0)


def _dilated(views):
    B, S, _ = views[0].shape
    assert all(w // r == BLOCK for w, r in DIL_PAIRS) and S // DIL_PAIRS[2][1] == BLOCK
    assert DIL_PAIRS[2][1] % DIL16_STREAMS_PER_STEP == 0
    bias = jnp.asarray(_window_bias(BLOCK))
    view_specs = [pl.BlockSpec((1, S, W_B), lambda b: (b, 0, 0))] * 3
    for _, r in DIL_PAIRS[1:]:
        view_specs += [pl.BlockSpec((1, r, S // r, W_B), lambda b: (b, 0, 0, 0))] * 3
    slab = (W_B // PAIR, S, V7X_LANES)
    chains = max(DIL_BLOCKS_PER_STEP, DIL16_STREAMS_PER_STEP, DIL_PAIRS[1][1]) * (W_B // PAIR)
    stage = (chains, 2 * BLOCK, 2 * BLOCK)
    pipelined = 10 * _nbytes((S, W_B), bf16) + _nbytes(bias.shape, f32)
    return pl.pallas_call(
        _dilated_kernel,
        grid=(B,),
        in_specs=view_specs + [pl.BlockSpec(bias.shape, lambda b: (0, 0, 0))],
        out_specs=pl.BlockSpec((1, S, W_B), lambda b: (b, 0, 0)),
        out_shape=jax.ShapeDtypeStruct((B, S, W_B), bf16),
        scratch_shapes=[pltpu.VMEM(slab, f32) for _ in range(4)]
        + [pltpu.VMEM(stage, f32), pltpu.VMEM(stage, bf16)],
        compiler_params=pltpu.CompilerParams(
            dimension_semantics=("arbitrary",),
            vmem_limit_bytes=_vmem_limit(
                pipelined, 4 * _nbytes(slab, f32) + _nbytes(stage, f32) + _nbytes(stage, bf16))),
        name="dilated_attention",
    )(*views, bias)


def _tail_kernel(sink_ref, qa_ref, ka_ref, ka_prev_ref, va_ref, va_prev_ref, qc_ref, mem_ref,
                 w_mem_ref, ob_ref, gate_ref, x_ref, w_out_ref, gain_ref, ln_bias_ref, bias_ref,
                 out_ref, k_win_ref, v_win_ref, mk_ref, mv_ref, y_ref, s_ref, p_ref):
    tile = pl.program_id(1)
    tm = x_ref.shape[1]
    groups_per_kv = (SWA_Q_HEADS // SWA_KV_HEADS) // 2

    @pl.when(tile == 0)
    def _():
        mkv = jnp.dot(mem_ref[0].astype(bf16), w_mem_ref[...], preferred_element_type=f32)
        mk_ref[...] = mkv[:, :W_C].astype(bf16)
        mv_ref[...] = mkv[:, W_C:].astype(bf16)

    for win_ref, prev_ref, cur_ref in ((k_win_ref, ka_prev_ref, ka_ref),
                                       (v_win_ref, va_prev_ref, va_ref)):
        win_ref[0:BLOCK] = prev_ref[0]
        win_ref[BLOCK:BLOCK + tm] = cur_ref[0]

    first_row = lax.broadcasted_iota(jnp.int32, (BLOCK, 1), 0) == 0
    first_col = lax.broadcasted_iota(jnp.int32, (1, BLOCK), 1) == 0

    def with_sink_slot(kv):
        return jnp.concatenate([jnp.where(first_row, jnp.zeros_like(kv[:BLOCK]), kv[:BLOCK]),
                                kv[BLOCK:]], axis=0)

    def sink_bias(bias, head):
        sink = sink_ref[head] * LOG2_E
        return jnp.concatenate([jnp.where(first_col, sink, bias[:, :BLOCK]), bias[:, BLOCK:]],
                               axis=1)

    for j in range(tm // BLOCK):
        chains, slots = [], []
        rows = slice(j * BLOCK, (j + 1) * BLOCK)
        bias = bias_ref[jnp.minimum(tile, 1)] if j == 0 else bias_ref[1]
        k = k_win_ref[j * BLOCK:(j + 2) * BLOCK]
        v = v_win_ref[j * BLOCK:(j + 2) * BLOCK]
        for g in range(W_A // PAIR):
            kv = (g // groups_per_kv) * PAIR
            chains.append((qa_ref[0, rows, g * PAIR:(g + 1) * PAIR],
                           with_sink_slot(k[:, kv:kv + PAIR]), with_sink_slot(v[:, kv:kv + PAIR]),
                           (sink_bias(bias, 2 * g), sink_bias(bias, 2 * g + 1))))
            slots.append((rows, g * PAIR))
        for g in range(W_C // PAIR):
            c = g * PAIR
            chains.append((qc_ref[0, rows, c:c + PAIR],
                           mk_ref[:, c:c + PAIR], mv_ref[:, c:c + PAIR], None))
            slots.append((rows, W_A + W_B + c))
        for (rows, c), (out, _) in zip(slots, _attend(chains, s_ref, p_ref)):
            y_ref[rows, c:c + PAIR] = out

    def rms_normalize(t):
        return t * lax.rsqrt(jnp.mean(t * t, axis=-1, keepdims=True) + RMS_EPS)

    y = jnp.concatenate([rms_normalize(y_ref[:, 0:W_A]),
                         rms_normalize(ob_ref[0].astype(f32)),
                         rms_normalize(y_ref[:, W_A + W_B:D_MIX])], axis=-1)
    y = (y * gate_ref[0].astype(f32)).astype(bf16)
    r = DEEPNORM_ALPHA * x_ref[0] + jnp.dot(y, w_out_ref[...], preferred_element_type=f32)
    mu = jnp.mean(r, axis=-1, keepdims=True)
    d = r - mu
    var = jnp.mean(d * d, axis=-1, keepdims=True)
    out_ref[0] = d * lax.rsqrt(var + LN_EPS) * gain_ref[...] + ln_bias_ref[...]


def _tail_bias():
    qi = np.arange(BLOCK)[:, None]
    kj = np.arange(2 * BLOCK)[None, :]
    dist = qi + BLOCK - kj
    band = (dist >= 0) & (dist <= SWA_WINDOW - 1)
    return np.where(np.stack([band & (kj >= BLOCK), band]), 0.0, MASKED).astype(np.float32)


def _tail(sinks, qa, ka, va, qc, mem, w_mem, ob, gate, x, w_out, ln_gain, ln_bias):
    B, S, _ = x.shape
    M = mem.shape[1]
    tm = ROW_TILE
    blocks_per_tile = tm // BLOCK
    bias = jnp.asarray(_tail_bias())
    row_block = lambda width: pl.BlockSpec((1, tm, width), lambda b, i: (b, i, 0))
    prev_block = lambda width: pl.BlockSpec(
        (1, BLOCK, width), lambda b, i: (b, jnp.maximum(i * blocks_per_tile - 1, 0), 0))
    const = lambda shape: pl.BlockSpec(shape, lambda b, i: (0,) * len(shape))
    kv_w = 2 * W_KV_A
    stage = ((W_A + W_C) // PAIR, 2 * BLOCK, 2 * BLOCK)
    pipelined = (_nbytes((tm, W_A + 2 * kv_w + W_C + W_B + D_MIX), bf16)
                 + 2 * _nbytes((BLOCK, kv_w), bf16) + _nbytes((M, D_MODEL), f32)
                 + _nbytes((D_MODEL, 2 * W_C), bf16) + 2 * _nbytes((tm, D_MODEL), f32)
                 + _nbytes((D_MIX, D_MODEL), bf16) + _nbytes(bias.shape, f32))
    scratch = [pltpu.VMEM((BLOCK + tm, kv_w), bf16), pltpu.VMEM((BLOCK + tm, kv_w), bf16),
               pltpu.VMEM((M, W_C), bf16), pltpu.VMEM((M, W_C), bf16),
               pltpu.VMEM((tm, D_MIX), f32), pltpu.VMEM(stage, f32), pltpu.VMEM(stage, bf16)]
    scratch_bytes = (2 * _nbytes((BLOCK + tm, kv_w), bf16) + 2 * _nbytes((M, W_C), bf16)
                     + 4 * _nbytes((tm, D_MIX), f32) + _nbytes(stage, f32) + _nbytes(stage, bf16))
    return pl.pallas_call(
        _tail_kernel,
        grid=(B, S // tm),
        in_specs=[pl.BlockSpec(memory_space=pltpu.SMEM),
                  row_block(W_A), row_block(kv_w), prev_block(kv_w),
                  row_block(kv_w), prev_block(kv_w), row_block(W_C),
                  pl.BlockSpec((1, M, D_MODEL), lambda b, i: (b, 0, 0)),
                  const((D_MODEL, 2 * W_C)), row_block(W_B), row_block(D_MIX),
                  row_block(D_MODEL), const((D_MIX, D_MODEL)),
                  const((1, D_MODEL)), const((1, D_MODEL)), const(bias.shape)],
        out_specs=row_block(D_MODEL),
        out_shape=jax.ShapeDtypeStruct((B, S, D_MODEL), f32),
        scratch_shapes=scratch,
        compiler_params=pltpu.CompilerParams(
            dimension_semantics=("arbitrary", "arbitrary"),
            vmem_limit_bytes=_vmem_limit(pipelined, scratch_bytes)),
        name="mix_out",
    )(sinks, qa, ka, ka, va, va, qc, mem, w_mem, ob, gate, x, w_out, ln_gain, ln_bias, bias)


def _rope_tables(seq_len):
    pos = jnp.arange(seq_len, dtype=f32)
    inv = ROPE_THETA ** (-jnp.arange(0, HEAD_DIM, 2, dtype=f32) / HEAD_DIM)
    ang = pos[:, None] * inv[None, :]
    cos, sin = jnp.cos(ang), jnp.sin(ang)
    cos2 = jnp.concatenate([cos, cos, cos, cos], axis=-1)
    sin2 = jnp.concatenate([-sin, sin, -sin, sin], axis=-1)
    return cos2, sin2


def kernel(x, mem, w_in, b_in, w_mem, attn_sinks, g_branch, w_out, ln_gain, ln_bias):
    assert w_in.shape[0] == DEPTH
    S = x.shape[1]
    cos2, sin2 = _rope_tables(S)
    for l in range(DEPTH):
        qa, ka, va, qc, gate, *views_b = _in_proj(
            x, w_in[l].astype(bf16), b_in[l][None, :], cos2, sin2, g_branch[l][None, :])
        ob = _dilated(views_b)
        x = _tail(attn_sinks[l], qa, ka, va, qc, mem, w_mem[l].astype(bf16), ob, gate, x,
                  w_out[l].astype(bf16), ln_gain[l][None, :], ln_bias[l][None, :])
    return x
```

```python
import math

import numpy as np
import jax
import jax.numpy as jnp
from jax import lax
from jax.experimental import pallas as pl
from jax.experimental.pallas import tpu as pltpu

D_MODEL = 1024
HEAD_DIM = 64
SWA_Q_HEADS = 8
SWA_KV_HEADS = 2
SWA_WINDOW = 128
DIL_HEADS = 4
DIL_PAIRS = ((128, 1), (512, 4), (2048, 16))
MEM_HEADS = 4
BLOCK = 128
ROPE_THETA = 10000.0
LN_EPS = 1e-5
RMS_EPS = 1e-6
DEPTH = 1
DEEPNORM_ALPHA = (2 * DEPTH) ** 0.25

W_A = SWA_Q_HEADS * HEAD_DIM
W_KV_A = SWA_KV_HEADS * HEAD_DIM
W_B = DIL_HEADS * HEAD_DIM
W_C = MEM_HEADS * HEAD_DIM
D_MIX = W_A + W_B + W_C
D_IN = W_A + 2 * W_KV_A + 3 * W_B + W_C + D_MIX

V7X_LANES = 128
V7X_VMEM_BYTES = 64 * 2**20
VMEM_HEADROOM_BYTES = 8 * 2**20

PAIR = 2 * HEAD_DIM
assert PAIR == V7X_LANES
LOG2_E = math.log2(math.e)
Q_SCALE = HEAD_DIM ** -0.5 * LOG2_E
MASKED = -1e30

OFF_QA = 0
OFF_KA = OFF_QA + W_A
OFF_VA = OFF_KA + W_KV_A
OFF_QB = OFF_VA + W_KV_A
OFF_KB = OFF_QB + W_B
OFF_VB = OFF_KB + W_B
OFF_QC = OFF_VB + W_B
OFF_Z = OFF_QC + W_C
assert OFF_Z + D_MIX == D_IN

ROW_TILE = 512
DIL_BLOCKS_PER_STEP = 4
DIL16_STREAMS_PER_STEP = 8

f32 = jnp.float32
bf16 = jnp.bfloat16


def _vmem_limit(pipelined_bytes, scratch_bytes=0):
    budget = V7X_VMEM_BYTES - VMEM_HEADROOM_BYTES
    assert 2 * pipelined_bytes + scratch_bytes + VMEM_HEADROOM_BYTES <= budget
    return budget


def _nbytes(shape, dtype):
    return int(np.prod(shape)) * jnp.dtype(dtype).itemsize


def _lane_is_first_head():
    return lax.broadcasted_iota(jnp.int32, (1, V7X_LANES), 1) < HEAD_DIM


def _in_proj_kernel(x_ref, w_ref, b_ref, cos_ref, sin_ref, g_ref,
                    qa_ref, ka_ref, va_ref, qc_ref, gate_ref,
                    q1_ref, k1_ref, v1_ref, q4_ref, k4_ref, v4_ref, q16_ref, k16_ref, v16_ref,
                    *stage_refs):
    tm = x_ref.shape[1]
    xb = x_ref[0].astype(bf16)
    cos = cos_ref[...]
    sin = sin_ref[...]
    lane = lax.broadcasted_iota(jnp.int32, (1, V7X_LANES), 1)
    first_half = (lane % HEAD_DIM) < (HEAD_DIM // 2)
    first_head = lane < HEAD_DIM

    def proj(off, width):
        acc = jnp.dot(xb, w_ref[:, off:off + width], preferred_element_type=f32)
        return acc + b_ref[:, off:off + width]

    def rope(a):
        rot = jnp.where(first_half,
                        pltpu.roll(a, V7X_LANES - HEAD_DIM // 2, 1),
                        pltpu.roll(a, HEAD_DIM // 2, 1))
        return a * cos + rot * sin

    def groups(a):
        return [a[:, i:i + V7X_LANES] for i in range(0, a.shape[1], V7X_LANES)]

    def both_halves(a):
        swapped = pltpu.roll(a, HEAD_DIM, 1)
        return jnp.where(first_head, a, swapped), jnp.where(first_head, swapped, a)

    for c in range(0, D_MIX, 2 * V7X_LANES):
        z = proj(OFF_Z + c, 2 * V7X_LANES)
        gate = z * jax.nn.sigmoid(z) * g_ref[:, c:c + 2 * V7X_LANES]
        gate_ref[0, :, c:c + 2 * V7X_LANES] = gate.astype(bf16)
    for c in range(0, W_A, 2 * V7X_LANES):
        for i, a in enumerate(groups(proj(OFF_QA + c, 2 * V7X_LANES))):
            lo = c + i * V7X_LANES
            qa_ref[0, :, lo:lo + V7X_LANES] = (rope(a) * Q_SCALE).astype(bf16)
    k_a, v_a = groups(proj(OFF_KA, 2 * W_KV_A))
    for ref, a in ((ka_ref, rope(k_a)), (va_ref, v_a)):
        kv0, kv1 = both_halves(a)
        ref[0, :, 0:V7X_LANES] = kv0.astype(bf16)
        ref[0, :, V7X_LANES:2 * V7X_LANES] = kv1.astype(bf16)
    mixer_b = ((OFF_QB, lambda a: rope(a) * Q_SCALE, (q1_ref, q4_ref, q16_ref)),
               (OFF_KB, rope, (k1_ref, k4_ref, k16_ref)),
               (OFF_VB, lambda a: a, (v1_ref, v4_ref, v16_ref)))
    dil4, dil16 = DIL_PAIRS[1][1], DIL_PAIRS[2][1]
    for t, (off, fn, (nat_ref, r4_ref, r16_ref)) in enumerate(mixer_b):
        for g, a in enumerate(groups(proj(off, W_B))):
            a = fn(a)
            nat_ref[0, :, g * PAIR:(g + 1) * PAIR] = a.astype(bf16)
            slab = t * (W_B // PAIR) + g
            stage_ref, stage4_ref = stage_refs[2 * slab], stage_refs[2 * slab + 1]
            stage_ref[...] = a
            lanes = slice(g * PAIR, (g + 1) * PAIR)
            rows4, rows16 = tm // dil4, tm // dil16
            for j4 in range(dil4):
                stream = stage_ref[pl.ds(j4, rows4, stride=dil4), :]
                r4_ref[0, j4, :, lanes] = stream.astype(bf16)
                stage4_ref[j4 * rows4:(j4 + 1) * rows4] = stream
            for j16 in range(dil16):
                j4, phase = j16 % dil4, j16 // dil4
                stream = stage4_ref[pl.ds(j4 * rows4 + phase, rows16, stride=dil16 // dil4), :]
                r16_ref[0, j16, :, lanes] = stream.astype(bf16)
    qc_ref[0] = (proj(OFF_QC, W_C) * Q_SCALE).astype(bf16)


def _in_proj(x, w_in, b_in, cos2, sin2, g_branch):
    B, S, _ = x.shape
    tm = ROW_TILE
    widths = (W_A, 2 * W_KV_A, 2 * W_KV_A, W_C, D_MIX, W_B, W_B, W_B)
    row_block = lambda width: pl.BlockSpec((1, tm, width), lambda b, i: (b, i, 0))
    out_specs = [row_block(w) for w in widths]
    out_shape = [jax.ShapeDtypeStruct((B, S, w), bf16) for w in widths]
    for _, r in DIL_PAIRS[1:]:
        out_specs += [pl.BlockSpec((1, r, tm // r, W_B), lambda b, i: (b, 0, i, 0))] * 3
        out_shape += [jax.ShapeDtypeStruct((B, r, S // r, W_B), bf16)] * 3
    const = lambda shape: pl.BlockSpec(shape, lambda b, i: (0,) * len(shape))
    n_stage, stage = 2 * 3 * (W_B // PAIR), (tm, V7X_LANES)
    pipelined = (_nbytes((tm, D_MODEL), f32) + _nbytes((D_MODEL, D_IN), bf16)
                 + _nbytes((tm, sum(widths) + 6 * W_B), bf16) + 2 * _nbytes((tm, V7X_LANES), f32))
    return pl.pallas_call(
        _in_proj_kernel,
        grid=(B, S // tm),
        in_specs=[row_block(D_MODEL), const((D_MODEL, D_IN)), const((1, D_IN)),
                  pl.BlockSpec((tm, V7X_LANES), lambda b, i: (i, 0)),
                  pl.BlockSpec((tm, V7X_LANES), lambda b, i: (i, 0)),
                  const((1, D_MIX))],
        out_specs=out_specs,
        out_shape=out_shape,
        scratch_shapes=[pltpu.VMEM(stage, f32)] * n_stage,
        compiler_params=pltpu.CompilerParams(
            dimension_semantics=("arbitrary", "arbitrary"),
            vmem_limit_bytes=_vmem_limit(pipelined, n_stage * _nbytes(stage, f32))),
        name="in_proj",
    )(x, w_in, b_in, cos2, sin2, g_branch)


def _attend(chains, s_ref, p_ref):
    first_head = _lane_is_first_head()
    for c, (q2, k2, _, _) in enumerate(chains):
        zero = jnp.zeros_like(q2)
        stacked = jnp.concatenate([jnp.where(first_head, q2, zero),
                                   jnp.where(first_head, zero, q2)], axis=0)
        s_ref[c, :, 0:k2.shape[0]] = lax.dot_general(
            stacked, k2, (((1,), (1,)), ((), ())), preferred_element_type=f32)
    tops = []
    for c, (q2, k2, _, biases) in enumerate(chains):
        m_rows, n = q2.shape[0], k2.shape[0]
        pair_tops = []
        for h in range(2):
            rows = slice(h * m_rows, (h + 1) * m_rows)
            sh = s_ref[c, rows, 0:n]
            if biases is not None:
                sh = sh + biases[h]
            m = jnp.max(sh, axis=-1, keepdims=True)
            p_ref[c, rows, 0:n] = jnp.exp2(sh - m).astype(bf16)
            pair_tops.append(m)
        tops.append(pair_tops)
    results = []
    for c, (q2, k2, v2, _) in enumerate(chains):
        m_rows, n = q2.shape[0], k2.shape[0]
        v_ones = jnp.concatenate([v2, jnp.ones_like(v2)], axis=1)
        o = jnp.dot(p_ref[c, :, 0:n], v_ones, preferred_element_type=f32)
        acc = jnp.where(first_head, o[:m_rows, :PAIR], o[m_rows:, :PAIR])
        denom = jnp.where(first_head, o[:m_rows, PAIR:], o[m_rows:, PAIR:])
        lse = jnp.where(first_head, tops[c][0], tops[c][1]) + jnp.log(denom) * LOG2_E
        results.append((acc * (1.0 / denom), lse))
    return results


def _window_bias(max_dist):
    qi = np.arange(BLOCK)[:, None]
    kj = np.arange(2 * BLOCK)[None, :]
    dist = np.stack([qi - kj, qi + BLOCK - kj])
    return np.where((dist >= 0) & (dist <= max_dist), 0.0, MASKED).astype(np.float32)


def _window(blk):
    r0 = pl.multiple_of(blk * BLOCK, BLOCK)
    k0 = pl.multiple_of(jnp.maximum(blk - 1, 0) * BLOCK, BLOCK)
    return r0, k0, jnp.minimum(blk, 1)


def _dilated_kernel(q1_ref, k1_ref, v1_ref, q4_ref, k4_ref, v4_ref, q16_ref, k16_ref, v16_ref,
                    bias_ref, o_ref, o4_ref, l4_ref, o16_ref, l16_ref, s_ref, p_ref):
    seq_len = q1_ref.shape[1]
    n_pairs = W_B // PAIR

    dil16 = DIL_PAIRS[2][1]
    causal = bias_ref[0, :, 0:BLOCK]

    def body16(i, carry):
        streams = [i * DIL16_STREAMS_PER_STEP + u for u in range(DIL16_STREAMS_PER_STEP)]
        chains = [(q16_ref[0, j, :, g * PAIR:(g + 1) * PAIR],
                   k16_ref[0, j, :, g * PAIR:(g + 1) * PAIR],
                   v16_ref[0, j, :, g * PAIR:(g + 1) * PAIR], (causal, causal))
                  for j in streams for g in range(n_pairs)]
        results = _attend(chains, s_ref, p_ref)
        for n, (out, lse) in enumerate(results):
            j, g = streams[n // n_pairs], n % n_pairs
            o16_ref[g, pl.ds(j, BLOCK, stride=dil16), :] = out
            l16_ref[g, pl.ds(j, BLOCK, stride=dil16), :] = lse
        return carry

    lax.fori_loop(0, dil16 // DIL16_STREAMS_PER_STEP, body16, 0)

    dil4 = DIL_PAIRS[1][1]

    def body4(blk, carry):
        u0, k0, first = _window(blk)
        bias = bias_ref[first]
        chains = [(q4_ref[0, j, pl.ds(u0, BLOCK), g * PAIR:(g + 1) * PAIR],
                   k4_ref[0, j, pl.ds(k0, 2 * BLOCK), g * PAIR:(g + 1) * PAIR],
                   v4_ref[0, j, pl.ds(k0, 2 * BLOCK), g * PAIR:(g + 1) * PAIR], (bias, bias))
                  for j in range(dil4) for g in range(n_pairs)]
        results = _attend(chains, s_ref, p_ref)
        for n, (out, lse) in enumerate(results):
            j, g = n // n_pairs, n % n_pairs
            o4_ref[g, pl.ds(u0 * dil4 + j, BLOCK, stride=dil4), :] = out
            l4_ref[g, pl.ds(u0 * dil4 + j, BLOCK, stride=dil4), :] = lse
        return carry

    lax.fori_loop(0, seq_len // dil4 // BLOCK, body4, 0)

    def body1(i, carry):
        windows = [_window(i * DIL_BLOCKS_PER_STEP + u) for u in range(DIL_BLOCKS_PER_STEP)]
        chains = [(q1_ref[0, pl.ds(r0, BLOCK), g * PAIR:(g + 1) * PAIR],
                   k1_ref[0, pl.ds(k0, 2 * BLOCK), g * PAIR:(g + 1) * PAIR],
                   v1_ref[0, pl.ds(k0, 2 * BLOCK), g * PAIR:(g + 1) * PAIR],
                   (bias_ref[first], bias_ref[first]))
                  for r0, k0, first in windows for g in range(n_pairs)]
        results = _attend(chains, s_ref, p_ref)
        for n, (o1, l1) in enumerate(results):
            r0, g = windows[n // n_pairs][0], n % n_pairs
            o4 = o4_ref[g, pl.ds(r0, BLOCK), :]
            l4 = l4_ref[g, pl.ds(r0, BLOCK), :]
            o16 = o16_ref[g, pl.ds(r0, BLOCK), :]
            l16 = l16_ref[g, pl.ds(r0, BLOCK), :]
            top = jnp.maximum(jnp.maximum(l1, l4), l16)
            e1 = jnp.exp2(l1 - top)
            e4 = jnp.exp2(l4 - top)
            e16 = jnp.exp2(l16 - top)
            merged = (e1 * o1 + e4 * o4 + e16 * o16) / (e1 + e4 + e16)
            o_ref[0, pl.ds(r0, BLOCK), g * PAIR:(g + 1) * PAIR] = merged.astype(bf16)
        return carry

    lax.fori_loop(0, seq_len // BLOCK // DIL_BLOCKS_PER_STEP, body1, 0)


def _dilated(views):
    B, S, _ = views[0].shape
    assert all(w // r == BLOCK for w, r in DIL_PAIRS) and S // DIL_PAIRS[2][1] == BLOCK
    assert DIL_PAIRS[2][1] % DIL16_STREAMS_PER_STEP == 0
    bias = jnp.asarray(_window_bias(BLOCK))
    view_specs = [pl.BlockSpec((1, S, W_B), lambda b: (b, 0, 0))] * 3
    for _, r in DIL_PAIRS[1:]:
        view_specs += [pl.BlockSpec((1, r, S // r, W_B), lambda b: (b, 0, 0, 0))] * 3
    slab = (W_B // PAIR, S, V7X_LANES)
    chains = max(DIL_BLOCKS_PER_STEP, DIL16_STREAMS_PER_STEP, DIL_PAIRS[1][1]) * (W_B // PAIR)
    stage = (chains, 2 * BLOCK, 2 * BLOCK)
    pipelined = 10 * _nbytes((S, W_B), bf16) + _nbytes(bias.shape, f32)
    return pl.pallas_call(
        _dilated_kernel,
        grid=(B,),
        in_specs=view_specs + [pl.BlockSpec(bias.shape, lambda b: (0, 0, 0))],
        out_specs=pl.BlockSpec((1, S, W_B), lambda b: (b, 0, 0)),
        out_shape=jax.ShapeDtypeStruct((B, S, W_B), bf16),
        scratch_shapes=[pltpu.VMEM(slab, f32) for _ in range(4)]
        + [pltpu.VMEM(stage, f32), pltpu.VMEM(stage, bf16)],
        compiler_params=pltpu.CompilerParams(
            dimension_semantics=("arbitrary",),
            vmem_limit_bytes=_vmem_limit(
                pipelined, 4 * _nbytes(slab, f32) + _nbytes(stage, f32) + _nbytes(stage, bf16))),
        name="dilated_attention",
    )(*views, bias)


def _tail_kernel(sink_ref, qa_ref, ka_ref, ka_prev_ref, va_ref, va_prev_ref, qc_ref, mem_ref,
                 w_mem_ref, ob_ref, gate_ref, x_ref, w_out_ref, gain_ref, ln_bias_ref, bias_ref,
                 out_ref, k_win_ref, v_win_ref, mk_ref, mv_ref, y_ref, s_ref, p_ref):
    tile = pl.program_id(1)
    tm = x_ref.shape[1]
    groups_per_kv = (SWA_Q_HEADS // SWA_KV_HEADS) // 2

    @pl.when(tile == 0)
    def _():
        mkv = jnp.dot(mem_ref[0].astype(bf16), w_mem_ref[...], preferred_element_type=f32)
        mk_ref[...] = mkv[:, :W_C].astype(bf16)
        mv_ref[...] = mkv[:, W_C:].astype(bf16)

    for win_ref, prev_ref, cur_ref in ((k_win_ref, ka_prev_ref, ka_ref),
                                       (v_win_ref, va_prev_ref, va_ref)):
        win_ref[0:BLOCK] = prev_ref[0]
        win_ref[BLOCK:BLOCK + tm] = cur_ref[0]

    first_row = lax.broadcasted_iota(jnp.int32, (BLOCK, 1), 0) == 0
    first_col = lax.broadcasted_iota(jnp.int32, (1, BLOCK), 1) == 0

    def with_sink_slot(kv):
        return jnp.concatenate([jnp.where(first_row, jnp.zeros_like(kv[:BLOCK]), kv[:BLOCK]),
                                kv[BLOCK:]], axis=0)

    def sink_bias(bias, head):
        sink = sink_ref[head] * LOG2_E
        return jnp.concatenate([jnp.where(first_col, sink, bias[:, :BLOCK]), bias[:, BLOCK:]],
                               axis=1)

    for j in range(tm // BLOCK):
        chains, slots = [], []
        rows = slice(j * BLOCK, (j + 1) * BLOCK)
        bias = bias_ref[jnp.minimum(tile, 1)] if j == 0 else bias_ref[1]
        k = k_win_ref[j * BLOCK:(j + 2) * BLOCK]
        v = v_win_ref[j * BLOCK:(j + 2) * BLOCK]
        for g in range(W_A // PAIR):
            kv = (g // groups_per_kv) * PAIR
            chains.append((qa_ref[0, rows, g * PAIR:(g + 1) * PAIR],
                           with_sink_slot(k[:, kv:kv + PAIR]), with_sink_slot(v[:, kv:kv + PAIR]),
                           (sink_bias(bias, 2 * g), sink_bias(bias, 2 * g + 1))))
            slots.append((rows, g * PAIR))
        for g in range(W_C // PAIR):
            c = g * PAIR
            chains.append((qc_ref[0, rows, c:c + PAIR],
                           mk_ref[:, c:c + PAIR], mv_ref[:, c:c + PAIR], None))
            slots.append((rows, W_A + W_B + c))
        for (rows, c), (out, _) in zip(slots, _attend(chains, s_ref, p_ref)):
            y_ref[rows, c:c + PAIR] = out

    def rms_normalize(t):
        return t * lax.rsqrt(jnp.mean(t * t, axis=-1, keepdims=True) + RMS_EPS)

    y = jnp.concatenate([rms_normalize(y_ref[:, 0:W_A]),
                         rms_normalize(ob_ref[0].astype(f32)),
                         rms_normalize(y_ref[:, W_A + W_B:D_MIX])], axis=-1)
    y = (y * gate_ref[0].astype(f32)).astype(bf16)
    r = DEEPNORM_ALPHA * x_ref[0] + jnp.dot(y, w_out_ref[...], preferred_element_type=f32)
    mu = jnp.mean(r, axis=-1, keepdims=True)
    d = r - mu
    var = jnp.mean(d * d, axis=-1, keepdims=True)
    out_ref[0] = d * lax.rsqrt(var + LN_EPS) * gain_ref[...] + ln_bias_ref[...]


def _tail_bias():
    qi = np.arange(BLOCK)[:, None]
    kj = np.arange(2 * BLOCK)[None, :]
    dist = qi + BLOCK - kj
    band = (dist >= 0) & (dist <= SWA_WINDOW - 1)
    return np.where(np.stack([band & (kj >= BLOCK), band]), 0.0, MASKED).astype(np.float32)


def _tail(sinks, qa, ka, va, qc, mem, w_mem, ob, gate, x, w_out, ln_gain, ln_bias):
    B, S, _ = x.shape
    M = mem.shape[1]
    tm = ROW_TILE
    blocks_per_tile = tm // BLOCK
    bias = jnp.asarray(_tail_bias())
    row_block = lambda width: pl.BlockSpec((1, tm, width), lambda b, i: (b, i, 0))
    prev_block = lambda width: pl.BlockSpec(
        (1, BLOCK, width), lambda b, i: (b, jnp.maximum(i * blocks_per_tile - 1, 0), 0))
    const = lambda shape: pl.BlockSpec(shape, lambda b, i: (0,) * len(shape))
    kv_w = 2 * W_KV_A
    stage = ((W_A + W_C) // PAIR, 2 * BLOCK, 2 * BLOCK)
    pipelined = (_nbytes((tm, W_A + 2 * kv_w + W_C + W_B + D_MIX), bf16)
                 + 2 * _nbytes((BLOCK, kv_w), bf16) + _nbytes((M, D_MODEL), f32)
                 + _nbytes((D_MODEL, 2 * W_C), bf16) + 2 * _nbytes((tm, D_MODEL), f32)
                 + _nbytes((D_MIX, D_MODEL), bf16) + _nbytes(bias.shape, f32))
    scratch = [pltpu.VMEM((BLOCK + tm, kv_w), bf16), pltpu.VMEM((BLOCK + tm, kv_w), bf16),
               pltpu.VMEM((M, W_C), bf16), pltpu.VMEM((M, W_C), bf16),
               pltpu.VMEM((tm, D_MIX), f32), pltpu.VMEM(stage, f32), pltpu.VMEM(stage, bf16)]
    scratch_bytes = (2 * _nbytes((BLOCK + tm, kv_w), bf16) + 2 * _nbytes((M, W_C), bf16)
                     + 4 * _nbytes((tm, D_MIX), f32) + _nbytes(stage, f32) + _nbytes(stage, bf16))
    return pl.pallas_call(
        _tail_kernel,
        grid=(B, S // tm),
        in_specs=[pl.BlockSpec(memory_space=pltpu.SMEM),
                  row_block(W_A), row_block(kv_w), prev_block(kv_w),
                  row_block(kv_w), prev_block(kv_w), row_block(W_C),
                  pl.BlockSpec((1, M, D_MODEL), lambda b, i: (b, 0, 0)),
                  const((D_MODEL, 2 * W_C)), row_block(W_B), row_block(D_MIX),
                  row_block(D_MODEL), const((D_MIX, D_MODEL)),
                  const((1, D_MODEL)), const((1, D_MODEL)), const(bias.shape)],
        out_specs=row_block(D_MODEL),
        out_shape=jax.ShapeDtypeStruct((B, S, D_MODEL), f32),
        scratch_shapes=scratch,
        compiler_params=pltpu.CompilerParams(
            dimension_semantics=("arbitrary", "arbitrary"),
            vmem_limit_bytes=_vmem_limit(pipelined, scratch_bytes)),
        name="mix_out",
    )(sinks, qa, ka, ka, va, va, qc, mem, w_mem, ob, gate, x, w_out, ln_gain, ln_bias, bias)


def _rope_tables(seq_len):
    pos = jnp.arange(seq_len, dtype=f32)
    inv = ROPE_THETA ** (-jnp.arange(0, HEAD_DIM, 2, dtype=f32) / HEAD_DIM)
    ang = pos[:, None] * inv[None, :]
    cos, sin = jnp.cos(ang), jnp.sin(ang)
    cos2 = jnp.concatenate([cos, cos, cos, cos], axis=-1)
    sin2 = jnp.concatenate([-sin, sin, -sin, sin], axis=-1)
    return cos2, sin2


def kernel(x, mem, w_in, b_in, w_mem, attn_sinks, g_branch, w_out, ln_gain, ln_bias):
    assert w_in.shape[0] == DEPTH
    S = x.shape[1]
    cos2, sin2 = _rope_tables(S)
    for l in range(DEPTH):
        qa, ka, va, qc, gate, *views_b = _in_proj(
            x, w_in[l].astype(bf16), b_in[l][None, :], cos2, sin2, g_branch[l][None, :])
        ob = _dilated(views_b)
        x = _tail(attn_sinks[l], qa, ka, va, qc, mem, w_mem[l].astype(bf16), ob, gate, x,
                  w_out[l].astype(bf16), ln_gain[l][None, :], ln_bias[l][None, :])
    return x
```

```python
import math

import numpy as np
import jax
import jax.numpy as jnp
from jax import lax
from jax.experimental import pallas as pl
from jax.experimental.pallas import tpu as pltpu

D_MODEL = 1024
HEAD_DIM = 64
SWA_Q_HEADS = 8
SWA_KV_HEADS = 2
SWA_WINDOW = 128
DIL_HEADS = 4
DIL_PAIRS = ((128, 1), (512, 4), (2048, 16))
MEM_HEADS = 4
BLOCK = 128
ROPE_THETA = 10000.0
LN_EPS = 1e-5
RMS_EPS = 1e-6
DEPTH = 1
DEEPNORM_ALPHA = (2 * DEPTH) ** 0.25

W_A = SWA_Q_HEADS * HEAD_DIM
W_KV_A = SWA_KV_HEADS * HEAD_DIM
W_B = DIL_HEADS * HEAD_DIM
W_C = MEM_HEADS * HEAD_DIM
D_MIX = W_A + W_B + W_C
D_IN = W_A + 2 * W_KV_A + 3 * W_B + W_C + D_MIX

V7X_LANES = 128
V7X_VMEM_BYTES = 64 * 2**20
VMEM_HEADROOM_BYTES = 8 * 2**20

PAIR = 2 * HEAD_DIM
assert PAIR == V7X_LANES
LOG2_E = math.log2(math.e)
Q_SCALE = HEAD_DIM ** -0.5 * LOG2_E
MASKED = -1e30

OFF_QA = 0
OFF_KA = OFF_QA + W_A
OFF_VA = OFF_KA + W_KV_A
OFF_QB = OFF_VA + W_KV_A
OFF_KB = OFF_QB + W_B
OFF_VB = OFF_KB + W_B
OFF_QC = OFF_VB + W_B
OFF_Z = OFF_QC + W_C
assert OFF_Z + D_MIX == D_IN

ROW_TILE = 512
DIL_BLOCKS_PER_STEP = 4
DIL16_STREAMS_PER_STEP = 8

f32 = jnp.float32
bf16 = jnp.bfloat16


def _vmem_limit(pipelined_bytes, scratch_bytes=0):
    budget = V7X_VMEM_BYTES - VMEM_HEADROOM_BYTES
    assert 2 * pipelined_bytes + scratch_bytes + VMEM_HEADROOM_BYTES <= budget
    return budget


def _nbytes(shape, dtype):
    return int(np.prod(shape)) * jnp.dtype(dtype).itemsize


def _lane_is_first_head():
    return lax.broadcasted_iota(jnp.int32, (1, V7X_LANES), 1) < HEAD_DIM


def _in_proj_kernel(x_ref, w_ref, b_ref, cos_ref, sin_ref, g_ref,
                    qa_ref, ka_ref, va_ref, qc_ref, gate_ref,
                    q1_ref, k1_ref, v1_ref, q4_ref, k4_ref, v4_ref, q16_ref, k16_ref, v16_ref,
                    *stage_refs):
    tm = x_ref.shape[1]
    xb = x_ref[0].astype(bf16)
    cos = cos_ref[...]
    sin = sin_ref[...]
    lane = lax.broadcasted_iota(jnp.int32, (1, V7X_LANES), 1)
    first_half = (lane % HEAD_DIM) < (HEAD_DIM // 2)
    first_head = lane < HEAD_DIM

    def proj(off, width):
        acc = jnp.dot(xb, w_ref[:, off:off + width], preferred_element_type=f32)
        return acc + b_ref[:, off:off + width]

    def rope(a):
        rot = jnp.where(first_half,
                        pltpu.roll(a, V7X_LANES - HEAD_DIM // 2, 1),
                        pltpu.roll(a, HEAD_DIM // 2, 1))
        return a * cos + rot * sin

    def groups(a):
        return [a[:, i:i + V7X_LANES] for i in range(0, a.shape[1], V7X_LANES)]

    def both_halves(a):
        swapped = pltpu.roll(a, HEAD_DIM, 1)
        return jnp.where(first_head, a, swapped), jnp.where(first_head, swapped, a)

    for c in range(0, D_MIX, 2 * V7X_LANES):
        z = proj(OFF_Z + c, 2 * V7X_LANES)
        gate = z * jax.nn.sigmoid(z) * g_ref[:, c:c + 2 * V7X_LANES]
        gate_ref[0, :, c:c + 2 * V7X_LANES] = gate.astype(bf16)
    for c in range(0, W_A, 2 * V7X_LANES):
        for i, a in enumerate(groups(proj(OFF_QA + c, 2 * V7X_LANES))):
            lo = c + i * V7X_LANES
            qa_ref[0, :, lo:lo + V7X_LANES] = (rope(a) * Q_SCALE).astype(bf16)
    k_a, v_a = groups(proj(OFF_KA, 2 * W_KV_A))
    for ref, a in ((ka_ref, rope(k_a)), (va_ref, v_a)):
        kv0, kv1 = both_halves(a)
        ref[0, :, 0:V7X_LANES] = kv0.astype(bf16)
        ref[0, :, V7X_LANES:2 * V7X_LANES] = kv1.astype(bf16)
    mixer_b = ((OFF_QB, lambda a: rope(a) * Q_SCALE, (q1_ref, q4_ref, q16_ref)),
               (OFF_KB, rope, (k1_ref, k4_ref, k16_ref)),
               (OFF_VB, lambda a: a, (v1_ref, v4_ref, v16_ref)))
    dil4, dil16 = DIL_PAIRS[1][1], DIL_PAIRS[2][1]
    for t, (off, fn, (nat_ref, r4_ref, r16_ref)) in enumerate(mixer_b):
        for g, a in enumerate(groups(proj(off, W_B))):
            a = fn(a)
            nat_ref[0, :, g * PAIR:(g + 1) * PAIR] = a.astype(bf16)
            slab = t * (W_B // PAIR) + g
            stage_ref, stage4_ref = stage_refs[2 * slab], stage_refs[2 * slab + 1]
            stage_ref[...] = a
            lanes = slice(g * PAIR, (g + 1) * PAIR)
            rows4, rows16 = tm // dil4, tm // dil16
            for j4 in range(dil4):
                stream = stage_ref[pl.ds(j4, rows4, stride=dil4), :]
                r4_ref[0, j4, :, lanes] = stream.astype(bf16)
                stage4_ref[j4 * rows4:(j4 + 1) * rows4] = stream
            for j16 in range(dil16):
                j4, phase = j16 % dil4, j16 // dil4
                stream = stage4_ref[pl.ds(j4 * rows4 + phase, rows16, stride=dil16 // dil4), :]
                r16_ref[0, j16, :, lanes] = stream.astype(bf16)
    qc_ref[0] = (proj(OFF_QC, W_C) * Q_SCALE).astype(bf16)


def _in_proj(x, w_in, b_in, cos2, sin2, g_branch):
    B, S, _ = x.shape
    tm = ROW_TILE
    widths = (W_A, 2 * W_KV_A, 2 * W_KV_A, W_C, D_MIX, W_B, W_B, W_B)
    row_block = lambda width: pl.BlockSpec((1, tm, width), lambda b, i: (b, i, 0))
    out_specs = [row_block(w) for w in widths]
    out_shape = [jax.ShapeDtypeStruct((B, S, w), bf16) for w in widths]
    for _, r in DIL_PAIRS[1:]:
        out_specs += [pl.BlockSpec((1, r, tm // r, W_B), lambda b, i: (b, 0, i, 0))] * 3
        out_shape += [jax.ShapeDtypeStruct((B, r, S // r, W_B), bf16)] * 3
    const = lambda shape: pl.BlockSpec(shape, lambda b, i: (0,) * len(shape))
    n_stage, stage = 2 * 3 * (W_B // PAIR), (tm, V7X_LANES)
    pipelined = (_nbytes((tm, D_MODEL), f32) + _nbytes((D_MODEL, D_IN), bf16)
                 + _nbytes((tm, sum(widths) + 6 * W_B), bf16) + 2 * _nbytes((tm, V7X_LANES), f32))
    return pl.pallas_call(
        _in_proj_kernel,
        grid=(B, S // tm),
        in_specs=[row_block(D_MODEL), const((D_MODEL, D_IN)), const((1, D_IN)),
                  pl.BlockSpec((tm, V7X_LANES), lambda b, i: (i, 0)),
                  pl.BlockSpec((tm, V7X_LANES), lambda b, i: (i, 0)),
                  const((1, D_MIX))],
        out_specs=out_specs,
        out_shape=out_shape,
        scratch_shapes=[pltpu.VMEM(stage, f32)] * n_stage,
        compiler_params=pltpu.CompilerParams(
            dimension_semantics=("arbitrary", "arbitrary"),
            vmem_limit_bytes=_vmem_limit(pipelined, n_stage * _nbytes(stage, f32))),
        name="in_proj",
    )(x, w_in, b_in, cos2, sin2, g_branch)


def _attend(chains, s_ref, p_ref):
    first_head = _lane_is_first_head()
    for c, (q2, k2, _, _) in enumerate(chains):
        zero = jnp.zeros_like(q2)
        stacked = jnp.concatenate([jnp.where(first_head, q2, zero),
                                   jnp.where(first_head, zero, q2)], axis=0)
        s_ref[c, :, 0:k2.shape[0]] = lax.dot_general(
            stacked, k2, (((1,), (1,)), ((), ())), preferred_element_type=f32)
    tops = []
    for c, (q2, k2, _, biases) in enumerate(chains):
        m_rows, n = q2.shape[0], k2.shape[0]
        pair_tops = []
        for h in range(2):
            rows = slice(h * m_rows, (h + 1) * m_rows)
            sh = s_ref[c, rows, 0:n]
            if biases is not None:
                sh = sh + biases[h]
            m = jnp.max(sh, axis=-1, keepdims=True)
            p_ref[c, rows, 0:n] = jnp.exp2(sh - m).astype(bf16)
            pair_tops.append(m)
        tops.append(pair_tops)
    results = []
    for c, (q2, k2, v2, _) in enumerate(chains):
        m_rows, n = q2.shape[0], k2.shape[0]
        v_ones = jnp.concatenate([v2, jnp.ones_like(v2)], axis=1)
        o = jnp.dot(p_ref[c, :, 0:n], v_ones, preferred_element_type=f32)
        acc = jnp.where(first_head, o[:m_rows, :PAIR], o[m_rows:, :PAIR])
        denom = jnp.where(first_head, o[:m_rows, PAIR:], o[m_rows:, PAIR:])
        lse = jnp.where(first_head, tops[c][0], tops[c][1]) + jnp.log(denom) * LOG2_E
        results.append((acc * (1.0 / denom), lse))
    return results


def _window_bias(max_dist):
    qi = np.arange(BLOCK)[:, None]
    kj = np.arange(2 * BLOCK)[None, :]
    dist = np.stack([qi - kj, qi + BLOCK - kj])
    return np.where((dist >= 0) & (dist <= max_dist), 0.0, MASKED).astype(np.float32)


def _window(blk):
    r0 = pl.multiple_of(blk * BLOCK, BLOCK)
    k0 = pl.multiple_of(jnp.maximum(blk - 1, 0) * BLOCK, BLOCK)
    return r0, k0, jnp.minimum(blk, 1)


def _merge(o_a, l_a, o_b, l_b):
    top = jnp.maximum(l_a, l_b)
    e_a = jnp.exp2(l_a - top)
    e_b = jnp.exp2(l_b - top)
    denom = e_a + e_b
    return (e_a * o_a + e_b * o_b) * (1.0 / denom), top + jnp.log(denom) * LOG2_E


def _dilated_kernel(q1_ref, k1_ref, v1_ref, q4_ref, k4_ref, v4_ref, q16_ref, k16_ref, v16_ref,
                    bias_ref, o_ref, o4_ref, l4_ref, o16_ref, l16_ref, s_ref, p_ref):
    seq_len = q1_ref.shape[1]
    n_pairs = W_B // PAIR
    dil4, dil16 = DIL_PAIRS[1][1], DIL_PAIRS[2][1]
    sub = dil16 // dil4

    causal = bias_ref[0, :, 0:BLOCK]

    def body16(i, carry):
        streams = [i * DIL16_STREAMS_PER_STEP + u for u in range(DIL16_STREAMS_PER_STEP)]
        chains = [(q16_ref[0, j, :, g * PAIR:(g + 1) * PAIR],
                   k16_ref[0, j, :, g * PAIR:(g + 1) * PAIR],
                   v16_ref[0, j, :, g * PAIR:(g + 1) * PAIR], (causal, causal))
                  for j in streams for g in range(n_pairs)]
        results = _attend(chains, s_ref, p_ref)
        for n, (out, lse) in enumerate(results):
            j, g = streams[n // n_pairs], n % n_pairs
            stream4, phase = lax.rem(j, dil4), lax.div(j, dil4)
            o16_ref[g, stream4, pl.ds(phase, BLOCK, stride=sub), :] = out
            l16_ref[g, stream4, pl.ds(phase, BLOCK, stride=sub), :] = lse
        return carry

    lax.fori_loop(0, dil16 // DIL16_STREAMS_PER_STEP, body16, 0)

    def body4(blk, carry):
        u0, k0, first = _window(blk)
        bias = bias_ref[first]
        chains = [(q4_ref[0, j, pl.ds(u0, BLOCK), g * PAIR:(g + 1) * PAIR],
                   k4_ref[0, j, pl.ds(k0, 2 * BLOCK), g * PAIR:(g + 1) * PAIR],
                   v4_ref[0, j, pl.ds(k0, 2 * BLOCK), g * PAIR:(g + 1) * PAIR], (bias, bias))
                  for j in range(dil4) for g in range(n_pairs)]
        results = _attend(chains, s_ref, p_ref)
        for n, (out, lse) in enumerate(results):
            j, g = n // n_pairs, n % n_pairs
            out, lse = _merge(out, lse, o16_ref[g, j, pl.ds(u0, BLOCK), :],
                              l16_ref[g, j, pl.ds(u0, BLOCK), :])
            o4_ref[g, pl.ds(u0 * dil4 + j, BLOCK, stride=dil4), :] = out
            l4_ref[g, pl.ds(u0 * dil4 + j, BLOCK, stride=dil4), :] = lse
        return carry

    lax.fori_loop(0, seq_len // dil4 // BLOCK, body4, 0)

    def body1(i, carry):
        windows = [_window(i * DIL_BLOCKS_PER_STEP + u) for u in range(DIL_BLOCKS_PER_STEP)]
        chains = [(q1_ref[0, pl.ds(r0, BLOCK), g * PAIR:(g + 1) * PAIR],
                   k1_ref[0, pl.ds(k0, 2 * BLOCK), g * PAIR:(g + 1) * PAIR],
                   v1_ref[0, pl.ds(k0, 2 * BLOCK), g * PAIR:(g + 1) * PAIR],
                   (bias_ref[first], bias_ref[first]))
                  for r0, k0, first in windows for g in range(n_pairs)]
        results = _attend(chains, s_ref, p_ref)
        for n, (out, lse) in enumerate(results):
            r0, g = windows[n // n_pairs][0], n % n_pairs
            out, _ = _merge(out, lse, o4_ref[g, pl.ds(r0, BLOCK), :],
                            l4_ref[g, pl.ds(r0, BLOCK), :])
            o_ref[0, pl.ds(r0, BLOCK), g * PAIR:(g + 1) * PAIR] = out.astype(bf16)
        return carry

    lax.fori_loop(0, seq_len // BLOCK // DIL_BLOCKS_PER_STEP, body1, 0)


def _dilated(views):
    B, S, _ = views[0].shape
    assert all(w // r == BLOCK for w, r in DIL_PAIRS) and S // DIL_PAIRS[2][1] == BLOCK
    assert DIL_PAIRS[2][1] % DIL16_STREAMS_PER_STEP == 0
    bias = jnp.asarray(_window_bias(BLOCK))
    view_specs = [pl.BlockSpec((1, S, W_B), lambda b: (b, 0, 0))] * 3
    for _, r in DIL_PAIRS[1:]:
        view_specs += [pl.BlockSpec((1, r, S // r, W_B), lambda b: (b, 0, 0, 0))] * 3
    slab = (W_B // PAIR, S, V7X_LANES)
    slab16 = (W_B // PAIR, DIL_PAIRS[1][1], S // DIL_PAIRS[1][1], V7X_LANES)
    chains = max(DIL_BLOCKS_PER_STEP, DIL16_STREAMS_PER_STEP, DIL_PAIRS[1][1]) * (W_B // PAIR)
    stage = (chains, 2 * BLOCK, 2 * BLOCK)
    pipelined = 10 * _nbytes((S, W_B), bf16) + _nbytes(bias.shape, f32)
    return pl.pallas_call(
        _dilated_kernel,
        grid=(B,),
        in_specs=view_specs + [pl.BlockSpec(bias.shape, lambda b: (0, 0, 0))],
        out_specs=pl.BlockSpec((1, S, W_B), lambda b: (b, 0, 0)),
        out_shape=jax.ShapeDtypeStruct((B, S, W_B), bf16),
        scratch_shapes=[pltpu.VMEM(slab, f32), pltpu.VMEM(slab, f32),
                        pltpu.VMEM(slab16, f32), pltpu.VMEM(slab16, f32),
                        pltpu.VMEM(stage, f32), pltpu.VMEM(stage, bf16)],
        compiler_params=pltpu.CompilerParams(
            dimension_semantics=("arbitrary",),
            vmem_limit_bytes=_vmem_limit(
                pipelined, 4 * _nbytes(slab, f32) + _nbytes(stage, f32) + _nbytes(stage, bf16))),
        name="dilated_attention",
    )(*views, bias)


def _tail_kernel(sink_ref, qa_ref, ka_ref, ka_prev_ref, va_ref, va_prev_ref, qc_ref, mem_ref,
                 w_mem_ref, ob_ref, gate_ref, x_ref, w_out_ref, gain_ref, ln_bias_ref, bias_ref,
                 out_ref, k_win_ref, v_win_ref, mk_ref, mv_ref, y_ref, s_ref, p_ref):
    tile = pl.program_id(1)
    tm = x_ref.shape[1]
    groups_per_kv = (SWA_Q_HEADS // SWA_KV_HEADS) // 2

    @pl.when(tile == 0)
    def _():
        mkv = jnp.dot(mem_ref[0].astype(bf16), w_mem_ref[...], preferred_element_type=f32)
        mk_ref[...] = mkv[:, :W_C].astype(bf16)
        mv_ref[...] = mkv[:, W_C:].astype(bf16)

    for win_ref, prev_ref, cur_ref in ((k_win_ref, ka_prev_ref, ka_ref),
                                       (v_win_ref, va_prev_ref, va_ref)):
        win_ref[0:BLOCK] = prev_ref[0]
        win_ref[BLOCK:BLOCK + tm] = cur_ref[0]

    first_row = lax.broadcasted_iota(jnp.int32, (BLOCK, 1), 0) == 0
    first_col = lax.broadcasted_iota(jnp.int32, (1, BLOCK), 1) == 0

    def with_sink_slot(kv):
        return jnp.concatenate([jnp.where(first_row, jnp.zeros_like(kv[:BLOCK]), kv[:BLOCK]),
                                kv[BLOCK:]], axis=0)

    def sink_bias(bias, head):
        sink = sink_ref[head] * LOG2_E
        return jnp.concatenate([jnp.where(first_col, sink, bias[:, :BLOCK]), bias[:, BLOCK:]],
                               axis=1)

    for j in range(tm // BLOCK):
        chains, slots = [], []
        rows = slice(j * BLOCK, (j + 1) * BLOCK)
        bias = bias_ref[jnp.minimum(tile, 1)] if j == 0 else bias_ref[1]
        k = k_win_ref[j * BLOCK:(j + 2) * BLOCK]
        v = v_win_ref[j * BLOCK:(j + 2) * BLOCK]
        for g in range(W_A // PAIR):
            kv = (g // groups_per_kv) * PAIR
            chains.append((qa_ref[0, rows, g * PAIR:(g + 1) * PAIR],
                           with_sink_slot(k[:, kv:kv + PAIR]), with_sink_slot(v[:, kv:kv + PAIR]),
                           (sink_bias(bias, 2 * g), sink_bias(bias, 2 * g + 1))))
            slots.append((rows, g * PAIR))
        for g in range(W_C // PAIR):
            c = g * PAIR
            chains.append((qc_ref[0, rows, c:c + PAIR],
                           mk_ref[:, c:c + PAIR], mv_ref[:, c:c + PAIR], None))
            slots.append((rows, W_A + W_B + c))
        for (rows, c), (out, _) in zip(slots, _attend(chains, s_ref, p_ref)):
            y_ref[rows, c:c + PAIR] = out

    def rms_normalize(t):
        return t * lax.rsqrt(jnp.mean(t * t, axis=-1, keepdims=True) + RMS_EPS)

    y = jnp.concatenate([rms_normalize(y_ref[:, 0:W_A]),
                         rms_normalize(ob_ref[0].astype(f32)),
                         rms_normalize(y_ref[:, W_A + W_B:D_MIX])], axis=-1)
    y = (y * gate_ref[0].astype(f32)).astype(bf16)
    r = DEEPNORM_ALPHA * x_ref[0] + jnp.dot(y, w_out_ref[...], preferred_element_type=f32)
    mu = jnp.mean(r, axis=-1, keepdims=True)
    d = r - mu
    var = jnp.mean(d * d, axis=-1, keepdims=True)
    out_ref[0] = d * lax.rsqrt(var + LN_EPS) * gain_ref[...] + ln_bias_ref[...]


def _tail_bias():
    qi = np.arange(BLOCK)[:, None]
    kj = np.arange(2 * BLOCK)[None, :]
    dist = qi + BLOCK - kj
    band = (dist >= 0) & (dist <= SWA_WINDOW - 1)
    return np.where(np.stack([band & (kj >= BLOCK), band]), 0.0, MASKED).astype(np.float32)


def _tail(sinks, qa, ka, va, qc, mem, w_mem, ob, gate, x, w_out, ln_gain, ln_bias):
    B, S, _ = x.shape
    M = mem.shape[1]
    tm = ROW_TILE
    blocks_per_tile = tm // BLOCK
    bias = jnp.asarray(_tail_bias())
    row_block = lambda width: pl.BlockSpec((1, tm, width), lambda b, i: (b, i, 0))
    prev_block = lambda width: pl.BlockSpec(
        (1, BLOCK, width), lambda b, i: (b, jnp.maximum(i * blocks_per_tile - 1, 0), 0))
    const = lambda shape: pl.BlockSpec(shape, lambda b, i: (0,) * len(shape))
    kv_w = 2 * W_KV_A
    stage = ((W_A + W_C) // PAIR, 2 * BLOCK, 2 * BLOCK)
    pipelined = (_nbytes((tm, W_A + 2 * kv_w + W_C + W_B + D_MIX), bf16)
                 + 2 * _nbytes((BLOCK, kv_w), bf16) + _nbytes((M, D_MODEL), f32)
                 + _nbytes((D_MODEL, 2 * W_C), bf16) + 2 * _nbytes((tm, D_MODEL), f32)
                 + _nbytes((D_MIX, D_MODEL), bf16) + _nbytes(bias.shape, f32))
    scratch = [pltpu.VMEM((BLOCK + tm, kv_w), bf16), pltpu.VMEM((BLOCK + tm, kv_w), bf16),
               pltpu.VMEM((M, W_C), bf16), pltpu.VMEM((M, W_C), bf16),
               pltpu.VMEM((tm, D_MIX), f32), pltpu.VMEM(stage, f32), pltpu.VMEM(stage, bf16)]
    scratch_bytes = (2 * _nbytes((BLOCK + tm, kv_w), bf16) + 2 * _nbytes((M, W_C), bf16)
                     + 4 * _nbytes((tm, D_MIX), f32) + _nbytes(stage, f32) + _nbytes(stage, bf16))
    return pl.pallas_call(
        _tail_kernel,
        grid=(B, S // tm),
        in_specs=[pl.BlockSpec(memory_space=pltpu.SMEM),
                  row_block(W_A), row_block(kv_w), prev_block(kv_w),
                  row_block(kv_w), prev_block(kv_w), row_block(W_C),
                  pl.BlockSpec((1, M, D_MODEL), lambda b, i: (b, 0, 0)),
                  const((D_MODEL, 2 * W_C)), row_block(W_B), row_block(D_MIX),
                  row_block(D_MODEL), const((D_MIX, D_MODEL)),
                  const((1, D_MODEL)), const((1, D_MODEL)), const(bias.shape)],
        out_specs=row_block(D_MODEL),
        out_shape=jax.ShapeDtypeStruct((B, S, D_MODEL), f32),
        scratch_shapes=scratch,
        compiler_params=pltpu.CompilerParams(
            dimension_semantics=("arbitrary", "arbitrary"),
            vmem_limit_bytes=_vmem_limit(pipelined, scratch_bytes)),
        name="mix_out",
    )(sinks, qa, ka, ka, va, va, qc, mem, w_mem, ob, gate, x, w_out, ln_gain, ln_bias, bias)


def _rope_tables(seq_len):
    pos = jnp.arange(seq_len, dtype=f32)
    inv = ROPE_THETA ** (-jnp.arange(0, HEAD_DIM, 2, dtype=f32) / HEAD_DIM)
    ang = pos[:, None] * inv[None, :]
    cos, sin = jnp.cos(ang), jnp.sin(ang)
    cos2 = jnp.concatenate([cos, cos, cos, cos], axis=-1)
    sin2 = jnp.concatenate([-sin, sin, -sin, sin], axis=-1)
    return cos2, sin2


def kernel(x, mem, w_in, b_in, w_mem, attn_sinks, g_branch, w_out, ln_gain, ln_bias):
    assert w_in.shape[0] == DEPTH
    S = x.shape[1]
    cos2, sin2 = _rope_tables(S)
    for l in range(DEPTH):
        qa, ka, va, qc, gate, *views_b = _in_proj(
            x, w_in[l].astype(bf16), b_in[l][None, :], cos2, sin2, g_branch[l][None, :])
        ob = _dilated(views_b)
        x = _tail(attn_sinks[l], qa, ka, va, qc, mem, w_mem[l].astype(bf16), ob, gate, x,
                  w_out[l].astype(bf16), ln_gain[l][None, :], ln_bias[l][None, :])
    return x
```

```python
import functools
import math

import numpy as np
import jax
import jax.numpy as jnp
from jax import lax
from jax.experimental import pallas as pl
from jax.experimental.pallas import tpu as pltpu

D_MODEL = 1024
HEAD_DIM = 64
SWA_Q_HEADS = 8
SWA_KV_HEADS = 2
SWA_WINDOW = 128
DIL_HEADS = 4
DIL_PAIRS = ((128, 1), (512, 4), (2048, 16))
MEM_HEADS = 4
BLOCK = 128
ROPE_THETA = 10000.0
LN_EPS = 1e-5
RMS_EPS = 1e-6
DEPTH = 1
DEEPNORM_ALPHA = (2 * DEPTH) ** 0.25

W_A = SWA_Q_HEADS * HEAD_DIM
W_KV_A = SWA_KV_HEADS * HEAD_DIM
W_B = DIL_HEADS * HEAD_DIM
W_C = MEM_HEADS * HEAD_DIM
D_MIX = W_A + W_B + W_C
D_IN = W_A + 2 * W_KV_A + 3 * W_B + W_C + D_MIX

V7X_LANES = 128
V7X_VMEM_BYTES = 64 * 2**20
VMEM_HEADROOM_BYTES = 8 * 2**20

PAIR = 2 * HEAD_DIM
assert PAIR == V7X_LANES
LOG2_E = math.log2(math.e)
Q_SCALE = HEAD_DIM ** -0.5 * LOG2_E
MASKED = -1e30

OFF_QA = 0
OFF_KA = OFF_QA + W_A
OFF_VA = OFF_KA + W_KV_A
OFF_QB = OFF_VA + W_KV_A
OFF_KB = OFF_QB + W_B
OFF_VB = OFF_KB + W_B
OFF_QC = OFF_VB + W_B
OFF_Z = OFF_QC + W_C
assert OFF_Z + D_MIX == D_IN

ROW_TILE = 512
DIL_BLOCKS_PER_STEP = 4
DIL16_STREAMS_PER_STEP = 8

f32 = jnp.float32
bf16 = jnp.bfloat16


def _vmem_limit(pipelined_bytes, scratch_bytes=0):
    budget = V7X_VMEM_BYTES - VMEM_HEADROOM_BYTES
    assert 2 * pipelined_bytes + scratch_bytes + VMEM_HEADROOM_BYTES <= budget
    return budget


def _nbytes(shape, dtype):
    return int(np.prod(shape)) * jnp.dtype(dtype).itemsize


def _lane_is_first_head():
    return lax.broadcasted_iota(jnp.int32, (1, V7X_LANES), 1) < HEAD_DIM


def _in_proj_kernel(x_ref, w_ref, b_ref, cos_ref, sin_ref, g_ref,
                    qa_ref, ka_ref, va_ref, qc_ref, gate_ref,
                    q1_ref, k1_ref, v1_ref, q4_ref, k4_ref, v4_ref, q16_ref, k16_ref, v16_ref,
                    *stage_refs):
    tm = x_ref.shape[1]
    xb = x_ref[0].astype(bf16)
    cos = cos_ref[...]
    sin = sin_ref[...]
    lane = lax.broadcasted_iota(jnp.int32, (1, V7X_LANES), 1)
    first_half = (lane % HEAD_DIM) < (HEAD_DIM // 2)
    first_head = lane < HEAD_DIM

    def proj(off, width):
        acc = jnp.dot(xb, w_ref[:, off:off + width], preferred_element_type=f32)
        return acc + b_ref[:, off:off + width]

    def rope(a):
        rot = jnp.where(first_half,
                        pltpu.roll(a, V7X_LANES - HEAD_DIM // 2, 1),
                        pltpu.roll(a, HEAD_DIM // 2, 1))
        return a * cos + rot * sin

    def groups(a):
        return [a[:, i:i + V7X_LANES] for i in range(0, a.shape[1], V7X_LANES)]

    def both_halves(a):
        swapped = pltpu.roll(a, HEAD_DIM, 1)
        return jnp.where(first_head, a, swapped), jnp.where(first_head, swapped, a)

    for c in range(0, D_MIX, 2 * V7X_LANES):
        z = proj(OFF_Z + c, 2 * V7X_LANES)
        gate = z * jax.nn.sigmoid(z) * g_ref[:, c:c + 2 * V7X_LANES]
        gate_ref[0, :, c:c + 2 * V7X_LANES] = gate.astype(bf16)
    for c in range(0, W_A, 2 * V7X_LANES):
        for i, a in enumerate(groups(proj(OFF_QA + c, 2 * V7X_LANES))):
            lo = c + i * V7X_LANES
            qa_ref[0, :, lo:lo + V7X_LANES] = (rope(a) * Q_SCALE).astype(bf16)
    k_a, v_a = groups(proj(OFF_KA, 2 * W_KV_A))
    for ref, a in ((ka_ref, rope(k_a)), (va_ref, v_a)):
        kv0, kv1 = both_halves(a)
        ref[0, :, 0:V7X_LANES] = kv0.astype(bf16)
        ref[0, :, V7X_LANES:2 * V7X_LANES] = kv1.astype(bf16)
    mixer_b = ((OFF_QB, lambda a: rope(a) * Q_SCALE, (q1_ref, q4_ref, q16_ref)),
               (OFF_KB, rope, (k1_ref, k4_ref, k16_ref)),
               (OFF_VB, lambda a: a, (v1_ref, v4_ref, v16_ref)))
    dil4, dil16 = DIL_PAIRS[1][1], DIL_PAIRS[2][1]
    for t, (off, fn, (nat_ref, r4_ref, r16_ref)) in enumerate(mixer_b):
        for g, a in enumerate(groups(proj(off, W_B))):
            a = fn(a)
            nat_ref[0, :, g * PAIR:(g + 1) * PAIR] = a.astype(bf16)
            slab = t * (W_B // PAIR) + g
            stage_ref, stage4_ref = stage_refs[2 * slab], stage_refs[2 * slab + 1]
            stage_ref[...] = a
            lanes = slice(g * PAIR, (g + 1) * PAIR)
            rows4, rows16 = tm // dil4, tm // dil16
            for j4 in range(dil4):
                stream = stage_ref[pl.ds(j4, rows4, stride=dil4), :]
                r4_ref[0, j4, :, lanes] = stream.astype(bf16)
                stage4_ref[j4 * rows4:(j4 + 1) * rows4] = stream
            for j16 in range(dil16):
                j4, phase = j16 % dil4, j16 // dil4
                stream = stage4_ref[pl.ds(j4 * rows4 + phase, rows16, stride=dil16 // dil4), :]
                r16_ref[0, j16, :, lanes] = stream.astype(bf16)
    qc_ref[0] = (proj(OFF_QC, W_C) * Q_SCALE).astype(bf16)


def _in_proj(x, w_in, b_in, cos2, sin2, g_branch):
    B, S, _ = x.shape
    tm = ROW_TILE
    widths = (W_A, 2 * W_KV_A, 2 * W_KV_A, W_C, D_MIX, W_B, W_B, W_B)
    row_block = lambda width: pl.BlockSpec((1, tm, width), lambda b, i: (b, i, 0))
    out_specs = [row_block(w) for w in widths]
    out_shape = [jax.ShapeDtypeStruct((B, S, w), bf16) for w in widths]
    for _, r in DIL_PAIRS[1:]:
        out_specs += [pl.BlockSpec((1, r, tm // r, W_B), lambda b, i: (b, 0, i, 0))] * 3
        out_shape += [jax.ShapeDtypeStruct((B, r, S // r, W_B), bf16)] * 3
    const = lambda shape: pl.BlockSpec(shape, lambda b, i: (0,) * len(shape))
    n_stage, stage = 2 * 3 * (W_B // PAIR), (tm, V7X_LANES)
    pipelined = (_nbytes((tm, D_MODEL), f32) + _nbytes((D_MODEL, D_IN), bf16)
                 + _nbytes((tm, sum(widths) + 6 * W_B), bf16) + 2 * _nbytes((tm, V7X_LANES), f32))
    return pl.pallas_call(
        _in_proj_kernel,
        grid=(B, S // tm),
        in_specs=[row_block(D_MODEL), const((D_MODEL, D_IN)), const((1, D_IN)),
                  pl.BlockSpec((tm, V7X_LANES), lambda b, i: (i, 0)),
                  pl.BlockSpec((tm, V7X_LANES), lambda b, i: (i, 0)),
                  const((1, D_MIX))],
        out_specs=out_specs,
        out_shape=out_shape,
        scratch_shapes=[pltpu.VMEM(stage, f32)] * n_stage,
        compiler_params=pltpu.CompilerParams(
            dimension_semantics=("arbitrary", "arbitrary"),
            vmem_limit_bytes=_vmem_limit(pipelined, n_stage * _nbytes(stage, f32))),
        name="in_proj",
    )(x, w_in, b_in, cos2, sin2, g_branch)


def _attend(chains, s_ref, p_ref):
    first_head = _lane_is_first_head()
    for c, (q2, k2, _, _) in enumerate(chains):
        zero = jnp.zeros_like(q2)
        stacked = jnp.concatenate([jnp.where(first_head, q2, zero),
                                   jnp.where(first_head, zero, q2)], axis=0)
        s_ref[c, :, 0:k2.shape[0]] = lax.dot_general(
            stacked, k2, (((1,), (1,)), ((), ())), preferred_element_type=f32)
    tops = []
    for c, (q2, k2, _, biases) in enumerate(chains):
        m_rows, n = q2.shape[0], k2.shape[0]
        pair_tops = []
        for h in range(2):
            rows = slice(h * m_rows, (h + 1) * m_rows)
            sh = s_ref[c, rows, 0:n]
            if biases is not None:
                sh = sh + biases[h]
            m = jnp.max(sh, axis=-1, keepdims=True)
            p_ref[c, rows, 0:n] = jnp.exp2(sh - m).astype(bf16)
            pair_tops.append(m)
        tops.append(pair_tops)
    results = []
    for c, (q2, k2, v2, _) in enumerate(chains):
        m_rows, n = q2.shape[0], k2.shape[0]
        v_ones = jnp.concatenate([v2, jnp.ones_like(v2)], axis=1)
        o = jnp.dot(p_ref[c, :, 0:n], v_ones, preferred_element_type=f32)
        acc = jnp.where(first_head, o[:m_rows, :PAIR], o[m_rows:, :PAIR])
        denom = jnp.where(first_head, o[:m_rows, PAIR:], o[m_rows:, PAIR:])
        lse = jnp.where(first_head, tops[c][0], tops[c][1]) + jnp.log(denom) * LOG2_E
        results.append((acc * (1.0 / denom), lse))
    return results


def _window_bias(max_dist):
    qi = np.arange(BLOCK)[:, None]
    kj = np.arange(2 * BLOCK)[None, :]
    dist = np.stack([qi - kj, qi + BLOCK - kj])
    return np.where((dist >= 0) & (dist <= max_dist), 0.0, MASKED).astype(np.float32)


def _window(blk):
    r0 = pl.multiple_of(blk * BLOCK, BLOCK)
    k0 = pl.multiple_of(jnp.maximum(blk - 1, 0) * BLOCK, BLOCK)
    return r0, k0, jnp.minimum(blk, 1)


def _merge(o_a, l_a, o_b, l_b):
    top = jnp.maximum(l_a, l_b)
    e_a = jnp.exp2(l_a - top)
    e_b = jnp.exp2(l_b - top)
    denom = e_a + e_b
    return (e_a * o_a + e_b * o_b) * (1.0 / denom), top + jnp.log(denom) * LOG2_E


def _dilated_kernel(q1_ref, k1_ref, v1_ref, q4_ref, k4_ref, v4_ref, q16_ref, k16_ref, v16_ref,
                    bias_ref, o_ref, o4_ref, l4_ref, o16_ref, l16_ref, s_ref, p_ref):
    seq_len = q1_ref.shape[1]
    n_pairs = W_B // PAIR
    dil4, dil16 = DIL_PAIRS[1][1], DIL_PAIRS[2][1]
    sub = dil16 // dil4

    causal = bias_ref[0, :, 0:BLOCK]

    def body16(i, carry):
        streams = [i * DIL16_STREAMS_PER_STEP + u for u in range(DIL16_STREAMS_PER_STEP)]
        chains = [(q16_ref[0, j, :, g * PAIR:(g + 1) * PAIR],
                   k16_ref[0, j, :, g * PAIR:(g + 1) * PAIR],
                   v16_ref[0, j, :, g * PAIR:(g + 1) * PAIR], (causal, causal))
                  for j in streams for g in range(n_pairs)]
        results = _attend(chains, s_ref, p_ref)
        for n, (out, lse) in enumerate(results):
            j, g = streams[n // n_pairs], n % n_pairs
            stream4, phase = lax.rem(j, dil4), lax.div(j, dil4)
            o16_ref[g, stream4, pl.ds(phase, BLOCK, stride=sub), :] = out
            l16_ref[g, stream4, pl.ds(phase, BLOCK, stride=sub), :] = lse
        return carry

    lax.fori_loop(0, dil16 // DIL16_STREAMS_PER_STEP, body16, 0)

    def body4(blk, carry):
        u0, k0, first = _window(blk)
        bias = bias_ref[first]
        chains = [(q4_ref[0, j, pl.ds(u0, BLOCK), g * PAIR:(g + 1) * PAIR],
                   k4_ref[0, j, pl.ds(k0, 2 * BLOCK), g * PAIR:(g + 1) * PAIR],
                   v4_ref[0, j, pl.ds(k0, 2 * BLOCK), g * PAIR:(g + 1) * PAIR], (bias, bias))
                  for j in range(dil4) for g in range(n_pairs)]
        results = _attend(chains, s_ref, p_ref)
        for n, (out, lse) in enumerate(results):
            j, g = n // n_pairs, n % n_pairs
            out, lse = _merge(out, lse, o16_ref[g, j, pl.ds(u0, BLOCK), :],
                              l16_ref[g, j, pl.ds(u0, BLOCK), :])
            o4_ref[g, pl.ds(u0 * dil4 + j, BLOCK, stride=dil4), :] = out
            l4_ref[g, pl.ds(u0 * dil4 + j, BLOCK, stride=dil4), :] = lse
        return carry

    lax.fori_loop(0, seq_len // dil4 // BLOCK, body4, 0)

    def body1(i, carry):
        windows = [_window(i * DIL_BLOCKS_PER_STEP + u) for u in range(DIL_BLOCKS_PER_STEP)]
        chains = [(q1_ref[0, pl.ds(r0, BLOCK), g * PAIR:(g + 1) * PAIR],
                   k1_ref[0, pl.ds(k0, 2 * BLOCK), g * PAIR:(g + 1) * PAIR],
                   v1_ref[0, pl.ds(k0, 2 * BLOCK), g * PAIR:(g + 1) * PAIR],
                   (bias_ref[first], bias_ref[first]))
                  for r0, k0, first in windows for g in range(n_pairs)]
        results = _attend(chains, s_ref, p_ref)
        for n, (out, lse) in enumerate(results):
            r0, g = windows[n // n_pairs][0], n % n_pairs
            out, _ = _merge(out, lse, o4_ref[g, pl.ds(r0, BLOCK), :],
                            l4_ref[g, pl.ds(r0, BLOCK), :])
            o_ref[0, pl.ds(r0, BLOCK), g * PAIR:(g + 1) * PAIR] = out.astype(bf16)
        return carry

    lax.fori_loop(0, seq_len // BLOCK // DIL_BLOCKS_PER_STEP, body1, 0)


def _dilated(views):
    B, S, _ = views[0].shape
    assert all(w // r == BLOCK for w, r in DIL_PAIRS) and S // DIL_PAIRS[2][1] == BLOCK
    assert DIL_PAIRS[2][1] % DIL16_STREAMS_PER_STEP == 0
    bias = jnp.asarray(_window_bias(BLOCK))
    view_specs = [pl.BlockSpec((1, S, W_B), lambda b: (b, 0, 0))] * 3
    for _, r in DIL_PAIRS[1:]:
        view_specs += [pl.BlockSpec((1, r, S // r, W_B), lambda b: (b, 0, 0, 0))] * 3
    slab = (W_B // PAIR, S, V7X_LANES)
    slab16 = (W_B // PAIR, DIL_PAIRS[1][1], S // DIL_PAIRS[1][1], V7X_LANES)
    chains = max(DIL_BLOCKS_PER_STEP, DIL16_STREAMS_PER_STEP, DIL_PAIRS[1][1]) * (W_B // PAIR)
    stage = (chains, 2 * BLOCK, 2 * BLOCK)
    pipelined = 10 * _nbytes((S, W_B), bf16) + _nbytes(bias.shape, f32)
    return pl.pallas_call(
        _dilated_kernel,
        grid=(B,),
        in_specs=view_specs + [pl.BlockSpec(bias.shape, lambda b: (0, 0, 0))],
        out_specs=pl.BlockSpec((1, S, W_B), lambda b: (b, 0, 0)),
        out_shape=jax.ShapeDtypeStruct((B, S, W_B), bf16),
        scratch_shapes=[pltpu.VMEM(slab, f32), pltpu.VMEM(slab, f32),
                        pltpu.VMEM(slab16, f32), pltpu.VMEM(slab16, f32),
                        pltpu.VMEM(stage, f32), pltpu.VMEM(stage, bf16)],
        compiler_params=pltpu.CompilerParams(
            dimension_semantics=("arbitrary",),
            vmem_limit_bytes=_vmem_limit(
                pipelined, 4 * _nbytes(slab, f32) + _nbytes(stage, f32) + _nbytes(stage, bf16))),
        name="dilated_attention",
    )(*views, bias)


def _tail_kernel(sink_ref, qa_ref, ka_ref, ka_prev_ref, va_ref, va_prev_ref, qc_ref, mem_ref,
                 w_mem_ref, ob_ref, gate_ref, x_ref, w_out_ref, gain_ref, ln_bias_ref, bias_ref,
                 out_ref, k_win_ref, v_win_ref, mk_ref, mv_ref, y_ref, s_ref, p_ref, r_ref,
                 *, tiles_per_seq, n_tiles):
    step = pl.program_id(0)
    tile = lax.rem(jnp.minimum(step, n_tiles - 1), tiles_per_seq)
    tm = x_ref.shape[1]
    groups_per_kv = (SWA_Q_HEADS // SWA_KV_HEADS) // 2

    @pl.when(step == 0)
    def _():
        r_ref[...] = jnp.zeros_like(r_ref)

    @pl.when(tile == 0)
    def _():
        mkv = jnp.dot(mem_ref[0].astype(bf16), w_mem_ref[...], preferred_element_type=f32)
        mk_ref[...] = mkv[:, :W_C].astype(bf16)
        mv_ref[...] = mkv[:, W_C:].astype(bf16)

    r = r_ref[...]
    mu = jnp.mean(r, axis=-1, keepdims=True)
    d = r - mu
    var = jnp.mean(d * d, axis=-1, keepdims=True)
    out_ref[0] = d * lax.rsqrt(var + LN_EPS) * gain_ref[...] + ln_bias_ref[...]

    for win_ref, prev_ref, cur_ref in ((k_win_ref, ka_prev_ref, ka_ref),
                                       (v_win_ref, va_prev_ref, va_ref)):
        win_ref[0:BLOCK] = prev_ref[0]
        win_ref[BLOCK:BLOCK + tm] = cur_ref[0]

    first_row = lax.broadcasted_iota(jnp.int32, (BLOCK, 1), 0) == 0
    first_col = lax.broadcasted_iota(jnp.int32, (1, BLOCK), 1) == 0

    def with_sink_slot(kv):
        return jnp.concatenate([jnp.where(first_row, jnp.zeros_like(kv[:BLOCK]), kv[:BLOCK]),
                                kv[BLOCK:]], axis=0)

    def sink_bias(bias, head):
        sink = sink_ref[head] * LOG2_E
        return jnp.concatenate([jnp.where(first_col, sink, bias[:, :BLOCK]), bias[:, BLOCK:]],
                               axis=1)

    for j in range(tm // BLOCK):
        chains, slots = [], []
        rows = slice(j * BLOCK, (j + 1) * BLOCK)
        bias = bias_ref[jnp.minimum(tile, 1)] if j == 0 else bias_ref[1]
        k = k_win_ref[j * BLOCK:(j + 2) * BLOCK]
        v = v_win_ref[j * BLOCK:(j + 2) * BLOCK]
        for g in range(W_A // PAIR):
            kv = (g // groups_per_kv) * PAIR
            chains.append((qa_ref[0, rows, g * PAIR:(g + 1) * PAIR],
                           with_sink_slot(k[:, kv:kv + PAIR]), with_sink_slot(v[:, kv:kv + PAIR]),
                           (sink_bias(bias, 2 * g), sink_bias(bias, 2 * g + 1))))
            slots.append((rows, g * PAIR))
        for g in range(W_C // PAIR):
            c = g * PAIR
            chains.append((qc_ref[0, rows, c:c + PAIR],
                           mk_ref[:, c:c + PAIR], mv_ref[:, c:c + PAIR], None))
            slots.append((rows, W_A + W_B + c))
        for (rows, c), (out, _) in zip(slots, _attend(chains, s_ref, p_ref)):
            y_ref[rows, c:c + PAIR] = out

    def rms_normalize(t):
        return t * lax.rsqrt(jnp.mean(t * t, axis=-1, keepdims=True) + RMS_EPS)

    y = jnp.concatenate([rms_normalize(y_ref[:, 0:W_A]),
                         rms_normalize(ob_ref[0].astype(f32)),
                         rms_normalize(y_ref[:, W_A + W_B:D_MIX])], axis=-1)
    y = (y * gate_ref[0].astype(f32)).astype(bf16)
    r_ref[...] = DEEPNORM_ALPHA * x_ref[0] + jnp.dot(y, w_out_ref[...],
                                                     preferred_element_type=f32)


def _tail_bias():
    qi = np.arange(BLOCK)[:, None]
    kj = np.arange(2 * BLOCK)[None, :]
    dist = qi + BLOCK - kj
    band = (dist >= 0) & (dist <= SWA_WINDOW - 1)
    return np.where(np.stack([band & (kj >= BLOCK), band]), 0.0, MASKED).astype(np.float32)


def _tail(sinks, qa, ka, va, qc, mem, w_mem, ob, gate, x, w_out, ln_gain, ln_bias):
    B, S, _ = x.shape
    M = mem.shape[1]
    tm = ROW_TILE
    blocks_per_tile = tm // BLOCK
    tiles_per_seq = S // tm
    n_tiles = B * tiles_per_seq
    bias = jnp.asarray(_tail_bias())

    def seq_tile(step):
        t = jnp.minimum(step, n_tiles - 1)
        return t // tiles_per_seq, t % tiles_per_seq

    def out_tile(step):
        t = jnp.maximum(step - 1, 0)
        return t // tiles_per_seq, t % tiles_per_seq

    row_block = lambda width: pl.BlockSpec((1, tm, width), lambda s: (*seq_tile(s), 0))
    prev_block = lambda width: pl.BlockSpec(
        (1, BLOCK, width),
        lambda s: (seq_tile(s)[0], jnp.maximum(seq_tile(s)[1] * blocks_per_tile - 1, 0), 0))
    const = lambda shape: pl.BlockSpec(shape, lambda s: (0,) * len(shape))
    kv_w = 2 * W_KV_A
    stage = ((W_A + W_C) // PAIR, 2 * BLOCK, 2 * BLOCK)
    pipelined = (_nbytes((tm, W_A + 2 * kv_w + W_C + W_B + D_MIX), bf16)
                 + 2 * _nbytes((BLOCK, kv_w), bf16) + _nbytes((M, D_MODEL), f32)
                 + _nbytes((D_MODEL, 2 * W_C), bf16) + 2 * _nbytes((tm, D_MODEL), f32)
                 + _nbytes((D_MIX, D_MODEL), bf16) + _nbytes(bias.shape, f32))
    scratch = [pltpu.VMEM((BLOCK + tm, kv_w), bf16), pltpu.VMEM((BLOCK + tm, kv_w), bf16),
               pltpu.VMEM((M, W_C), bf16), pltpu.VMEM((M, W_C), bf16),
               pltpu.VMEM((tm, D_MIX), f32), pltpu.VMEM(stage, f32), pltpu.VMEM(stage, bf16),
               pltpu.VMEM((tm, D_MODEL), f32)]
    scratch_bytes = (2 * _nbytes((BLOCK + tm, kv_w), bf16) + 2 * _nbytes((M, W_C), bf16)
                     + 5 * _nbytes((tm, D_MIX), f32) + _nbytes(stage, f32) + _nbytes(stage, bf16))
    return pl.pallas_call(
        functools.partial(_tail_kernel, tiles_per_seq=tiles_per_seq, n_tiles=n_tiles),
        grid=(n_tiles + 1,),
        in_specs=[pl.BlockSpec(memory_space=pltpu.SMEM),
                  row_block(W_A), row_block(kv_w), prev_block(kv_w),
                  row_block(kv_w), prev_block(kv_w), row_block(W_C),
                  pl.BlockSpec((1, M, D_MODEL), lambda s: (seq_tile(s)[0], 0, 0)),
                  const((D_MODEL, 2 * W_C)), row_block(W_B), row_block(D_MIX),
                  row_block(D_MODEL), const((D_MIX, D_MODEL)),
                  const((1, D_MODEL)), const((1, D_MODEL)), const(bias.shape)],
        out_specs=pl.BlockSpec((1, tm, D_MODEL), lambda s: (*out_tile(s), 0)),
        out_shape=jax.ShapeDtypeStruct((B, S, D_MODEL), f32),
        scratch_shapes=scratch,
        compiler_params=pltpu.CompilerParams(
            dimension_semantics=("arbitrary",),
            vmem_limit_bytes=_vmem_limit(pipelined, scratch_bytes)),
        name="mix_out",
    )(sinks, qa, ka, ka, va, va, qc, mem, w_mem, ob, gate, x, w_out, ln_gain, ln_bias, bias)


def _rope_tables(seq_len):
    pos = jnp.arange(seq_len, dtype=f32)
    inv = ROPE_THETA ** (-jnp.arange(0, HEAD_DIM, 2, dtype=f32) / HEAD_DIM)
    ang = pos[:, None] * inv[None, :]
    cos, sin = jnp.cos(ang), jnp.sin(ang)
    cos2 = jnp.concatenate([cos, cos, cos, cos], axis=-1)
    sin2 = jnp.concatenate([-sin, sin, -sin, sin], axis=-1)
    return cos2, sin2


def kernel(x, mem, w_in, b_in, w_mem, attn_sinks, g_branch, w_out, ln_gain, ln_bias):
    assert w_in.shape[0] == DEPTH
    S = x.shape[1]
    cos2, sin2 = _rope_tables(S)
    for l in range(DEPTH):
        qa, ka, va, qc, gate, *views_b = _in_proj(
            x, w_in[l].astype(bf16), b_in[l][None, :], cos2, sin2, g_branch[l][None, :])
        ob = _dilated(views_b)
        x = _tail(attn_sinks[l], qa, ka, va, qc, mem, w_mem[l].astype(bf16), ob, gate, x,
                  w_out[l].astype(bf16), ln_gain[l][None, :], ln_bias[l][None, :])
    return x
```

```python
import functools
import math

import numpy as np
import jax
import jax.numpy as jnp
from jax import lax
from jax.experimental import pallas as pl
from jax.experimental.pallas import tpu as pltpu

D_MODEL = 1024
HEAD_DIM = 64
SWA_Q_HEADS = 8
SWA_KV_HEADS = 2
SWA_WINDOW = 128
DIL_HEADS = 4
DIL_PAIRS = ((128, 1), (512, 4), (2048, 16))
MEM_HEADS = 4
BLOCK = 128
ROPE_THETA = 10000.0
LN_EPS = 1e-5
RMS_EPS = 1e-6
DEPTH = 1
DEEPNORM_ALPHA = (2 * DEPTH) ** 0.25

W_A = SWA_Q_HEADS * HEAD_DIM
W_KV_A = SWA_KV_HEADS * HEAD_DIM
W_B = DIL_HEADS * HEAD_DIM
W_C = MEM_HEADS * HEAD_DIM
D_MIX = W_A + W_B + W_C
D_IN = W_A + 2 * W_KV_A + 3 * W_B + W_C + D_MIX

V7X_LANES = 128
V7X_VMEM_BYTES = 64 * 2**20
VMEM_HEADROOM_BYTES = 8 * 2**20

PAIR = 2 * HEAD_DIM
assert PAIR == V7X_LANES
LOG2_E = math.log2(math.e)
Q_SCALE = HEAD_DIM ** -0.5 * LOG2_E
MASKED = -1e30

OFF_QA = 0
OFF_KA = OFF_QA + W_A
OFF_VA = OFF_KA + W_KV_A
OFF_QB = OFF_VA + W_KV_A
OFF_KB = OFF_QB + W_B
OFF_VB = OFF_KB + W_B
OFF_QC = OFF_VB + W_B
OFF_Z = OFF_QC + W_C
assert OFF_Z + D_MIX == D_IN

ROW_TILE = 512
DIL_BLOCKS_PER_STEP = 4
DIL16_STREAMS_PER_STEP = 8
ONES_ROWS = 16

f32 = jnp.float32
bf16 = jnp.bfloat16


def _vmem_limit(pipelined_bytes, scratch_bytes=0):
    budget = V7X_VMEM_BYTES - VMEM_HEADROOM_BYTES
    assert 2 * pipelined_bytes + scratch_bytes + VMEM_HEADROOM_BYTES <= budget
    return budget


def _nbytes(shape, dtype):
    return int(np.prod(shape)) * jnp.dtype(dtype).itemsize


def _lane_is_first_head():
    return lax.broadcasted_iota(jnp.int32, (1, V7X_LANES), 1) < HEAD_DIM


def _in_proj_kernel(x_ref, w_ref, b_ref, cos_ref, sin_ref, g_ref,
                    qa_ref, ka_ref, va_ref, qc_ref, gate_ref,
                    q1_ref, k1_ref, v1_ref, q4_ref, k4_ref, v4_ref, q16_ref, k16_ref, v16_ref,
                    *stage_refs):
    tm = x_ref.shape[1]
    xb = x_ref[0].astype(bf16)
    cos = cos_ref[...]
    sin = sin_ref[...]
    lane = lax.broadcasted_iota(jnp.int32, (1, V7X_LANES), 1)
    first_half = (lane % HEAD_DIM) < (HEAD_DIM // 2)
    first_head = lane < HEAD_DIM

    def proj(off, width):
        acc = jnp.dot(xb, w_ref[:, off:off + width], preferred_element_type=f32)
        return acc + b_ref[:, off:off + width]

    def rope(a):
        rot = jnp.where(first_half,
                        pltpu.roll(a, V7X_LANES - HEAD_DIM // 2, 1),
                        pltpu.roll(a, HEAD_DIM // 2, 1))
        return a * cos + rot * sin

    def groups(a):
        return [a[:, i:i + V7X_LANES] for i in range(0, a.shape[1], V7X_LANES)]

    def both_halves(a):
        swapped = pltpu.roll(a, HEAD_DIM, 1)
        return jnp.where(first_head, a, swapped), jnp.where(first_head, swapped, a)

    for c in range(0, D_MIX, 2 * V7X_LANES):
        z = proj(OFF_Z + c, 2 * V7X_LANES)
        gate = z * jax.nn.sigmoid(z) * g_ref[:, c:c + 2 * V7X_LANES]
        gate_ref[0, :, c:c + 2 * V7X_LANES] = gate.astype(bf16)
    for c in range(0, W_A, 2 * V7X_LANES):
        for i, a in enumerate(groups(proj(OFF_QA + c, 2 * V7X_LANES))):
            lo = c + i * V7X_LANES
            qa_ref[0, :, lo:lo + V7X_LANES] = (rope(a) * Q_SCALE).astype(bf16)
    k_a, v_a = groups(proj(OFF_KA, 2 * W_KV_A))
    kv0, kv1 = both_halves(rope(k_a))
    ka_ref[0, :, 0:V7X_LANES] = kv0.astype(bf16)
    ka_ref[0, :, V7X_LANES:2 * V7X_LANES] = kv1.astype(bf16)
    va_ref[0] = v_a.T.astype(bf16)
    mixer_b = ((OFF_QB, lambda a: rope(a) * Q_SCALE, (q1_ref, q4_ref, q16_ref)),
               (OFF_KB, rope, (k1_ref, k4_ref, k16_ref)),
               (OFF_VB, lambda a: a, (v1_ref, v4_ref, v16_ref)))
    dil4, dil16 = DIL_PAIRS[1][1], DIL_PAIRS[2][1]
    for t, (off, fn, (nat_ref, r4_ref, r16_ref)) in enumerate(mixer_b):
        for g, a in enumerate(groups(proj(off, W_B))):
            a = fn(a)
            nat_ref[0, :, g * PAIR:(g + 1) * PAIR] = a.astype(bf16)
            slab = t * (W_B // PAIR) + g
            stage_ref, stage4_ref = stage_refs[2 * slab], stage_refs[2 * slab + 1]
            stage_ref[...] = a
            lanes = slice(g * PAIR, (g + 1) * PAIR)
            rows4, rows16 = tm // dil4, tm // dil16
            for j4 in range(dil4):
                stream = stage_ref[pl.ds(j4, rows4, stride=dil4), :]
                r4_ref[0, j4, :, lanes] = stream.astype(bf16)
                stage4_ref[j4 * rows4:(j4 + 1) * rows4] = stream
            for j16 in range(dil16):
                j4, phase = j16 % dil4, j16 // dil4
                stream = stage4_ref[pl.ds(j4 * rows4 + phase, rows16, stride=dil16 // dil4), :]
                r16_ref[0, j16, :, lanes] = stream.astype(bf16)
    qc_ref[0] = (proj(OFF_QC, W_C) * Q_SCALE).astype(bf16)


def _in_proj(x, w_in, b_in, cos2, sin2, g_branch):
    B, S, _ = x.shape
    tm = ROW_TILE
    widths = (W_A, 2 * W_KV_A, W_C, D_MIX, W_B, W_B, W_B)
    row_block = lambda width: pl.BlockSpec((1, tm, width), lambda b, i: (b, i, 0))
    out_specs = [row_block(w) for w in widths]
    out_shape = [jax.ShapeDtypeStruct((B, S, w), bf16) for w in widths]
    out_specs.insert(2, pl.BlockSpec((1, W_KV_A, tm), lambda b, i: (b, 0, i)))
    out_shape.insert(2, jax.ShapeDtypeStruct((B, W_KV_A, S), bf16))
    for _, r in DIL_PAIRS[1:]:
        out_specs += [pl.BlockSpec((1, r, tm // r, W_B), lambda b, i: (b, 0, i, 0))] * 3
        out_shape += [jax.ShapeDtypeStruct((B, r, S // r, W_B), bf16)] * 3
    const = lambda shape: pl.BlockSpec(shape, lambda b, i: (0,) * len(shape))
    n_stage, stage = 2 * 3 * (W_B // PAIR), (tm, V7X_LANES)
    pipelined = (_nbytes((tm, D_MODEL), f32) + _nbytes((D_MODEL, D_IN), bf16)
                 + _nbytes((tm, sum(widths) + W_KV_A + 6 * W_B), bf16)
                 + 2 * _nbytes((tm, V7X_LANES), f32))
    return pl.pallas_call(
        _in_proj_kernel,
        grid=(B, S // tm),
        in_specs=[row_block(D_MODEL), const((D_MODEL, D_IN)), const((1, D_IN)),
                  pl.BlockSpec((tm, V7X_LANES), lambda b, i: (i, 0)),
                  pl.BlockSpec((tm, V7X_LANES), lambda b, i: (i, 0)),
                  const((1, D_MIX))],
        out_specs=out_specs,
        out_shape=out_shape,
        scratch_shapes=[pltpu.VMEM(stage, f32)] * n_stage,
        compiler_params=pltpu.CompilerParams(
            dimension_semantics=("arbitrary", "arbitrary"),
            vmem_limit_bytes=_vmem_limit(pipelined, n_stage * _nbytes(stage, f32))),
        name="in_proj",
    )(x, w_in, b_in, cos2, sin2, g_branch)


def _attend(chains, s_ref, p_ref):
    first_head = _lane_is_first_head()
    for c, (q2, k2, _, _) in enumerate(chains):
        zero = jnp.zeros_like(q2)
        stacked = jnp.concatenate([jnp.where(first_head, q2, zero),
                                   jnp.where(first_head, zero, q2)], axis=0)
        s_ref[c, :, 0:k2.shape[0]] = lax.dot_general(
            stacked, k2, (((1,), (1,)), ((), ())), preferred_element_type=f32)
    tops = []
    for c, (q2, k2, _, biases) in enumerate(chains):
        m_rows, n = q2.shape[0], k2.shape[0]
        pair_tops = []
        for h in range(2):
            rows = slice(h * m_rows, (h + 1) * m_rows)
            sh = s_ref[c, rows, 0:n]
            if biases is not None:
                sh = sh + biases[h]
            m = jnp.max(sh, axis=-1, keepdims=True)
            p_ref[c, rows, 0:n] = jnp.exp2(sh - m).astype(bf16)
            pair_tops.append(m)
        tops.append(pair_tops)
    results = []
    for c, (q2, k2, v2, _) in enumerate(chains):
        m_rows, n = q2.shape[0], k2.shape[0]
        v_ones = jnp.concatenate([v2, jnp.ones_like(v2)], axis=1)
        o = jnp.dot(p_ref[c, :, 0:n], v_ones, preferred_element_type=f32)
        acc = jnp.where(first_head, o[:m_rows, :PAIR], o[m_rows:, :PAIR])
        denom = jnp.where(first_head, o[:m_rows, PAIR:], o[m_rows:, PAIR:])
        lse = jnp.where(first_head, tops[c][0], tops[c][1]) + jnp.log(denom) * LOG2_E
        results.append((acc * (1.0 / denom), lse))
    return results


def _attend_t(groups, s_ref, p_ref):
    first_head = _lane_is_first_head()
    for c, (q2s, k2, _, _) in enumerate(groups):
        parts = []
        for q2 in q2s:
            zero = jnp.zeros_like(q2)
            parts += [jnp.where(first_head, q2, zero), jnp.where(first_head, zero, q2)]
        stacked = jnp.concatenate(parts, axis=0)
        s_ref[c, :, 0:stacked.shape[0]] = lax.dot_general(
            k2, stacked, (((1,), (1,)), ((), ())), preferred_element_type=f32)
    for c, (q2s, _, _, biases) in enumerate(groups):
        m_rows = q2s[0].shape[0]
        for h in range(2 * len(q2s)):
            cols = slice(h * m_rows, (h + 1) * m_rows)
            sh = s_ref[c, :, cols]
            if biases is not None:
                sh = sh + biases[h]
            top = jnp.max(sh, axis=0, keepdims=True)
            p_ref[c, :, cols] = jnp.exp2(sh - top).astype(bf16)
    results = []
    for c, (q2s, _, v_t, _) in enumerate(groups):
        width = 2 * len(q2s) * q2s[0].shape[0]
        results.append(jnp.dot(v_t, p_ref[c, :, 0:width], preferred_element_type=f32))
    return results


def _window_bias(max_dist):
    qi = np.arange(BLOCK)[:, None]
    kj = np.arange(2 * BLOCK)[None, :]
    dist = np.stack([qi - kj, qi + BLOCK - kj])
    return np.where((dist >= 0) & (dist <= max_dist), 0.0, MASKED).astype(np.float32)


def _window(blk):
    r0 = pl.multiple_of(blk * BLOCK, BLOCK)
    k0 = pl.multiple_of(jnp.maximum(blk - 1, 0) * BLOCK, BLOCK)
    return r0, k0, jnp.minimum(blk, 1)


def _merge(o_a, l_a, o_b, l_b):
    top = jnp.maximum(l_a, l_b)
    e_a = jnp.exp2(l_a - top)
    e_b = jnp.exp2(l_b - top)
    denom = e_a + e_b
    return (e_a * o_a + e_b * o_b) * (1.0 / denom), top + jnp.log(denom) * LOG2_E


def _dilated_kernel(q1_ref, k1_ref, v1_ref, q4_ref, k4_ref, v4_ref, q16_ref, k16_ref, v16_ref,
                    bias_ref, o_ref, o4_ref, l4_ref, o16_ref, l16_ref, s_ref, p_ref):
    seq_len = q1_ref.shape[1]
    n_pairs = W_B // PAIR
    dil4, dil16 = DIL_PAIRS[1][1], DIL_PAIRS[2][1]
    sub = dil16 // dil4

    causal = bias_ref[0, :, 0:BLOCK]

    def body16(i, carry):
        streams = [i * DIL16_STREAMS_PER_STEP + u for u in range(DIL16_STREAMS_PER_STEP)]
        chains = [(q16_ref[0, j, :, g * PAIR:(g + 1) * PAIR],
                   k16_ref[0, j, :, g * PAIR:(g + 1) * PAIR],
                   v16_ref[0, j, :, g * PAIR:(g + 1) * PAIR], (causal, causal))
                  for j in streams for g in range(n_pairs)]
        results = _attend(chains, s_ref, p_ref)
        for n, (out, lse) in enumerate(results):
            j, g = streams[n // n_pairs], n % n_pairs
            stream4, phase = lax.rem(j, dil4), lax.div(j, dil4)
            o16_ref[g, stream4, pl.ds(phase, BLOCK, stride=sub), :] = out
            l16_ref[g, stream4, pl.ds(phase, BLOCK, stride=sub), :] = lse
        return carry

    lax.fori_loop(0, dil16 // DIL16_STREAMS_PER_STEP, body16, 0)

    def body4(blk, carry):
        u0, k0, first = _window(blk)
        bias = bias_ref[first]
        chains = [(q4_ref[0, j, pl.ds(u0, BLOCK), g * PAIR:(g + 1) * PAIR],
                   k4_ref[0, j, pl.ds(k0, 2 * BLOCK), g * PAIR:(g + 1) * PAIR],
                   v4_ref[0, j, pl.ds(k0, 2 * BLOCK), g * PAIR:(g + 1) * PAIR], (bias, bias))
                  for j in range(dil4) for g in range(n_pairs)]
        results = _attend(chains, s_ref, p_ref)
        for n, (out, lse) in enumerate(results):
            j, g = n // n_pairs, n % n_pairs
            out, lse = _merge(out, lse, o16_ref[g, j, pl.ds(u0, BLOCK), :],
                              l16_ref[g, j, pl.ds(u0, BLOCK), :])
            o4_ref[g, pl.ds(u0 * dil4 + j, BLOCK, stride=dil4), :] = out
            l4_ref[g, pl.ds(u0 * dil4 + j, BLOCK, stride=dil4), :] = lse
        return carry

    lax.fori_loop(0, seq_len // dil4 // BLOCK, body4, 0)

    def body1(i, carry):
        windows = [_window(i * DIL_BLOCKS_PER_STEP + u) for u in range(DIL_BLOCKS_PER_STEP)]
        chains = [(q1_ref[0, pl.ds(r0, BLOCK), g * PAIR:(g + 1) * PAIR],
                   k1_ref[0, pl.ds(k0, 2 * BLOCK), g * PAIR:(g + 1) * PAIR],
                   v1_ref[0, pl.ds(k0, 2 * BLOCK), g * PAIR:(g + 1) * PAIR],
                   (bias_ref[first], bias_ref[first]))
                  for r0, k0, first in windows for g in range(n_pairs)]
        results = _attend(chains, s_ref, p_ref)
        for n, (out, lse) in enumerate(results):
            r0, g = windows[n // n_pairs][0], n % n_pairs
            out, _ = _merge(out, lse, o4_ref[g, pl.ds(r0, BLOCK), :],
                            l4_ref[g, pl.ds(r0, BLOCK), :])
            o_ref[0, pl.ds(r0, BLOCK), g * PAIR:(g + 1) * PAIR] = out.astype(bf16)
        return carry

    lax.fori_loop(0, seq_len // BLOCK // DIL_BLOCKS_PER_STEP, body1, 0)


def _dilated(views):
    B, S, _ = views[0].shape
    assert all(w // r == BLOCK for w, r in DIL_PAIRS) and S // DIL_PAIRS[2][1] == BLOCK
    assert DIL_PAIRS[2][1] % DIL16_STREAMS_PER_STEP == 0
    bias = jnp.asarray(_window_bias(BLOCK))
    view_specs = [pl.BlockSpec((1, S, W_B), lambda b: (b, 0, 0))] * 3
    for _, r in DIL_PAIRS[1:]:
        view_specs += [pl.BlockSpec((1, r, S // r, W_B), lambda b: (b, 0, 0, 0))] * 3
    slab = (W_B // PAIR, S, V7X_LANES)
    slab16 = (W_B // PAIR, DIL_PAIRS[1][1], S // DIL_PAIRS[1][1], V7X_LANES)
    chains = max(DIL_BLOCKS_PER_STEP, DIL16_STREAMS_PER_STEP, DIL_PAIRS[1][1]) * (W_B // PAIR)
    stage = (chains, 2 * BLOCK, 2 * BLOCK)
    pipelined = 10 * _nbytes((S, W_B), bf16) + _nbytes(bias.shape, f32)
    return pl.pallas_call(
        _dilated_kernel,
        grid=(B,),
        in_specs=view_specs + [pl.BlockSpec(bias.shape, lambda b: (0, 0, 0))],
        out_specs=pl.BlockSpec((1, S, W_B), lambda b: (b, 0, 0)),
        out_shape=jax.ShapeDtypeStruct((B, S, W_B), bf16),
        scratch_shapes=[pltpu.VMEM(slab, f32), pltpu.VMEM(slab, f32),
                        pltpu.VMEM(slab16, f32), pltpu.VMEM(slab16, f32),
                        pltpu.VMEM(stage, f32), pltpu.VMEM(stage, bf16)],
        compiler_params=pltpu.CompilerParams(
            dimension_semantics=("arbitrary",),
            vmem_limit_bytes=_vmem_limit(
                pipelined, 4 * _nbytes(slab, f32) + _nbytes(stage, f32) + _nbytes(stage, bf16))),
        name="dilated_attention",
    )(*views, bias)


def _tail_kernel(sink_ref, qa_ref, ka_ref, ka_prev_ref, va_ref, va_prev_ref, qc_ref, mem_ref,
                 w_mk_ref, w_mv_ref, ob_ref, gate_ref, x_ref, w_out_ref, gain_ref, ln_bias_ref,
                 bias_ref, out_ref, k_win_ref, v_win_ref, mk_ref, mv_ref, y_ref, s_ref, p_ref,
                 r_ref, *, tiles_per_seq, n_tiles):
    step = pl.program_id(0)
    tile = lax.rem(jnp.minimum(step, n_tiles - 1), tiles_per_seq)
    tm = x_ref.shape[1]
    heads_per_kv = SWA_Q_HEADS // SWA_KV_HEADS
    ones_rows = jnp.ones((ONES_ROWS, 2 * BLOCK), bf16)

    @pl.when(step == 0)
    def _():
        r_ref[...] = jnp.zeros_like(r_ref)

    @pl.when(tile == 0)
    def _():
        mem = mem_ref[0].astype(bf16)
        mk_ref[...] = jnp.dot(mem, w_mk_ref[...], preferred_element_type=f32).astype(bf16)
        mv_ref[...] = lax.dot_general(w_mv_ref[...], mem, (((1,), (1,)), ((), ())),
                                      preferred_element_type=f32).astype(bf16)

    r = r_ref[...]
    mu = jnp.mean(r, axis=-1, keepdims=True)
    d = r - mu
    var = jnp.mean(d * d, axis=-1, keepdims=True)
    out_ref[0] = d * lax.rsqrt(var + LN_EPS) * gain_ref[...] + ln_bias_ref[...]

    k_win_ref[0:BLOCK] = ka_prev_ref[0]
    k_win_ref[BLOCK:BLOCK + tm] = ka_ref[0]
    v_win_ref[:, 0:BLOCK] = va_prev_ref[0]
    v_win_ref[:, BLOCK:BLOCK + tm] = va_ref[0]

    first_row = lax.broadcasted_iota(jnp.int32, (BLOCK, 1), 0) == 0
    first_lane = lax.broadcasted_iota(jnp.int32, (1, BLOCK), 1) == 0
    top_rows = lax.broadcasted_iota(jnp.int32, (8, 1), 0) == 0

    def sink_bias(bias, head):
        sink = sink_ref[head] * LOG2_E
        return jnp.concatenate([jnp.where(top_rows, sink, bias[:8]), bias[8:]], axis=0)

    groups = []
    groups_per_block = SWA_KV_HEADS + W_C // PAIR
    for j in range(tm // BLOCK):
        rows = slice(j * BLOCK, (j + 1) * BLOCK)
        bias = bias_ref[jnp.minimum(tile, 1)] if j == 0 else bias_ref[1]
        k = k_win_ref[j * BLOCK:(j + 2) * BLOCK]
        k = jnp.concatenate([jnp.where(first_row, jnp.zeros_like(k[:BLOCK]), k[:BLOCK]),
                             k[BLOCK:]], axis=0)
        v_t = v_win_ref[:, j * BLOCK:(j + 2) * BLOCK]
        v_t = jnp.concatenate([jnp.where(first_lane, jnp.zeros_like(v_t[:, :BLOCK]),
                                         v_t[:, :BLOCK]), v_t[:, BLOCK:]], axis=1)
        for kv in range(SWA_KV_HEADS):
            heads = range(kv * heads_per_kv, (kv + 1) * heads_per_kv)
            q2s = [qa_ref[0, rows, h * HEAD_DIM:h * HEAD_DIM + PAIR] for h in heads[::2]]
            groups.append((q2s, k[:, kv * PAIR:(kv + 1) * PAIR],
                           jnp.concatenate([v_t[kv * HEAD_DIM:(kv + 1) * HEAD_DIM], ones_rows],
                                           axis=0),
                           [sink_bias(bias, h) for h in heads]))
        for g in range(W_C // PAIR):
            c = g * PAIR
            groups.append(([qc_ref[0, rows, c:c + PAIR]], mk_ref[:, c:c + PAIR],
                           jnp.concatenate([mv_ref[c:c + PAIR], ones_rows], axis=0), None))
    all_outs = _attend_t(groups, s_ref, p_ref)
    for j in range(tm // BLOCK):
        rows = slice(j * BLOCK, (j + 1) * BLOCK)
        outs = all_outs[j * groups_per_block:(j + 1) * groups_per_block]
        for kv in range(SWA_KV_HEADS):
            o_t = outs[kv]
            normed = o_t[:HEAD_DIM] * (1.0 / o_t[HEAD_DIM:HEAD_DIM + 1])
            for g in range(heads_per_kv // 2):
                pair_t = jnp.concatenate([normed[:, (2 * g) * BLOCK:(2 * g + 1) * BLOCK],
                                          normed[:, (2 * g + 1) * BLOCK:(2 * g + 2) * BLOCK]],
                                         axis=0)
                lane0 = (kv * heads_per_kv // 2 + g) * PAIR
                y_ref[rows, lane0:lane0 + PAIR] = pair_t.T
        for g in range(W_C // PAIR):
            o_t = outs[SWA_KV_HEADS + g]
            inv = 1.0 / o_t[PAIR:PAIR + 1]
            pair_t = jnp.concatenate([o_t[:HEAD_DIM, :BLOCK] * inv[:, :BLOCK],
                                      o_t[HEAD_DIM:PAIR, BLOCK:] * inv[:, BLOCK:]], axis=0)
            lane0 = W_A + W_B + g * PAIR
            y_ref[rows, lane0:lane0 + PAIR] = pair_t.T

    def rms_normalize(t):
        return t * lax.rsqrt(jnp.mean(t * t, axis=-1, keepdims=True) + RMS_EPS)

    y = jnp.concatenate([rms_normalize(y_ref[:, 0:W_A]),
                         rms_normalize(ob_ref[0].astype(f32)),
                         rms_normalize(y_ref[:, W_A + W_B:D_MIX])], axis=-1)
    y = (y * gate_ref[0].astype(f32)).astype(bf16)
    r_ref[...] = DEEPNORM_ALPHA * x_ref[0] + jnp.dot(y, w_out_ref[...],
                                                     preferred_element_type=f32)


def _tail_bias():
    kj = np.arange(2 * BLOCK)[:, None]
    qi = np.arange(BLOCK)[None, :]
    dist = qi + BLOCK - kj
    band = (dist >= 0) & (dist <= SWA_WINDOW - 1)
    return np.where(np.stack([band & (kj >= BLOCK), band]), 0.0, MASKED).astype(np.float32)


def _tail(sinks, qa, ka, va_t, qc, mem, w_mk, w_mv_t, ob, gate, x, w_out, ln_gain, ln_bias):
    B, S, _ = x.shape
    M = mem.shape[1]
    tm = ROW_TILE
    blocks_per_tile = tm // BLOCK
    tiles_per_seq = S // tm
    n_tiles = B * tiles_per_seq
    bias = jnp.asarray(_tail_bias())

    def seq_tile(step):
        t = jnp.minimum(step, n_tiles - 1)
        return t // tiles_per_seq, t % tiles_per_seq

    def out_tile(step):
        t = jnp.maximum(step - 1, 0)
        return t // tiles_per_seq, t % tiles_per_seq

    row_block = lambda width: pl.BlockSpec((1, tm, width), lambda s: (*seq_tile(s), 0))
    prev_index = lambda s: jnp.maximum(seq_tile(s)[1] * blocks_per_tile - 1, 0)
    prev_block = lambda width: pl.BlockSpec(
        (1, BLOCK, width), lambda s: (seq_tile(s)[0], prev_index(s), 0))
    const = lambda shape: pl.BlockSpec(shape, lambda s: (0,) * len(shape))
    kv_w = 2 * W_KV_A
    heads_per_kv = SWA_Q_HEADS // SWA_KV_HEADS
    stage = (blocks_per_tile * (SWA_KV_HEADS + W_C // PAIR), 2 * BLOCK, heads_per_kv * BLOCK)
    pipelined = (_nbytes((tm, W_A + kv_w + W_KV_A + W_C + W_B + D_MIX), bf16)
                 + _nbytes((BLOCK, kv_w + W_KV_A), bf16) + _nbytes((M, D_MODEL), f32)
                 + _nbytes((D_MODEL, 2 * W_C), bf16) + 2 * _nbytes((tm, D_MODEL), f32)
                 + _nbytes((D_MIX, D_MODEL), bf16) + _nbytes(bias.shape, f32))
    scratch = [pltpu.VMEM((BLOCK + tm, kv_w), bf16), pltpu.VMEM((W_KV_A, BLOCK + tm), bf16),
               pltpu.VMEM((M, W_C), bf16), pltpu.VMEM((W_C, M), bf16),
               pltpu.VMEM((tm, D_MIX), f32), pltpu.VMEM(stage, f32), pltpu.VMEM(stage, bf16),
               pltpu.VMEM((tm, D_MODEL), f32)]
    scratch_bytes = (2 * _nbytes((BLOCK + tm, kv_w), bf16) + 2 * _nbytes((M, W_C), bf16)
                     + 5 * _nbytes((tm, D_MIX), f32) + _nbytes(stage, f32) + _nbytes(stage, bf16))
    return pl.pallas_call(
        functools.partial(_tail_kernel, tiles_per_seq=tiles_per_seq, n_tiles=n_tiles),
        grid=(n_tiles + 1,),
        in_specs=[pl.BlockSpec(memory_space=pltpu.SMEM),
                  row_block(W_A), row_block(kv_w), prev_block(kv_w),
                  pl.BlockSpec((1, W_KV_A, tm), lambda s: (seq_tile(s)[0], 0, seq_tile(s)[1])),
                  pl.BlockSpec((1, W_KV_A, BLOCK), lambda s: (seq_tile(s)[0], 0, prev_index(s))),
                  row_block(W_C),
                  pl.BlockSpec((1, M, D_MODEL), lambda s: (seq_tile(s)[0], 0, 0)),
                  const((D_MODEL, W_C)), const((W_C, D_MODEL)), row_block(W_B), row_block(D_MIX),
                  row_block(D_MODEL), const((D_MIX, D_MODEL)),
                  const((1, D_MODEL)), const((1, D_MODEL)), const(bias.shape)],
        out_specs=pl.BlockSpec((1, tm, D_MODEL), lambda s: (*out_tile(s), 0)),
        out_shape=jax.ShapeDtypeStruct((B, S, D_MODEL), f32),
        scratch_shapes=scratch,
        compiler_params=pltpu.CompilerParams(
            dimension_semantics=("arbitrary",),
            vmem_limit_bytes=_vmem_limit(pipelined, scratch_bytes)),
        name="mix_out",
    )(sinks, qa, ka, ka, va_t, va_t, qc, mem, w_mk, w_mv_t, ob, gate, x, w_out, ln_gain, ln_bias,
      bias)


def _rope_tables(seq_len):
    pos = jnp.arange(seq_len, dtype=f32)
    inv = ROPE_THETA ** (-jnp.arange(0, HEAD_DIM, 2, dtype=f32) / HEAD_DIM)
    ang = pos[:, None] * inv[None, :]
    cos, sin = jnp.cos(ang), jnp.sin(ang)
    cos2 = jnp.concatenate([cos, cos, cos, cos], axis=-1)
    sin2 = jnp.concatenate([-sin, sin, -sin, sin], axis=-1)
    return cos2, sin2


def kernel(x, mem, w_in, b_in, w_mem, attn_sinks, g_branch, w_out, ln_gain, ln_bias):
    assert w_in.shape[0] == DEPTH
    S = x.shape[1]
    cos2, sin2 = _rope_tables(S)
    for l in range(DEPTH):
        qa, ka, va, qc, gate, *views_b = _in_proj(
            x, w_in[l].astype(bf16), b_in[l][None, :], cos2, sin2, g_branch[l][None, :])
        ob = _dilated(views_b)
        w_mk, w_mv = w_mem[l, :, :W_C].astype(bf16), w_mem[l, :, W_C:].astype(bf16)
        x = _tail(attn_sinks[l], qa, ka, va, qc, mem, w_mk, w_mv.T, ob, gate, x,
                  w_out[l].astype(bf16), ln_gain[l][None, :], ln_bias[l][None, :])
    return x
```

```python
import functools
import math

import numpy as np
import jax
import jax.numpy as jnp
from jax import lax
from jax.experimental import pallas as pl
from jax.experimental.pallas import tpu as pltpu

D_MODEL = 1024
HEAD_DIM = 64
SWA_Q_HEADS = 8
SWA_KV_HEADS = 2
SWA_WINDOW = 128
DIL_HEADS = 4
DIL_PAIRS = ((128, 1), (512, 4), (2048, 16))
MEM_HEADS = 4
BLOCK = 128
ROPE_THETA = 10000.0
LN_EPS = 1e-5
RMS_EPS = 1e-6
DEPTH = 1
DEEPNORM_ALPHA = (2 * DEPTH) ** 0.25

W_A = SWA_Q_HEADS * HEAD_DIM
W_KV_A = SWA_KV_HEADS * HEAD_DIM
W_B = DIL_HEADS * HEAD_DIM
W_C = MEM_HEADS * HEAD_DIM
D_MIX = W_A + W_B + W_C
D_IN = W_A + 2 * W_KV_A + 3 * W_B + W_C + D_MIX

V7X_LANES = 128
V7X_VMEM_BYTES = 64 * 2**20
VMEM_HEADROOM_BYTES = 8 * 2**20

PAIR = 2 * HEAD_DIM
assert PAIR == V7X_LANES
LOG2_E = math.log2(math.e)
Q_SCALE = HEAD_DIM ** -0.5 * LOG2_E
MASKED = -1e30

OFF_QA = 0
OFF_KA = OFF_QA + W_A
OFF_VA = OFF_KA + W_KV_A
OFF_QB = OFF_VA + W_KV_A
OFF_KB = OFF_QB + W_B
OFF_VB = OFF_KB + W_B
OFF_QC = OFF_VB + W_B
OFF_Z = OFF_QC + W_C
assert OFF_Z + D_MIX == D_IN

IN_ROW_TILE = 1024
ROW_TILE = 512
DIL_BLOCKS_PER_STEP = 8
DIL4_BLOCKS_PER_STEP = 2
DIL16_STREAMS_PER_STEP = 16

f32 = jnp.float32
bf16 = jnp.bfloat16


def _vmem_limit(pipelined_bytes, scratch_bytes=0):
    budget = V7X_VMEM_BYTES - VMEM_HEADROOM_BYTES
    assert 2 * pipelined_bytes + scratch_bytes + VMEM_HEADROOM_BYTES <= budget
    return budget


def _nbytes(shape, dtype):
    return int(np.prod(shape)) * jnp.dtype(dtype).itemsize


def _lane_is_first_head():
    return lax.broadcasted_iota(jnp.int32, (1, V7X_LANES), 1) < HEAD_DIM


def _in_proj_kernel(x_ref, w_ref, b_ref, cos_ref, sin_ref, g_ref,
                    qa_ref, ka_ref, va_ref, qc_ref, gate_ref,
                    q1_ref, k1_ref, v1_ref, q4_ref, k4_ref, v4_ref, q16_ref, k16_ref, v16_ref,
                    *stage_refs):
    tm = x_ref.shape[1]
    xb = x_ref[0].astype(bf16)
    cos = cos_ref[...]
    sin = sin_ref[...]
    lane = lax.broadcasted_iota(jnp.int32, (1, V7X_LANES), 1)
    first_half = (lane % HEAD_DIM) < (HEAD_DIM // 2)
    first_head = lane < HEAD_DIM

    def proj(off, width):
        acc = jnp.dot(xb, w_ref[:, off:off + width], preferred_element_type=f32)
        return acc + b_ref[:, off:off + width]

    def rope(a):
        rot = jnp.where(first_half,
                        pltpu.roll(a, V7X_LANES - HEAD_DIM // 2, 1),
                        pltpu.roll(a, HEAD_DIM // 2, 1))
        return a * cos + rot * sin

    def groups(a):
        return [a[:, i:i + V7X_LANES] for i in range(0, a.shape[1], V7X_LANES)]

    def both_halves(a):
        swapped = pltpu.roll(a, HEAD_DIM, 1)
        return jnp.where(first_head, a, swapped), jnp.where(first_head, swapped, a)

    for c in range(0, D_MIX, 2 * V7X_LANES):
        z = proj(OFF_Z + c, 2 * V7X_LANES)
        gate = z * jax.nn.sigmoid(z) * g_ref[:, c:c + 2 * V7X_LANES]
        gate_ref[0, :, c:c + 2 * V7X_LANES] = gate.astype(bf16)
    for c in range(0, W_A, 2 * V7X_LANES):
        for i, a in enumerate(groups(proj(OFF_QA + c, 2 * V7X_LANES))):
            lo = c + i * V7X_LANES
            qa_ref[0, :, lo:lo + V7X_LANES] = (rope(a) * Q_SCALE).astype(bf16)
    k_a, v_a = groups(proj(OFF_KA, 2 * W_KV_A))
    for ref, a in ((ka_ref, rope(k_a)), (va_ref, v_a)):
        kv0, kv1 = both_halves(a)
        ref[0, :, 0:V7X_LANES] = kv0.astype(bf16)
        ref[0, :, V7X_LANES:2 * V7X_LANES] = kv1.astype(bf16)
    mixer_b = ((OFF_QB, lambda a: rope(a) * Q_SCALE, (q1_ref, q4_ref, q16_ref)),
               (OFF_KB, rope, (k1_ref, k4_ref, k16_ref)),
               (OFF_VB, lambda a: a, (v1_ref, v4_ref, v16_ref)))
    dil4, dil16 = DIL_PAIRS[1][1], DIL_PAIRS[2][1]
    for t, (off, fn, (nat_ref, r4_ref, r16_ref)) in enumerate(mixer_b):
        for g, a in enumerate(groups(proj(off, W_B))):
            a = fn(a)
            nat_ref[0, :, g * PAIR:(g + 1) * PAIR] = a.astype(bf16)
            slab = t * (W_B // PAIR) + g
            stage_ref, stage4_ref = stage_refs[2 * slab], stage_refs[2 * slab + 1]
            stage_ref[...] = a
            lanes = slice(g * PAIR, (g + 1) * PAIR)
            rows4, rows16 = tm // dil4, tm // dil16
            for j4 in range(dil4):
                stream = stage_ref[pl.ds(j4, rows4, stride=dil4), :]
                r4_ref[0, j4, :, lanes] = stream.astype(bf16)
                stage4_ref[j4 * rows4:(j4 + 1) * rows4] = stream
            for j16 in range(dil16):
                j4, phase = j16 % dil4, j16 // dil4
                stream = stage4_ref[pl.ds(j4 * rows4 + phase, rows16, stride=dil16 // dil4), :]
                r16_ref[0, j16, :, lanes] = stream.astype(bf16)
    qc_ref[0] = (proj(OFF_QC, W_C) * Q_SCALE).astype(bf16)


def _in_proj(x, w_in, b_in, cos2, sin2, g_branch):
    B, S, _ = x.shape
    tm = IN_ROW_TILE
    widths = (W_A, 2 * W_KV_A, 2 * W_KV_A, W_C, D_MIX, W_B, W_B, W_B)
    row_block = lambda width: pl.BlockSpec((1, tm, width), lambda b, i: (b, i, 0))
    out_specs = [row_block(w) for w in widths]
    out_shape = [jax.ShapeDtypeStruct((B, S, w), bf16) for w in widths]
    for _, r in DIL_PAIRS[1:]:
        out_specs += [pl.BlockSpec((1, r, tm // r, W_B), lambda b, i: (b, 0, i, 0))] * 3
        out_shape += [jax.ShapeDtypeStruct((B, r, S // r, W_B), bf16)] * 3
    const = lambda shape: pl.BlockSpec(shape, lambda b, i: (0,) * len(shape))
    n_stage, stage = 2 * 3 * (W_B // PAIR), (tm, V7X_LANES)
    pipelined = (_nbytes((tm, D_MODEL), f32) + _nbytes((D_MODEL, D_IN), bf16)
                 + _nbytes((tm, sum(widths) + 6 * W_B), bf16) + 2 * _nbytes((tm, V7X_LANES), f32))
    return pl.pallas_call(
        _in_proj_kernel,
        grid=(B, S // tm),
        in_specs=[row_block(D_MODEL), const((D_MODEL, D_IN)), const((1, D_IN)),
                  pl.BlockSpec((tm, V7X_LANES), lambda b, i: (i, 0)),
                  pl.BlockSpec((tm, V7X_LANES), lambda b, i: (i, 0)),
                  const((1, D_MIX))],
        out_specs=out_specs,
        out_shape=out_shape,
        scratch_shapes=[pltpu.VMEM(stage, f32)] * n_stage,
        compiler_params=pltpu.CompilerParams(
            dimension_semantics=("arbitrary", "arbitrary"),
            vmem_limit_bytes=_vmem_limit(pipelined, n_stage * _nbytes(stage, f32))),
        name="in_proj",
    )(x, w_in, b_in, cos2, sin2, g_branch)


def _attend(chains, s_ref, p_ref):
    first_head = _lane_is_first_head()
    for c, (q2, k2, _, _) in enumerate(chains):
        zero = jnp.zeros_like(q2)
        stacked = jnp.concatenate([jnp.where(first_head, q2, zero),
                                   jnp.where(first_head, zero, q2)], axis=0)
        s_ref[c, :, 0:k2.shape[0]] = lax.dot_general(
            stacked, k2, (((1,), (1,)), ((), ())), preferred_element_type=f32)
    tops = []
    for c, (q2, k2, _, biases) in enumerate(chains):
        m_rows, n = q2.shape[0], k2.shape[0]
        pair_tops = []
        for h in range(2):
            rows = slice(h * m_rows, (h + 1) * m_rows)
            sh = s_ref[c, rows, 0:n]
            if biases is not None:
                sh = sh + biases[h]
            m = jnp.max(sh, axis=-1, keepdims=True)
            p_ref[c, rows, 0:n] = jnp.exp2(sh - m).astype(bf16)
            pair_tops.append(m)
        tops.append(pair_tops)
    results = []
    for c, (q2, k2, v2, _) in enumerate(chains):
        m_rows, n = q2.shape[0], k2.shape[0]
        v_ones = jnp.concatenate([v2, jnp.ones_like(v2)], axis=1)
        o = jnp.dot(p_ref[c, :, 0:n], v_ones, preferred_element_type=f32)
        acc = jnp.where(first_head, o[:m_rows, :PAIR], o[m_rows:, :PAIR])
        denom = jnp.where(first_head, o[:m_rows, PAIR:], o[m_rows:, PAIR:])
        lse = jnp.where(first_head, tops[c][0], tops[c][1]) + jnp.log(denom) * LOG2_E
        results.append((acc * (1.0 / denom), lse))
    return results


def _window_bias(max_dist):
    qi = np.arange(BLOCK)[:, None]
    kj = np.arange(2 * BLOCK)[None, :]
    dist = np.stack([qi - kj, qi + BLOCK - kj])
    return np.where((dist >= 0) & (dist <= max_dist), 0.0, MASKED).astype(np.float32)


def _window(blk):
    r0 = pl.multiple_of(blk * BLOCK, BLOCK)
    k0 = pl.multiple_of(jnp.maximum(blk - 1, 0) * BLOCK, BLOCK)
    return r0, k0, jnp.minimum(blk, 1)


def _merge(o_a, l_a, o_b, l_b):
    top = jnp.maximum(l_a, l_b)
    e_a = jnp.exp2(l_a - top)
    e_b = jnp.exp2(l_b - top)
    denom = e_a + e_b
    return (e_a * o_a + e_b * o_b) * (1.0 / denom), top + jnp.log(denom) * LOG2_E


def _dilated_kernel(q1_ref, k1_ref, v1_ref, q4_ref, k4_ref, v4_ref, q16_ref, k16_ref, v16_ref,
                    bias_ref, o_ref, o4_ref, l4_ref, o16_ref, l16_ref, s_ref, p_ref):
    seq_len = q1_ref.shape[1]
    n_pairs = W_B // PAIR
    dil4, dil16 = DIL_PAIRS[1][1], DIL_PAIRS[2][1]
    sub = dil16 // dil4

    causal = bias_ref[0, :, 0:BLOCK]

    def body16(i, carry):
        streams = [i * DIL16_STREAMS_PER_STEP + u for u in range(DIL16_STREAMS_PER_STEP)]
        chains = [(q16_ref[0, j, :, g * PAIR:(g + 1) * PAIR],
                   k16_ref[0, j, :, g * PAIR:(g + 1) * PAIR],
                   v16_ref[0, j, :, g * PAIR:(g + 1) * PAIR], (causal, causal))
                  for j in streams for g in range(n_pairs)]
        results = _attend(chains, s_ref, p_ref)
        for n, (out, lse) in enumerate(results):
            j, g = streams[n // n_pairs], n % n_pairs
            stream4, phase = lax.rem(j, dil4), lax.div(j, dil4)
            o16_ref[g, stream4, pl.ds(phase, BLOCK, stride=sub), :] = out
            l16_ref[g, stream4, pl.ds(phase, BLOCK, stride=sub), :] = lse
        return carry

    lax.fori_loop(0, dil16 // DIL16_STREAMS_PER_STEP, body16, 0)

    def body4(i, carry):
        windows = [_window(i * DIL4_BLOCKS_PER_STEP + u) for u in range(DIL4_BLOCKS_PER_STEP)]
        chains = [(q4_ref[0, j, pl.ds(u0, BLOCK), g * PAIR:(g + 1) * PAIR],
                   k4_ref[0, j, pl.ds(k0, 2 * BLOCK), g * PAIR:(g + 1) * PAIR],
                   v4_ref[0, j, pl.ds(k0, 2 * BLOCK), g * PAIR:(g + 1) * PAIR],
                   (bias_ref[first], bias_ref[first]))
                  for u0, k0, first in windows for j in range(dil4) for g in range(n_pairs)]
        results = _attend(chains, s_ref, p_ref)
        for n, (out, lse) in enumerate(results):
            u0 = windows[n // (dil4 * n_pairs)][0]
            j, g = (n // n_pairs) % dil4, n % n_pairs
            out, lse = _merge(out, lse, o16_ref[g, j, pl.ds(u0, BLOCK), :],
                              l16_ref[g, j, pl.ds(u0, BLOCK), :])
            o4_ref[g, pl.ds(u0 * dil4 + j, BLOCK, stride=dil4), :] = out
            l4_ref[g, pl.ds(u0 * dil4 + j, BLOCK, stride=dil4), :] = lse
        return carry

    lax.fori_loop(0, seq_len // dil4 // BLOCK // DIL4_BLOCKS_PER_STEP, body4, 0)

    def body1(i, carry):
        windows = [_window(i * DIL_BLOCKS_PER_STEP + u) for u in range(DIL_BLOCKS_PER_STEP)]
        chains = [(q1_ref[0, pl.ds(r0, BLOCK), g * PAIR:(g + 1) * PAIR],
                   k1_ref[0, pl.ds(k0, 2 * BLOCK), g * PAIR:(g + 1) * PAIR],
                   v1_ref[0, pl.ds(k0, 2 * BLOCK), g * PAIR:(g + 1) * PAIR],
                   (bias_ref[first], bias_ref[first]))
                  for r0, k0, first in windows for g in range(n_pairs)]
        results = _attend(chains, s_ref, p_ref)
        for n, (out, lse) in enumerate(results):
            r0, g = windows[n // n_pairs][0], n % n_pairs
            out, _ = _merge(out, lse, o4_ref[g, pl.ds(r0, BLOCK), :],
                            l4_ref[g, pl.ds(r0, BLOCK), :])
            o_ref[0, pl.ds(r0, BLOCK), g * PAIR:(g + 1) * PAIR] = out.astype(bf16)
        return carry

    lax.fori_loop(0, seq_len // BLOCK // DIL_BLOCKS_PER_STEP, body1, 0)


def _dilated(views):
    B, S, _ = views[0].shape
    assert all(w // r == BLOCK for w, r in DIL_PAIRS) and S // DIL_PAIRS[2][1] == BLOCK
    assert DIL_PAIRS[2][1] % DIL16_STREAMS_PER_STEP == 0
    bias = jnp.asarray(_window_bias(BLOCK))
    view_specs = [pl.BlockSpec((1, S, W_B), lambda b: (b, 0, 0))] * 3
    for _, r in DIL_PAIRS[1:]:
        view_specs += [pl.BlockSpec((1, r, S // r, W_B), lambda b: (b, 0, 0, 0))] * 3
    slab = (W_B // PAIR, S, V7X_LANES)
    slab16 = (W_B // PAIR, DIL_PAIRS[1][1], S // DIL_PAIRS[1][1], V7X_LANES)
    chains = max(DIL_BLOCKS_PER_STEP, DIL16_STREAMS_PER_STEP,
                 DIL4_BLOCKS_PER_STEP * DIL_PAIRS[1][1]) * (W_B // PAIR)
    stage = (chains, 2 * BLOCK, 2 * BLOCK)
    pipelined = 10 * _nbytes((S, W_B), bf16) + _nbytes(bias.shape, f32)
    return pl.pallas_call(
        _dilated_kernel,
        grid=(B,),
        in_specs=view_specs + [pl.BlockSpec(bias.shape, lambda b: (0, 0, 0))],
        out_specs=pl.BlockSpec((1, S, W_B), lambda b: (b, 0, 0)),
        out_shape=jax.ShapeDtypeStruct((B, S, W_B), bf16),
        scratch_shapes=[pltpu.VMEM(slab, f32), pltpu.VMEM(slab, f32),
                        pltpu.VMEM(slab16, f32), pltpu.VMEM(slab16, f32),
                        pltpu.VMEM(stage, f32), pltpu.VMEM(stage, bf16)],
        compiler_params=pltpu.CompilerParams(
            dimension_semantics=("arbitrary",),
            vmem_limit_bytes=_vmem_limit(
                pipelined, 4 * _nbytes(slab, f32) + _nbytes(stage, f32) + _nbytes(stage, bf16))),
        name="dilated_attention",
    )(*views, bias)


def _tail_kernel(sink_ref, qa_ref, ka_ref, ka_prev_ref, va_ref, va_prev_ref, qc_ref, mem_ref,
                 w_mem_ref, ob_ref, gate_ref, x_ref, w_out_ref, gain_ref, ln_bias_ref, bias_ref,
                 out_ref, k_win_ref, v_win_ref, mk_ref, mv_ref, y_ref, s_ref, p_ref, r_ref,
                 *, tiles_per_seq, n_tiles):
    step = pl.program_id(0)
    tile = lax.rem(jnp.minimum(step, n_tiles - 1), tiles_per_seq)
    tm = x_ref.shape[1]
    groups_per_kv = (SWA_Q_HEADS // SWA_KV_HEADS) // 2

    @pl.when(step == 0)
    def _():
        r_ref[...] = jnp.zeros_like(r_ref)

    @pl.when(tile == 0)
    def _():
        mkv = jnp.dot(mem_ref[0].astype(bf16), w_mem_ref[...], preferred_element_type=f32)
        mk_ref[...] = mkv[:, :W_C].astype(bf16)
        mv_ref[...] = mkv[:, W_C:].astype(bf16)

    r = r_ref[...]
    mu = jnp.mean(r, axis=-1, keepdims=True)
    d = r - mu
    var = jnp.mean(d * d, axis=-1, keepdims=True)
    out_ref[0] = d * lax.rsqrt(var + LN_EPS) * gain_ref[...] + ln_bias_ref[...]

    for win_ref, prev_ref, cur_ref in ((k_win_ref, ka_prev_ref, ka_ref),
                                       (v_win_ref, va_prev_ref, va_ref)):
        win_ref[0:BLOCK] = prev_ref[0]
        win_ref[BLOCK:BLOCK + tm] = cur_ref[0]

    first_row = lax.broadcasted_iota(jnp.int32, (BLOCK, 1), 0) == 0
    first_col = lax.broadcasted_iota(jnp.int32, (1, BLOCK), 1) == 0

    def with_sink_slot(kv):
        return jnp.concatenate([jnp.where(first_row, jnp.zeros_like(kv[:BLOCK]), kv[:BLOCK]),
                                kv[BLOCK:]], axis=0)

    def sink_bias(bias, head):
        sink = sink_ref[head] * LOG2_E
        return jnp.concatenate([jnp.where(first_col, sink, bias[:, :BLOCK]), bias[:, BLOCK:]],
                               axis=1)

    for j in range(tm // BLOCK):
        chains, slots = [], []
        rows = slice(j * BLOCK, (j + 1) * BLOCK)
        bias = bias_ref[jnp.minimum(tile, 1)] if j == 0 else bias_ref[1]
        k = k_win_ref[j * BLOCK:(j + 2) * BLOCK]
        v = v_win_ref[j * BLOCK:(j + 2) * BLOCK]
        for g in range(W_A // PAIR):
            kv = (g // groups_per_kv) * PAIR
            chains.append((qa_ref[0, rows, g * PAIR:(g + 1) * PAIR],
                           with_sink_slot(k[:, kv:kv + PAIR]), with_sink_slot(v[:, kv:kv + PAIR]),
                           (sink_bias(bias, 2 * g), sink_bias(bias, 2 * g + 1))))
            slots.append((rows, g * PAIR))
        for g in range(W_C // PAIR):
            c = g * PAIR
            chains.append((qc_ref[0, rows, c:c + PAIR],
                           mk_ref[:, c:c + PAIR], mv_ref[:, c:c + PAIR], None))
            slots.append((rows, W_A + W_B + c))
        for (rows, c), (out, _) in zip(slots, _attend(chains, s_ref, p_ref)):
            y_ref[rows, c:c + PAIR] = out

    def rms_normalize(t):
        return t * lax.rsqrt(jnp.mean(t * t, axis=-1, keepdims=True) + RMS_EPS)

    y = jnp.concatenate([rms_normalize(y_ref[:, 0:W_A]),
                         rms_normalize(ob_ref[0].astype(f32)),
                         rms_normalize(y_ref[:, W_A + W_B:D_MIX])], axis=-1)
    y = (y * gate_ref[0].astype(f32)).astype(bf16)
    r_ref[...] = DEEPNORM_ALPHA * x_ref[0] + jnp.dot(y, w_out_ref[...],
                                                     preferred_element_type=f32)


def _tail_bias():
    qi = np.arange(BLOCK)[:, None]
    kj = np.arange(2 * BLOCK)[None, :]
    dist = qi + BLOCK - kj
    band = (dist >= 0) & (dist <= SWA_WINDOW - 1)
    return np.where(np.stack([band & (kj >= BLOCK), band]), 0.0, MASKED).astype(np.float32)


def _tail(sinks, qa, ka, va, qc, mem, w_mem, ob, gate, x, w_out, ln_gain, ln_bias):
    B, S, _ = x.shape
    M = mem.shape[1]
    tm = ROW_TILE
    blocks_per_tile = tm // BLOCK
    tiles_per_seq = S // tm
    n_tiles = B * tiles_per_seq
    bias = jnp.asarray(_tail_bias())

    def seq_tile(step):
        t = jnp.minimum(step, n_tiles - 1)
        return t // tiles_per_seq, t % tiles_per_seq

    def out_tile(step):
        t = jnp.maximum(step - 1, 0)
        return t // tiles_per_seq, t % tiles_per_seq

    row_block = lambda width: pl.BlockSpec((1, tm, width), lambda s: (*seq_tile(s), 0))
    prev_block = lambda width: pl.BlockSpec(
        (1, BLOCK, width),
        lambda s: (seq_tile(s)[0], jnp.maximum(seq_tile(s)[1] * blocks_per_tile - 1, 0), 0))
    const = lambda shape: pl.BlockSpec(shape, lambda s: (0,) * len(shape))
    kv_w = 2 * W_KV_A
    stage = ((W_A + W_C) // PAIR, 2 * BLOCK, 2 * BLOCK)
    pipelined = (_nbytes((tm, W_A + 2 * kv_w + W_C + W_B + D_MIX), bf16)
                 + 2 * _nbytes((BLOCK, kv_w), bf16) + _nbytes((M, D_MODEL), f32)
                 + _nbytes((D_MODEL, 2 * W_C), bf16) + 2 * _nbytes((tm, D_MODEL), f32)
                 + _nbytes((D_MIX, D_MODEL), bf16) + _nbytes(bias.shape, f32))
    scratch = [pltpu.VMEM((BLOCK + tm, kv_w), bf16), pltpu.VMEM((BLOCK + tm, kv_w), bf16),
               pltpu.VMEM((M, W_C), bf16), pltpu.VMEM((M, W_C), bf16),
               pltpu.VMEM((tm, D_MIX), f32), pltpu.VMEM(stage, f32), pltpu.VMEM(stage, bf16),
               pltpu.VMEM((tm, D_MODEL), f32)]
    scratch_bytes = (2 * _nbytes((BLOCK + tm, kv_w), bf16) + 2 * _nbytes((M, W_C), bf16)
                     + 5 * _nbytes((tm, D_MIX), f32) + _nbytes(stage, f32) + _nbytes(stage, bf16))
    return pl.pallas_call(
        functools.partial(_tail_kernel, tiles_per_seq=tiles_per_seq, n_tiles=n_tiles),
        grid=(n_tiles + 1,),
        in_specs=[pl.BlockSpec(memory_space=pltpu.SMEM),
                  row_block(W_A), row_block(kv_w), prev_block(kv_w),
                  row_block(kv_w), prev_block(kv_w), row_block(W_C),
                  pl.BlockSpec((1, M, D_MODEL), lambda s: (seq_tile(s)[0], 0, 0)),
                  const((D_MODEL, 2 * W_C)), row_block(W_B), row_block(D_MIX),
                  row_block(D_MODEL), const((D_MIX, D_MODEL)),
                  const((1, D_MODEL)), const((1, D_MODEL)), const(bias.shape)],
        out_specs=pl.BlockSpec((1, tm, D_MODEL), lambda s: (*out_tile(s), 0)),
        out_shape=jax.ShapeDtypeStruct((B, S, D_MODEL), f32),
        scratch_shapes=scratch,
        compiler_params=pltpu.CompilerParams(
            dimension_semantics=("arbitrary",),
            vmem_limit_bytes=_vmem_limit(pipelined, scratch_bytes)),
        name="mix_out",
    )(sinks, qa, ka, ka, va, va, qc, mem, w_mem, ob, gate, x, w_out, ln_gain, ln_bias, bias)


def _rope_tables(seq_len):
    pos = jnp.arange(seq_len, dtype=f32)
    inv = ROPE_THETA ** (-jnp.arange(0, HEAD_DIM, 2, dtype=f32) / HEAD_DIM)
    ang = pos[:, None] * inv[None, :]
    cos, sin = jnp.cos(ang), jnp.sin(ang)
    cos2 = jnp.concatenate([cos, cos, cos, cos], axis=-1)
    sin2 = jnp.concatenate([-sin, sin, -sin, sin], axis=-1)
    return cos2, sin2


def kernel(x, mem, w_in, b_in, w_mem, attn_sinks, g_branch, w_out, ln_gain, ln_bias):
    assert w_in.shape[0] == DEPTH
    S = x.shape[1]
    cos2, sin2 = _rope_tables(S)
    for l in range(DEPTH):
        qa, ka, va, qc, gate, *views_b = _in_proj(
            x, w_in[l].astype(bf16), b_in[l][None, :], cos2, sin2, g_branch[l][None, :])
        ob = _dilated(views_b)
        x = _tail(attn_sinks[l], qa, ka, va, qc, mem, w_mem[l].astype(bf16), ob, gate, x,
                  w_out[l].astype(bf16), ln_gain[l][None, :], ln_bias[l][None, :])
    return x
```

```python
import functools
import math

import numpy as np
import jax
import jax.numpy as jnp
from jax import lax
from jax.experimental import pallas as pl
from jax.experimental.pallas import tpu as pltpu

D_MODEL = 1024
HEAD_DIM = 64
SWA_Q_HEADS = 8
SWA_KV_HEADS = 2
SWA_WINDOW = 128
DIL_HEADS = 4
DIL_PAIRS = ((128, 1), (512, 4), (2048, 16))
MEM_HEADS = 4
BLOCK = 128
ROPE_THETA = 10000.0
LN_EPS = 1e-5
RMS_EPS = 1e-6
DEPTH = 1
DEEPNORM_ALPHA = (2 * DEPTH) ** 0.25

W_A = SWA_Q_HEADS * HEAD_DIM
W_KV_A = SWA_KV_HEADS * HEAD_DIM
W_B = DIL_HEADS * HEAD_DIM
W_C = MEM_HEADS * HEAD_DIM
D_MIX = W_A + W_B + W_C
D_IN = W_A + 2 * W_KV_A + 3 * W_B + W_C + D_MIX

V7X_LANES = 128
V7X_VMEM_BYTES = 64 * 2**20
VMEM_HEADROOM_BYTES = 8 * 2**20

PAIR = 2 * HEAD_DIM
assert PAIR == V7X_LANES
LOG2_E = math.log2(math.e)
Q_SCALE = HEAD_DIM ** -0.5 * LOG2_E
MASKED = -1e30

OFF_QA = 0
OFF_KA = OFF_QA + W_A
OFF_VA = OFF_KA + W_KV_A
OFF_QB = OFF_VA + W_KV_A
OFF_KB = OFF_QB + W_B
OFF_VB = OFF_KB + W_B
OFF_QC = OFF_VB + W_B
OFF_Z = OFF_QC + W_C
assert OFF_Z + D_MIX == D_IN

IN_ROW_TILE = 1024
ROW_TILE = 512
DIL_BLOCKS_PER_STEP = 8
DIL4_BLOCKS_PER_STEP = 2
DIL16_STREAMS_PER_STEP = 16

f32 = jnp.float32
bf16 = jnp.bfloat16


def _vmem_limit(pipelined_bytes, scratch_bytes=0):
    budget = V7X_VMEM_BYTES - VMEM_HEADROOM_BYTES
    assert 2 * pipelined_bytes + scratch_bytes + VMEM_HEADROOM_BYTES <= budget
    return budget


def _nbytes(shape, dtype):
    return int(np.prod(shape)) * jnp.dtype(dtype).itemsize


def _lane_is_first_head():
    return lax.broadcasted_iota(jnp.int32, (1, V7X_LANES), 1) < HEAD_DIM


def _in_proj_kernel(x_ref, w_ref, b_ref, cos_ref, sin_ref, g_ref,
                    qa_ref, ka_ref, va_ref, qc_ref, gate_ref,
                    q1_ref, k1_ref, v1_ref, q4_ref, k4_ref, v4_ref, q16_ref, k16_ref, v16_ref,
                    *stage_refs):
    tm = x_ref.shape[1]
    xb = x_ref[0].astype(bf16)
    cos = cos_ref[...]
    sin = sin_ref[...]
    lane = lax.broadcasted_iota(jnp.int32, (1, V7X_LANES), 1)
    first_half = (lane % HEAD_DIM) < (HEAD_DIM // 2)
    first_head = lane < HEAD_DIM

    def proj(off, width):
        acc = jnp.dot(xb, w_ref[:, off:off + width], preferred_element_type=f32)
        return acc + b_ref[:, off:off + width]

    def rope(a):
        rot = jnp.where(first_half,
                        pltpu.roll(a, V7X_LANES - HEAD_DIM // 2, 1),
                        pltpu.roll(a, HEAD_DIM // 2, 1))
        return a * cos + rot * sin

    def groups(a):
        return [a[:, i:i + V7X_LANES] for i in range(0, a.shape[1], V7X_LANES)]

    def both_halves(a):
        swapped = pltpu.roll(a, HEAD_DIM, 1)
        return jnp.where(first_head, a, swapped), jnp.where(first_head, swapped, a)

    for c in range(0, D_MIX, 2 * V7X_LANES):
        z = proj(OFF_Z + c, 2 * V7X_LANES)
        gate = z * jax.nn.sigmoid(z) * g_ref[:, c:c + 2 * V7X_LANES]
        gate_ref[0, :, c:c + 2 * V7X_LANES] = gate.astype(bf16)
    for c in range(0, W_A, 2 * V7X_LANES):
        for i, a in enumerate(groups(proj(OFF_QA + c, 2 * V7X_LANES))):
            lo = c + i * V7X_LANES
            qa_ref[0, :, lo:lo + V7X_LANES] = (rope(a) * Q_SCALE).astype(bf16)
    k_a, v_a = groups(proj(OFF_KA, 2 * W_KV_A))
    for ref, a in ((ka_ref, rope(k_a)), (va_ref, v_a)):
        kv0, kv1 = both_halves(a)
        ref[0, :, 0:V7X_LANES] = kv0.astype(bf16)
        ref[0, :, V7X_LANES:2 * V7X_LANES] = kv1.astype(bf16)
    mixer_b = ((OFF_QB, lambda a: rope(a) * Q_SCALE, (q1_ref, q4_ref, q16_ref)),
               (OFF_KB, rope, (k1_ref, k4_ref, k16_ref)),
               (OFF_VB, lambda a: a, (v1_ref, v4_ref, v16_ref)))
    dil4, dil16 = DIL_PAIRS[1][1], DIL_PAIRS[2][1]
    for t, (off, fn, (nat_ref, r4_ref, r16_ref)) in enumerate(mixer_b):
        for g, a in enumerate(groups(proj(off, W_B))):
            a = fn(a)
            nat_ref[0, :, g * PAIR:(g + 1) * PAIR] = a.astype(bf16)
            slab = t * (W_B // PAIR) + g
            stage_ref, stage4_ref = stage_refs[2 * slab], stage_refs[2 * slab + 1]
            stage_ref[...] = a
            lanes = slice(g * PAIR, (g + 1) * PAIR)
            rows4, rows16 = tm // dil4, tm // dil16
            for j4 in range(dil4):
                stream = stage_ref[pl.ds(j4, rows4, stride=dil4), :]
                r4_ref[0, j4, :, lanes] = stream.astype(bf16)
                stage4_ref[j4 * rows4:(j4 + 1) * rows4] = stream
            for j16 in range(dil16):
                j4, phase = j16 % dil4, j16 // dil4
                stream = stage4_ref[pl.ds(j4 * rows4 + phase, rows16, stride=dil16 // dil4), :]
                r16_ref[0, j16, :, lanes] = stream.astype(bf16)
    qc_ref[0] = (proj(OFF_QC, W_C) * Q_SCALE).astype(bf16)


def _in_proj(x, w_in, b_in, cos2, sin2, g_branch):
    B, S, _ = x.shape
    tm = IN_ROW_TILE
    widths = (W_A, 2 * W_KV_A, 2 * W_KV_A, W_C, D_MIX, W_B, W_B, W_B)
    row_block = lambda width: pl.BlockSpec((1, tm, width), lambda b, i: (b, i, 0))
    out_specs = [row_block(w) for w in widths]
    out_shape = [jax.ShapeDtypeStruct((B, S, w), bf16) for w in widths]
    for _, r in DIL_PAIRS[1:]:
        out_specs += [pl.BlockSpec((1, r, tm // r, W_B), lambda b, i: (b, 0, i, 0))] * 3
        out_shape += [jax.ShapeDtypeStruct((B, r, S // r, W_B), bf16)] * 3
    const = lambda shape: pl.BlockSpec(shape, lambda b, i: (0,) * len(shape))
    n_stage, stage = 2 * 3 * (W_B // PAIR), (tm, V7X_LANES)
    pipelined = (_nbytes((tm, D_MODEL), f32) + _nbytes((D_MODEL, D_IN), bf16)
                 + _nbytes((tm, sum(widths) + 6 * W_B), bf16) + 2 * _nbytes((tm, V7X_LANES), f32))
    return pl.pallas_call(
        _in_proj_kernel,
        grid=(B, S // tm),
        in_specs=[row_block(D_MODEL), const((D_MODEL, D_IN)), const((1, D_IN)),
                  pl.BlockSpec((tm, V7X_LANES), lambda b, i: (i, 0)),
                  pl.BlockSpec((tm, V7X_LANES), lambda b, i: (i, 0)),
                  const((1, D_MIX))],
        out_specs=out_specs,
        out_shape=out_shape,
        scratch_shapes=[pltpu.VMEM(stage, f32)] * n_stage,
        compiler_params=pltpu.CompilerParams(
            dimension_semantics=("arbitrary", "arbitrary"),
            vmem_limit_bytes=_vmem_limit(pipelined, n_stage * _nbytes(stage, f32))),
        name="in_proj",
    )(x, w_in, b_in, cos2, sin2, g_branch)


def _attend(chains, s_ref, p_ref):
    first_head = _lane_is_first_head()
    for c, (q2, k2, _, _) in enumerate(chains):
        zero = jnp.zeros_like(q2)
        stacked = jnp.concatenate([jnp.where(first_head, q2, zero),
                                   jnp.where(first_head, zero, q2)], axis=0)
        s_ref[c, :, 0:k2.shape[0]] = lax.dot_general(
            stacked, k2, (((1,), (1,)), ((), ())), preferred_element_type=f32)
    tops = []
    for c, (q2, k2, _, biases) in enumerate(chains):
        m_rows, n = q2.shape[0], k2.shape[0]
        pair_tops = []
        for h in range(2):
            rows = slice(h * m_rows, (h + 1) * m_rows)
            sh = s_ref[c, rows, 0:n]
            if biases is not None:
                sh = sh + biases[h]
            m = jnp.max(sh, axis=-1, keepdims=True)
            p_ref[c, rows, 0:n] = jnp.exp2(sh - m).astype(bf16)
            pair_tops.append(m)
        tops.append(pair_tops)
    results = []
    for c, (q2, k2, v2, _) in enumerate(chains):
        m_rows, n = q2.shape[0], k2.shape[0]
        v_ones = jnp.concatenate([v2, jnp.ones_like(v2)], axis=1)
        o = jnp.dot(p_ref[c, :, 0:n], v_ones, preferred_element_type=f32)
        acc = jnp.where(first_head, o[:m_rows, :PAIR], o[m_rows:, :PAIR])
        denom = jnp.where(first_head, o[:m_rows, PAIR:], o[m_rows:, PAIR:])
        lse = jnp.where(first_head, tops[c][0], tops[c][1]) + jnp.log(denom) * LOG2_E
        results.append((acc * (1.0 / denom), lse))
    return results


def _window_bias(max_dist):
    qi = np.arange(BLOCK)[:, None]
    kj = np.arange(2 * BLOCK)[None, :]
    dist = np.stack([qi - kj, qi + BLOCK - kj])
    return np.where((dist >= 0) & (dist <= max_dist), 0.0, MASKED).astype(np.float32)


def _window(blk):
    r0 = pl.multiple_of(blk * BLOCK, BLOCK)
    k0 = pl.multiple_of(jnp.maximum(blk - 1, 0) * BLOCK, BLOCK)
    return r0, k0, jnp.minimum(blk, 1)


def _merge(o_a, l_a, o_b, l_b):
    top = jnp.maximum(l_a, l_b)
    e_a = jnp.exp2(l_a - top)
    e_b = jnp.exp2(l_b - top)
    denom = e_a + e_b
    return (e_a * o_a + e_b * o_b) * (1.0 / denom), top + jnp.log(denom) * LOG2_E


def _dilated_kernel(q1_ref, k1_ref, v1_ref, q4_ref, k4_ref, v4_ref, q16_ref, k16_ref, v16_ref,
                    bias_ref, o_ref, o4_ref, l4_ref, o16_ref, l16_ref, s_ref, p_ref):
    seq_len = q1_ref.shape[1]
    n_pairs = W_B // PAIR
    dil4, dil16 = DIL_PAIRS[1][1], DIL_PAIRS[2][1]
    sub = dil16 // dil4

    causal = bias_ref[0, :, 0:BLOCK]

    def body16(i, carry):
        streams = [i * DIL16_STREAMS_PER_STEP + u for u in range(DIL16_STREAMS_PER_STEP)]
        chains = [(q16_ref[0, j, :, g * PAIR:(g + 1) * PAIR],
                   k16_ref[0, j, :, g * PAIR:(g + 1) * PAIR],
                   v16_ref[0, j, :, g * PAIR:(g + 1) * PAIR], (causal, causal))
                  for j in streams for g in range(n_pairs)]
        results = _attend(chains, s_ref, p_ref)
        for n, (out, lse) in enumerate(results):
            j, g = streams[n // n_pairs], n % n_pairs
            stream4, phase = lax.rem(j, dil4), lax.div(j, dil4)
            o16_ref[g, stream4, pl.ds(phase, BLOCK, stride=sub), :] = out
            l16_ref[g, stream4, pl.ds(phase, BLOCK, stride=sub), :] = lse
        return carry

    lax.fori_loop(0, dil16 // DIL16_STREAMS_PER_STEP, body16, 0)

    def body4(i, carry):
        windows = [_window(i * DIL4_BLOCKS_PER_STEP + u) for u in range(DIL4_BLOCKS_PER_STEP)]
        chains = [(q4_ref[0, j, pl.ds(u0, BLOCK), g * PAIR:(g + 1) * PAIR],
                   k4_ref[0, j, pl.ds(k0, 2 * BLOCK), g * PAIR:(g + 1) * PAIR],
                   v4_ref[0, j, pl.ds(k0, 2 * BLOCK), g * PAIR:(g + 1) * PAIR],
                   (bias_ref[first], bias_ref[first]))
                  for u0, k0, first in windows for j in range(dil4) for g in range(n_pairs)]
        results = _attend(chains, s_ref, p_ref)
        for n, (out, lse) in enumerate(results):
            u0 = windows[n // (dil4 * n_pairs)][0]
            j, g = (n // n_pairs) % dil4, n % n_pairs
            out, lse = _merge(out, lse, o16_ref[g, j, pl.ds(u0, BLOCK), :],
                              l16_ref[g, j, pl.ds(u0, BLOCK), :])
            o4_ref[g, pl.ds(u0 * dil4 + j, BLOCK, stride=dil4), :] = out
            l4_ref[g, pl.ds(u0 * dil4 + j, BLOCK, stride=dil4), :] = lse
        return carry

    lax.fori_loop(0, seq_len // dil4 // BLOCK // DIL4_BLOCKS_PER_STEP, body4, 0)

    def body1(i, carry):
        windows = [_window(i * DIL_BLOCKS_PER_STEP + u) for u in range(DIL_BLOCKS_PER_STEP)]
        chains = [(q1_ref[0, pl.ds(r0, BLOCK), g * PAIR:(g + 1) * PAIR],
                   k1_ref[0, pl.ds(k0, 2 * BLOCK), g * PAIR:(g + 1) * PAIR],
                   v1_ref[0, pl.ds(k0, 2 * BLOCK), g * PAIR:(g + 1) * PAIR],
                   (bias_ref[first], bias_ref[first]))
                  for r0, k0, first in windows for g in range(n_pairs)]
        results = _attend(chains, s_ref, p_ref)
        for n, (out, lse) in enumerate(results):
            r0, g = windows[n // n_pairs][0], n % n_pairs
            out, _ = _merge(out, lse, o4_ref[g, pl.ds(r0, BLOCK), :],
                            l4_ref[g, pl.ds(r0, BLOCK), :])
            o_ref[0, pl.ds(r0, BLOCK), g * PAIR:(g + 1) * PAIR] = out.astype(bf16)
        return carry

    lax.fori_loop(0, seq_len // BLOCK // DIL_BLOCKS_PER_STEP, body1, 0)


def _dilated(views):
    B, S, _ = views[0].shape
    assert all(w // r == BLOCK for w, r in DIL_PAIRS) and S // DIL_PAIRS[2][1] == BLOCK
    assert DIL_PAIRS[2][1] % DIL16_STREAMS_PER_STEP == 0
    bias = jnp.asarray(_window_bias(BLOCK))
    view_specs = [pl.BlockSpec((1, S, W_B), lambda b: (b, 0, 0))] * 3
    for _, r in DIL_PAIRS[1:]:
        view_specs += [pl.BlockSpec((1, r, S // r, W_B), lambda b: (b, 0, 0, 0))] * 3
    slab = (W_B // PAIR, S, V7X_LANES)
    slab16 = (W_B // PAIR, DIL_PAIRS[1][1], S // DIL_PAIRS[1][1], V7X_LANES)
    chains = max(DIL_BLOCKS_PER_STEP, DIL16_STREAMS_PER_STEP,
                 DIL4_BLOCKS_PER_STEP * DIL_PAIRS[1][1]) * (W_B // PAIR)
    stage = (chains, 2 * BLOCK, 2 * BLOCK)
    pipelined = 10 * _nbytes((S, W_B), bf16) + _nbytes(bias.shape, f32)
    return pl.pallas_call(
        _dilated_kernel,
        grid=(B,),
        in_specs=view_specs + [pl.BlockSpec(bias.shape, lambda b: (0, 0, 0))],
        out_specs=pl.BlockSpec((1, S, W_B), lambda b: (b, 0, 0)),
        out_shape=jax.ShapeDtypeStruct((B, S, W_B), bf16),
        scratch_shapes=[pltpu.VMEM(slab, f32), pltpu.VMEM(slab, f32),
                        pltpu.VMEM(slab16, f32), pltpu.VMEM(slab16, f32),
                        pltpu.VMEM(stage, f32), pltpu.VMEM(stage, bf16)],
        compiler_params=pltpu.CompilerParams(
            dimension_semantics=("arbitrary",),
            vmem_limit_bytes=_vmem_limit(
                pipelined, 4 * _nbytes(slab, f32) + _nbytes(stage, f32) + _nbytes(stage, bf16))),
        name="dilated_attention",
    )(*views, bias)


def _tail_kernel(sink_ref, qa_ref, ka_ref, ka_prev_ref, va_ref, va_prev_ref, qc_ref, mem_ref,
                 w_mem_ref, ob_ref, gate_ref, x_ref, w_out_ref, gain_ref, ln_bias_ref, bias_ref,
                 out_ref, k_win_ref, v_win_ref, mk_ref, mv_ref, y_ref, s_ref, p_ref, r_ref,
                 *, tiles_per_seq, n_tiles):
    step = pl.program_id(0)
    tile = lax.rem(jnp.minimum(step, n_tiles - 1), tiles_per_seq)
    tm = x_ref.shape[1]
    groups_per_kv = (SWA_Q_HEADS // SWA_KV_HEADS) // 2

    @pl.when(step == 0)
    def _():
        r_ref[...] = jnp.zeros_like(r_ref)

    @pl.when(tile == 0)
    def _():
        mkv = jnp.dot(mem_ref[0].astype(bf16), w_mem_ref[...], preferred_element_type=f32)
        mk_ref[...] = mkv[:, :W_C].astype(bf16)
        mv_ref[...] = mkv[:, W_C:].astype(bf16)

    r = r_ref[...]
    mu = jnp.mean(r, axis=-1, keepdims=True)
    d = r - mu
    var = jnp.mean(d * d, axis=-1, keepdims=True)
    out_ref[0] = d * lax.rsqrt(var + LN_EPS) * gain_ref[...] + ln_bias_ref[...]

    for win_ref, prev_ref, cur_ref in ((k_win_ref, ka_prev_ref, ka_ref),
                                       (v_win_ref, va_prev_ref, va_ref)):
        win_ref[0:BLOCK] = prev_ref[0]
        win_ref[BLOCK:BLOCK + tm] = cur_ref[0]

    first_row = lax.broadcasted_iota(jnp.int32, (BLOCK, 1), 0) == 0
    first_col = lax.broadcasted_iota(jnp.int32, (1, BLOCK), 1) == 0

    def with_sink_slot(kv):
        return jnp.concatenate([jnp.where(first_row, jnp.zeros_like(kv[:BLOCK]), kv[:BLOCK]),
                                kv[BLOCK:]], axis=0)

    def sink_bias(bias, head):
        sink = sink_ref[head] * LOG2_E
        return jnp.concatenate([jnp.where(first_col, sink, bias[:, :BLOCK]), bias[:, BLOCK:]],
                               axis=1)

    for j in range(tm // BLOCK):
        chains, slots = [], []
        rows = slice(j * BLOCK, (j + 1) * BLOCK)
        bias = bias_ref[jnp.minimum(tile, 1)] if j == 0 else bias_ref[1]
        k = k_win_ref[j * BLOCK:(j + 2) * BLOCK]
        v = v_win_ref[j * BLOCK:(j + 2) * BLOCK]
        for g in range(W_A // PAIR):
            kv = (g // groups_per_kv) * PAIR
            chains.append((qa_ref[0, rows, g * PAIR:(g + 1) * PAIR],
                           with_sink_slot(k[:, kv:kv + PAIR]), with_sink_slot(v[:, kv:kv + PAIR]),
                           (sink_bias(bias, 2 * g), sink_bias(bias, 2 * g + 1))))
            slots.append((rows, g * PAIR))
        for g in range(W_C // PAIR):
            c = g * PAIR
            chains.append((qc_ref[0, rows, c:c + PAIR],
                           mk_ref[:, c:c + PAIR], mv_ref[:, c:c + PAIR], None))
            slots.append((rows, W_A + W_B + c))
        for (rows, c), (out, _) in zip(slots, _attend(chains, s_ref, p_ref)):
            y_ref[rows, c:c + PAIR] = out

    def rms_normalize(t):
        return t * lax.rsqrt(jnp.mean(t * t, axis=-1, keepdims=True) + RMS_EPS)

    y = jnp.concatenate([rms_normalize(y_ref[:, 0:W_A]),
                         rms_normalize(ob_ref[0].astype(f32)),
                         rms_normalize(y_ref[:, W_A + W_B:D_MIX])], axis=-1)
    y = (y * gate_ref[0].astype(f32)).astype(bf16)
    r_ref[...] = DEEPNORM_ALPHA * x_ref[0] + jnp.dot(y, w_out_ref[...],
                                                     preferred_element_type=f32)


def _tail_bias():
    qi = np.arange(BLOCK)[:, None]
    kj = np.arange(2 * BLOCK)[None, :]
    dist = qi + BLOCK - kj
    band = (dist >= 0) & (dist <= SWA_WINDOW - 1)
    return np.where(np.stack([band & (kj >= BLOCK), band]), 0.0, MASKED).astype(np.float32)


def _tail(sinks, qa, ka, va, qc, mem, w_mem, ob, gate, x, w_out, ln_gain, ln_bias):
    B, S, _ = x.shape
    M = mem.shape[1]
    tm = ROW_TILE
    blocks_per_tile = tm // BLOCK
    tiles_per_seq = S // tm
    n_tiles = B * tiles_per_seq
    bias = jnp.asarray(_tail_bias())

    def seq_tile(step):
        t = jnp.minimum(step, n_tiles - 1)
        return t // tiles_per_seq, t % tiles_per_seq

    def out_tile(step):
        t = jnp.maximum(step - 1, 0)
        return t // tiles_per_seq, t % tiles_per_seq

    row_block = lambda width: pl.BlockSpec((1, tm, width), lambda s: (*seq_tile(s), 0))
    prev_block = lambda width: pl.BlockSpec(
        (1, BLOCK, width),
        lambda s: (seq_tile(s)[0], jnp.maximum(seq_tile(s)[1] * blocks_per_tile - 1, 0), 0))
    const = lambda shape: pl.BlockSpec(shape, lambda s: (0,) * len(shape))
    kv_w = 2 * W_KV_A
    stage = ((W_A + W_C) // PAIR, 2 * BLOCK, 2 * BLOCK)
    pipelined = (_nbytes((tm, W_A + 2 * kv_w + W_C + W_B + D_MIX), bf16)
                 + 2 * _nbytes((BLOCK, kv_w), bf16) + _nbytes((M, D_MODEL), f32)
                 + _nbytes((D_MODEL, 2 * W_C), bf16) + 2 * _nbytes((tm, D_MODEL), f32)
                 + _nbytes((D_MIX, D_MODEL), bf16) + _nbytes(bias.shape, f32))
    scratch = [pltpu.VMEM((BLOCK + tm, kv_w), bf16), pltpu.VMEM((BLOCK + tm, kv_w), bf16),
               pltpu.VMEM((M, W_C), bf16), pltpu.VMEM((M, W_C), bf16),
               pltpu.VMEM((tm, D_MIX), f32), pltpu.VMEM(stage, f32), pltpu.VMEM(stage, bf16),
               pltpu.VMEM((tm, D_MODEL), f32)]
    scratch_bytes = (2 * _nbytes((BLOCK + tm, kv_w), bf16) + 2 * _nbytes((M, W_C), bf16)
                     + 5 * _nbytes((tm, D_MIX), f32) + _nbytes(stage, f32) + _nbytes(stage, bf16))
    return pl.pallas_call(
        functools.partial(_tail_kernel, tiles_per_seq=tiles_per_seq, n_tiles=n_tiles),
        grid=(n_tiles + 1,),
        in_specs=[pl.BlockSpec(memory_space=pltpu.SMEM),
                  row_block(W_A), row_block(kv_w), prev_block(kv_w),
                  row_block(kv_w), prev_block(kv_w), row_block(W_C),
                  pl.BlockSpec((1, M, D_MODEL), lambda s: (seq_tile(s)[0], 0, 0)),
                  const((D_MODEL, 2 * W_C)), row_block(W_B), row_block(D_MIX),
                  row_block(D_MODEL), const((D_MIX, D_MODEL)),
                  const((1, D_MODEL)), const((1, D_MODEL)), const(bias.shape)],
        out_specs=pl.BlockSpec((1, tm, D_MODEL), lambda s: (*out_tile(s), 0)),
        out_shape=jax.ShapeDtypeStruct((B, S, D_MODEL), f32),
        scratch_shapes=scratch,
        compiler_params=pltpu.CompilerParams(
            dimension_semantics=("arbitrary",),
            vmem_limit_bytes=_vmem_limit(pipelined, scratch_bytes)),
        name="mix_out",
    )(sinks, qa, ka, ka, va, va, qc, mem, w_mem, ob, gate, x, w_out, ln_gain, ln_bias, bias)


def _rope_tables(seq_len):
    pos = np.arange(seq_len, dtype=np.float64)
    inv = ROPE_THETA ** (-np.arange(0, HEAD_DIM, 2, dtype=np.float64) / HEAD_DIM)
    ang = pos[:, None] * inv[None, :]
    cos, sin = np.cos(ang), np.sin(ang)
    cos2 = np.concatenate([cos, cos, cos, cos], axis=-1)
    sin2 = np.concatenate([-sin, sin, -sin, sin], axis=-1)
    return jnp.asarray(cos2, f32), jnp.asarray(sin2, f32)


def kernel(x, mem, w_in, b_in, w_mem, attn_sinks, g_branch, w_out, ln_gain, ln_bias):
    assert w_in.shape[0] == DEPTH
    S = x.shape[1]
    cos2, sin2 = _rope_tables(S)
    for l in range(DEPTH):
        qa, ka, va, qc, gate, *views_b = _in_proj(
            x, w_in[l].astype(bf16), b_in[l][None, :], cos2, sin2, g_branch[l][None, :])
        ob = _dilated(views_b)
        x = _tail(attn_sinks[l], qa, ka, va, qc, mem, w_mem[l].astype(bf16), ob, gate, x,
                  w_out[l].astype(bf16), ln_gain[l][None, :], ln_bias[l][None, :])
    return x
```

```python
import functools
import math

import numpy as np
import jax
import jax.numpy as jnp
from jax import lax
from jax.experimental import pallas as pl
from jax.experimental.pallas import tpu as pltpu

D_MODEL = 1024
HEAD_DIM = 64
SWA_Q_HEADS = 8
SWA_KV_HEADS = 2
SWA_WINDOW = 128
DIL_HEADS = 4
DIL_PAIRS = ((128, 1), (512, 4), (2048, 16))
MEM_HEADS = 4
BLOCK = 128
ROPE_THETA = 10000.0
LN_EPS = 1e-5
RMS_EPS = 1e-6
DEPTH = 1
DEEPNORM_ALPHA = (2 * DEPTH) ** 0.25

W_A = SWA_Q_HEADS * HEAD_DIM
W_KV_A = SWA_KV_HEADS * HEAD_DIM
W_B = DIL_HEADS * HEAD_DIM
W_C = MEM_HEADS * HEAD_DIM
D_MIX = W_A + W_B + W_C
D_IN = W_A + 2 * W_KV_A + 3 * W_B + W_C + D_MIX

V7X_LANES = 128
V7X_VMEM_BYTES = 64 * 2**20
VMEM_HEADROOM_BYTES = 8 * 2**20

PAIR = 2 * HEAD_DIM
assert PAIR == V7X_LANES
LOG2_E = math.log2(math.e)
Q_SCALE = HEAD_DIM ** -0.5 * LOG2_E
MASKED = -1e30

OFF_QA = 0
OFF_KA = OFF_QA + W_A
OFF_VA = OFF_KA + W_KV_A
OFF_QB = OFF_VA + W_KV_A
OFF_KB = OFF_QB + W_B
OFF_VB = OFF_KB + W_B
OFF_QC = OFF_VB + W_B
OFF_Z = OFF_QC + W_C
assert OFF_Z + D_MIX == D_IN

IN_ROW_TILE = 1024
ROW_TILE = 512
DIL_BLOCKS_PER_STEP = 16
DIL4_BLOCKS_PER_STEP = 4
DIL16_STREAMS_PER_STEP = 16

f32 = jnp.float32
bf16 = jnp.bfloat16


def _vmem_limit(pipelined_bytes, scratch_bytes=0):
    budget = V7X_VMEM_BYTES - VMEM_HEADROOM_BYTES
    assert 2 * pipelined_bytes + scratch_bytes + VMEM_HEADROOM_BYTES <= budget
    return budget


def _nbytes(shape, dtype):
    return int(np.prod(shape)) * jnp.dtype(dtype).itemsize


def _lane_is_first_head():
    return lax.broadcasted_iota(jnp.int32, (1, V7X_LANES), 1) < HEAD_DIM


def _in_proj_kernel(x_ref, w_ref, b_ref, cos_ref, sin_ref, g_ref,
                    qa_ref, ka_ref, va_ref, qc_ref, gate_ref,
                    q1_ref, k1_ref, v1_ref, q4_ref, k4_ref, v4_ref, q16_ref, k16_ref, v16_ref,
                    *stage_refs):
    tm = x_ref.shape[1]
    xb = x_ref[0].astype(bf16)
    cos = cos_ref[...]
    sin = sin_ref[...]
    lane = lax.broadcasted_iota(jnp.int32, (1, V7X_LANES), 1)
    first_half = (lane % HEAD_DIM) < (HEAD_DIM // 2)
    first_head = lane < HEAD_DIM

    def proj(off, width):
        acc = jnp.dot(xb, w_ref[:, off:off + width], preferred_element_type=f32)
        return acc + b_ref[:, off:off + width]

    def rope(a):
        rot = jnp.where(first_half,
                        pltpu.roll(a, V7X_LANES - HEAD_DIM // 2, 1),
                        pltpu.roll(a, HEAD_DIM // 2, 1))
        return a * cos + rot * sin

    def groups(a):
        return [a[:, i:i + V7X_LANES] for i in range(0, a.shape[1], V7X_LANES)]

    def both_halves(a):
        swapped = pltpu.roll(a, HEAD_DIM, 1)
        return jnp.where(first_head, a, swapped), jnp.where(first_head, swapped, a)

    for c in range(0, D_MIX, 2 * V7X_LANES):
        z = proj(OFF_Z + c, 2 * V7X_LANES)
        gate = z * jax.nn.sigmoid(z) * g_ref[:, c:c + 2 * V7X_LANES]
        gate_ref[0, :, c:c + 2 * V7X_LANES] = gate.astype(bf16)
    for c in range(0, W_A, 2 * V7X_LANES):
        for i, a in enumerate(groups(proj(OFF_QA + c, 2 * V7X_LANES))):
            lo = c + i * V7X_LANES
            qa_ref[0, :, lo:lo + V7X_LANES] = (rope(a) * Q_SCALE).astype(bf16)
    k_a, v_a = groups(proj(OFF_KA, 2 * W_KV_A))
    for ref, a in ((ka_ref, rope(k_a)), (va_ref, v_a)):
        kv0, kv1 = both_halves(a)
        ref[0, :, 0:V7X_LANES] = kv0.astype(bf16)
        ref[0, :, V7X_LANES:2 * V7X_LANES] = kv1.astype(bf16)
    mixer_b = ((OFF_QB, lambda a: rope(a) * Q_SCALE, (q1_ref, q4_ref, q16_ref)),
               (OFF_KB, rope, (k1_ref, k4_ref, k16_ref)),
               (OFF_VB, lambda a: a, (v1_ref, v4_ref, v16_ref)))
    dil4, dil16 = DIL_PAIRS[1][1], DIL_PAIRS[2][1]
    for t, (off, fn, (nat_ref, r4_ref, r16_ref)) in enumerate(mixer_b):
        for g, a in enumerate(groups(proj(off, W_B))):
            a = fn(a)
            nat_ref[0, :, g * PAIR:(g + 1) * PAIR] = a.astype(bf16)
            slab = t * (W_B // PAIR) + g
            stage_ref, stage4_ref = stage_refs[2 * slab], stage_refs[2 * slab + 1]
            stage_ref[...] = a
            lanes = slice(g * PAIR, (g + 1) * PAIR)
            rows4, rows16 = tm // dil4, tm // dil16
            for j4 in range(dil4):
                stream = stage_ref[pl.ds(j4, rows4, stride=dil4), :]
                r4_ref[0, j4, :, lanes] = stream.astype(bf16)
                stage4_ref[j4 * rows4:(j4 + 1) * rows4] = stream
            for j16 in range(dil16):
                j4, phase = j16 % dil4, j16 // dil4
                stream = stage4_ref[pl.ds(j4 * rows4 + phase, rows16, stride=dil16 // dil4), :]
                r16_ref[0, j16, :, lanes] = stream.astype(bf16)
    qc_ref[0] = (proj(OFF_QC, W_C) * Q_SCALE).astype(bf16)


def _in_proj(x, w_in, b_in, cos2, sin2, g_branch):
    B, S, _ = x.shape
    tm = IN_ROW_TILE
    widths = (W_A, 2 * W_KV_A, 2 * W_KV_A, W_C, D_MIX, W_B, W_B, W_B)
    row_block = lambda width: pl.BlockSpec((1, tm, width), lambda b, i: (b, i, 0))
    out_specs = [row_block(w) for w in widths]
    out_shape = [jax.ShapeDtypeStruct((B, S, w), bf16) for w in widths]
    for _, r in DIL_PAIRS[1:]:
        out_specs += [pl.BlockSpec((1, r, tm // r, W_B), lambda b, i: (b, 0, i, 0))] * 3
        out_shape += [jax.ShapeDtypeStruct((B, r, S // r, W_B), bf16)] * 3
    const = lambda shape: pl.BlockSpec(shape, lambda b, i: (0,) * len(shape))
    n_stage, stage = 2 * 3 * (W_B // PAIR), (tm, V7X_LANES)
    pipelined = (_nbytes((tm, D_MODEL), f32) + _nbytes((D_MODEL, D_IN), bf16)
                 + _nbytes((tm, sum(widths) + 6 * W_B), bf16) + 2 * _nbytes((tm, V7X_LANES), f32))
    return pl.pallas_call(
        _in_proj_kernel,
        grid=(B, S // tm),
        in_specs=[row_block(D_MODEL), const((D_MODEL, D_IN)), const((1, D_IN)),
                  pl.BlockSpec((tm, V7X_LANES), lambda b, i: (i, 0)),
                  pl.BlockSpec((tm, V7X_LANES), lambda b, i: (i, 0)),
                  const((1, D_MIX))],
        out_specs=out_specs,
        out_shape=out_shape,
        scratch_shapes=[pltpu.VMEM(stage, f32)] * n_stage,
        compiler_params=pltpu.CompilerParams(
            dimension_semantics=("arbitrary", "arbitrary"),
            vmem_limit_bytes=_vmem_limit(pipelined, n_stage * _nbytes(stage, f32))),
        name="in_proj",
    )(x, w_in, b_in, cos2, sin2, g_branch)


def _attend(chains, s_ref, p_ref):
    first_head = _lane_is_first_head()
    for c, (q2, k2, _, _) in enumerate(chains):
        zero = jnp.zeros_like(q2)
        stacked = jnp.concatenate([jnp.where(first_head, q2, zero),
                                   jnp.where(first_head, zero, q2)], axis=0)
        s_ref[c, :, 0:k2.shape[0]] = lax.dot_general(
            stacked, k2, (((1,), (1,)), ((), ())), preferred_element_type=f32)
    tops = []
    for c, (q2, k2, _, biases) in enumerate(chains):
        m_rows, n = q2.shape[0], k2.shape[0]
        pair_tops = []
        for h in range(2):
            rows = slice(h * m_rows, (h + 1) * m_rows)
            sh = s_ref[c, rows, 0:n]
            if biases is not None:
                sh = sh + biases[h]
            m = jnp.max(sh, axis=-1, keepdims=True)
            p_ref[c, rows, 0:n] = jnp.exp2(sh - m).astype(bf16)
            pair_tops.append(m)
        tops.append(pair_tops)
    results = []
    for c, (q2, k2, v2, _) in enumerate(chains):
        m_rows, n = q2.shape[0], k2.shape[0]
        v_ones = jnp.concatenate([v2, jnp.ones_like(v2)], axis=1)
        o = jnp.dot(p_ref[c, :, 0:n], v_ones, preferred_element_type=f32)
        acc = jnp.where(first_head, o[:m_rows, :PAIR], o[m_rows:, :PAIR])
        denom = jnp.where(first_head, o[:m_rows, PAIR:], o[m_rows:, PAIR:])
        lse = jnp.where(first_head, tops[c][0], tops[c][1]) + jnp.log(denom) * LOG2_E
        results.append((acc * (1.0 / denom), lse))
    return results


def _window_bias(max_dist):
    qi = np.arange(BLOCK)[:, None]
    kj = np.arange(2 * BLOCK)[None, :]
    dist = np.stack([qi - kj, qi + BLOCK - kj])
    return np.where((dist >= 0) & (dist <= max_dist), 0.0, MASKED).astype(np.float32)


def _window(blk):
    r0 = pl.multiple_of(blk * BLOCK, BLOCK)
    k0 = pl.multiple_of(jnp.maximum(blk - 1, 0) * BLOCK, BLOCK)
    return r0, k0, jnp.minimum(blk, 1)


def _merge(o_a, l_a, o_b, l_b):
    top = jnp.maximum(l_a, l_b)
    e_a = jnp.exp2(l_a - top)
    e_b = jnp.exp2(l_b - top)
    denom = e_a + e_b
    return (e_a * o_a + e_b * o_b) * (1.0 / denom), top + jnp.log(denom) * LOG2_E


def _dilated_kernel(q1_ref, k1_ref, v1_ref, q4_ref, k4_ref, v4_ref, q16_ref, k16_ref, v16_ref,
                    bias_ref, o_ref, o4_ref, l4_ref, o16_ref, l16_ref, s_ref, p_ref):
    seq_len = q1_ref.shape[1]
    n_pairs = W_B // PAIR
    dil4, dil16 = DIL_PAIRS[1][1], DIL_PAIRS[2][1]
    sub = dil16 // dil4

    causal = bias_ref[0, :, 0:BLOCK]

    def body16(i, carry):
        streams = [i * DIL16_STREAMS_PER_STEP + u for u in range(DIL16_STREAMS_PER_STEP)]
        chains = [(q16_ref[0, j, :, g * PAIR:(g + 1) * PAIR],
                   k16_ref[0, j, :, g * PAIR:(g + 1) * PAIR],
                   v16_ref[0, j, :, g * PAIR:(g + 1) * PAIR], (causal, causal))
                  for j in streams for g in range(n_pairs)]
        results = _attend(chains, s_ref, p_ref)
        for n, (out, lse) in enumerate(results):
            j, g = streams[n // n_pairs], n % n_pairs
            stream4, phase = lax.rem(j, dil4), lax.div(j, dil4)
            o16_ref[g, stream4, pl.ds(phase, BLOCK, stride=sub), :] = out
            l16_ref[g, stream4, pl.ds(phase, BLOCK, stride=sub), :] = lse
        return carry

    lax.fori_loop(0, dil16 // DIL16_STREAMS_PER_STEP, body16, 0)

    def body4(i, carry):
        windows = [_window(i * DIL4_BLOCKS_PER_STEP + u) for u in range(DIL4_BLOCKS_PER_STEP)]
        chains = [(q4_ref[0, j, pl.ds(u0, BLOCK), g * PAIR:(g + 1) * PAIR],
                   k4_ref[0, j, pl.ds(k0, 2 * BLOCK), g * PAIR:(g + 1) * PAIR],
                   v4_ref[0, j, pl.ds(k0, 2 * BLOCK), g * PAIR:(g + 1) * PAIR],
                   (bias_ref[first], bias_ref[first]))
                  for u0, k0, first in windows for j in range(dil4) for g in range(n_pairs)]
        results = _attend(chains, s_ref, p_ref)
        for n, (out, lse) in enumerate(results):
            u0 = windows[n // (dil4 * n_pairs)][0]
            j, g = (n // n_pairs) % dil4, n % n_pairs
            out, lse = _merge(out, lse, o16_ref[g, j, pl.ds(u0, BLOCK), :],
                              l16_ref[g, j, pl.ds(u0, BLOCK), :])
            o4_ref[g, pl.ds(u0 * dil4 + j, BLOCK, stride=dil4), :] = out
            l4_ref[g, pl.ds(u0 * dil4 + j, BLOCK, stride=dil4), :] = lse
        return carry

    lax.fori_loop(0, seq_len // dil4 // BLOCK // DIL4_BLOCKS_PER_STEP, body4, 0)

    def body1(i, carry):
        windows = [_window(i * DIL_BLOCKS_PER_STEP + u) for u in range(DIL_BLOCKS_PER_STEP)]
        chains = [(q1_ref[0, pl.ds(r0, BLOCK), g * PAIR:(g + 1) * PAIR],
                   k1_ref[0, pl.ds(k0, 2 * BLOCK), g * PAIR:(g + 1) * PAIR],
                   v1_ref[0, pl.ds(k0, 2 * BLOCK), g * PAIR:(g + 1) * PAIR],
                   (bias_ref[first], bias_ref[first]))
                  for r0, k0, first in windows for g in range(n_pairs)]
        results = _attend(chains, s_ref, p_ref)
        for n, (out, lse) in enumerate(results):
            r0, g = windows[n // n_pairs][0], n % n_pairs
            out, _ = _merge(out, lse, o4_ref[g, pl.ds(r0, BLOCK), :],
                            l4_ref[g, pl.ds(r0, BLOCK), :])
            o_ref[0, pl.ds(r0, BLOCK), g * PAIR:(g + 1) * PAIR] = out.astype(bf16)
        return carry

    lax.fori_loop(0, seq_len // BLOCK // DIL_BLOCKS_PER_STEP, body1, 0)


def _dilated(views):
    B, S, _ = views[0].shape
    assert all(w // r == BLOCK for w, r in DIL_PAIRS) and S // DIL_PAIRS[2][1] == BLOCK
    assert DIL_PAIRS[2][1] % DIL16_STREAMS_PER_STEP == 0
    bias = jnp.asarray(_window_bias(BLOCK))
    view_specs = [pl.BlockSpec((1, S, W_B), lambda b: (b, 0, 0))] * 3
    for _, r in DIL_PAIRS[1:]:
        view_specs += [pl.BlockSpec((1, r, S // r, W_B), lambda b: (b, 0, 0, 0))] * 3
    slab = (W_B // PAIR, S, V7X_LANES)
    slab16 = (W_B // PAIR, DIL_PAIRS[1][1], S // DIL_PAIRS[1][1], V7X_LANES)
    chains = max(DIL_BLOCKS_PER_STEP, DIL16_STREAMS_PER_STEP,
                 DIL4_BLOCKS_PER_STEP * DIL_PAIRS[1][1]) * (W_B // PAIR)
    stage = (chains, 2 * BLOCK, 2 * BLOCK)
    pipelined = 10 * _nbytes((S, W_B), bf16) + _nbytes(bias.shape, f32)
    return pl.pallas_call(
        _dilated_kernel,
        grid=(B,),
        in_specs=view_specs + [pl.BlockSpec(bias.shape, lambda b: (0, 0, 0))],
        out_specs=pl.BlockSpec((1, S, W_B), lambda b: (b, 0, 0)),
        out_shape=jax.ShapeDtypeStruct((B, S, W_B), bf16),
        scratch_shapes=[pltpu.VMEM(slab, f32), pltpu.VMEM(slab, f32),
                        pltpu.VMEM(slab16, f32), pltpu.VMEM(slab16, f32),
                        pltpu.VMEM(stage, f32), pltpu.VMEM(stage, bf16)],
        compiler_params=pltpu.CompilerParams(
            dimension_semantics=("arbitrary",),
            vmem_limit_bytes=_vmem_limit(
                pipelined, 4 * _nbytes(slab, f32) + _nbytes(stage, f32) + _nbytes(stage, bf16))),
        name="dilated_attention",
    )(*views, bias)


def _tail_kernel(sink_ref, qa_ref, ka_ref, ka_prev_ref, va_ref, va_prev_ref, qc_ref, mem_ref,
                 w_mem_ref, ob_ref, gate_ref, x_ref, w_out_ref, gain_ref, ln_bias_ref, bias_ref,
                 out_ref, k_win_ref, v_win_ref, mk_ref, mv_ref, y_ref, s_ref, p_ref, r_ref,
                 *, tiles_per_seq, n_tiles):
    step = pl.program_id(0)
    tile = lax.rem(jnp.minimum(step, n_tiles - 1), tiles_per_seq)
    tm = x_ref.shape[1]
    groups_per_kv = (SWA_Q_HEADS // SWA_KV_HEADS) // 2

    @pl.when(step == 0)
    def _():
        r_ref[...] = jnp.zeros_like(r_ref)

    @pl.when(tile == 0)
    def _():
        mkv = jnp.dot(mem_ref[0].astype(bf16), w_mem_ref[...], preferred_element_type=f32)
        mk_ref[...] = mkv[:, :W_C].astype(bf16)
        mv_ref[...] = mkv[:, W_C:].astype(bf16)

    r = r_ref[...]
    mu = jnp.mean(r, axis=-1, keepdims=True)
    d = r - mu
    var = jnp.mean(d * d, axis=-1, keepdims=True)
    out_ref[0] = d * lax.rsqrt(var + LN_EPS) * gain_ref[...] + ln_bias_ref[...]

    for win_ref, prev_ref, cur_ref in ((k_win_ref, ka_prev_ref, ka_ref),
                                       (v_win_ref, va_prev_ref, va_ref)):
        win_ref[0:BLOCK] = prev_ref[0]
        win_ref[BLOCK:BLOCK + tm] = cur_ref[0]

    first_row = lax.broadcasted_iota(jnp.int32, (BLOCK, 1), 0) == 0
    first_col = lax.broadcasted_iota(jnp.int32, (1, BLOCK), 1) == 0

    def with_sink_slot(kv):
        return jnp.concatenate([jnp.where(first_row, jnp.zeros_like(kv[:BLOCK]), kv[:BLOCK]),
                                kv[BLOCK:]], axis=0)

    def sink_bias(bias, head):
        sink = sink_ref[head] * LOG2_E
        return jnp.concatenate([jnp.where(first_col, sink, bias[:, :BLOCK]), bias[:, BLOCK:]],
                               axis=1)

    for j in range(tm // BLOCK):
        chains, slots = [], []
        rows = slice(j * BLOCK, (j + 1) * BLOCK)
        bias = bias_ref[jnp.minimum(tile, 1)] if j == 0 else bias_ref[1]
        k = k_win_ref[j * BLOCK:(j + 2) * BLOCK]
        v = v_win_ref[j * BLOCK:(j + 2) * BLOCK]
        for g in range(W_A // PAIR):
            kv = (g // groups_per_kv) * PAIR
            chains.append((qa_ref[0, rows, g * PAIR:(g + 1) * PAIR],
                           with_sink_slot(k[:, kv:kv + PAIR]), with_sink_slot(v[:, kv:kv + PAIR]),
                           (sink_bias(bias, 2 * g), sink_bias(bias, 2 * g + 1))))
            slots.append((rows, g * PAIR))
        for g in range(W_C // PAIR):
            c = g * PAIR
            chains.append((qc_ref[0, rows, c:c + PAIR],
                           mk_ref[:, c:c + PAIR], mv_ref[:, c:c + PAIR], None))
            slots.append((rows, W_A + W_B + c))
        for (rows, c), (out, _) in zip(slots, _attend(chains, s_ref, p_ref)):
            y_ref[rows, c:c + PAIR] = out

    def rms_normalize(t):
        return t * lax.rsqrt(jnp.mean(t * t, axis=-1, keepdims=True) + RMS_EPS)

    y = jnp.concatenate([rms_normalize(y_ref[:, 0:W_A]),
                         rms_normalize(ob_ref[0].astype(f32)),
                         rms_normalize(y_ref[:, W_A + W_B:D_MIX])], axis=-1)
    y = (y * gate_ref[0].astype(f32)).astype(bf16)
    r_ref[...] = DEEPNORM_ALPHA * x_ref[0] + jnp.dot(y, w_out_ref[...],
                                                     preferred_element_type=f32)


def _tail_bias():
    qi = np.arange(BLOCK)[:, None]
    kj = np.arange(2 * BLOCK)[None, :]
    dist = qi + BLOCK - kj
    band = (dist >= 0) & (dist <= SWA_WINDOW - 1)
    return np.where(np.stack([band & (kj >= BLOCK), band]), 0.0, MASKED).astype(np.float32)


def _tail(sinks, qa, ka, va, qc, mem, w_mem, ob, gate, x, w_out, ln_gain, ln_bias):
    B, S, _ = x.shape
    M = mem.shape[1]
    tm = ROW_TILE
    blocks_per_tile = tm // BLOCK
    tiles_per_seq = S // tm
    n_tiles = B * tiles_per_seq
    bias = jnp.asarray(_tail_bias())

    def seq_tile(step):
        t = jnp.minimum(step, n_tiles - 1)
        return t // tiles_per_seq, t % tiles_per_seq

    def out_tile(step):
        t = jnp.maximum(step - 1, 0)
        return t // tiles_per_seq, t % tiles_per_seq

    row_block = lambda width: pl.BlockSpec((1, tm, width), lambda s: (*seq_tile(s), 0))
    prev_block = lambda width: pl.BlockSpec(
        (1, BLOCK, width),
        lambda s: (seq_tile(s)[0], jnp.maximum(seq_tile(s)[1] * blocks_per_tile - 1, 0), 0))
    const = lambda shape: pl.BlockSpec(shape, lambda s: (0,) * len(shape))
    kv_w = 2 * W_KV_A
    stage = ((W_A + W_C) // PAIR, 2 * BLOCK, 2 * BLOCK)
    pipelined = (_nbytes((tm, W_A + 2 * kv_w + W_C + W_B + D_MIX), bf16)
                 + 2 * _nbytes((BLOCK, kv_w), bf16) + _nbytes((M, D_MODEL), f32)
                 + _nbytes((D_MODEL, 2 * W_C), bf16) + 2 * _nbytes((tm, D_MODEL), f32)
                 + _nbytes((D_MIX, D_MODEL), bf16) + _nbytes(bias.shape, f32))
    scratch = [pltpu.VMEM((BLOCK + tm, kv_w), bf16), pltpu.VMEM((BLOCK + tm, kv_w), bf16),
               pltpu.VMEM((M, W_C), bf16), pltpu.VMEM((M, W_C), bf16),
               pltpu.VMEM((tm, D_MIX), f32), pltpu.VMEM(stage, f32), pltpu.VMEM(stage, bf16),
               pltpu.VMEM((tm, D_MODEL), f32)]
    scratch_bytes = (2 * _nbytes((BLOCK + tm, kv_w), bf16) + 2 * _nbytes((M, W_C), bf16)
                     + 5 * _nbytes((tm, D_MIX), f32) + _nbytes(stage, f32) + _nbytes(stage, bf16))
    return pl.pallas_call(
        functools.partial(_tail_kernel, tiles_per_seq=tiles_per_seq, n_tiles=n_tiles),
        grid=(n_tiles + 1,),
        in_specs=[pl.BlockSpec(memory_space=pltpu.SMEM),
                  row_block(W_A), row_block(kv_w), prev_block(kv_w),
                  row_block(kv_w), prev_block(kv_w), row_block(W_C),
                  pl.BlockSpec((1, M, D_MODEL), lambda s: (seq_tile(s)[0], 0, 0)),
                  const((D_MODEL, 2 * W_C)), row_block(W_B), row_block(D_MIX),
                  row_block(D_MODEL), const((D_MIX, D_MODEL)),
                  const((1, D_MODEL)), const((1, D_MODEL)), const(bias.shape)],
        out_specs=pl.BlockSpec((1, tm, D_MODEL), lambda s: (*out_tile(s), 0)),
        out_shape=jax.ShapeDtypeStruct((B, S, D_MODEL), f32),
        scratch_shapes=scratch,
        compiler_params=pltpu.CompilerParams(
            dimension_semantics=("arbitrary",),
            vmem_limit_bytes=_vmem_limit(pipelined, scratch_bytes)),
        name="mix_out",
    )(sinks, qa, ka, ka, va, va, qc, mem, w_mem, ob, gate, x, w_out, ln_gain, ln_bias, bias)


def _rope_tables(seq_len):
    pos = np.arange(seq_len, dtype=np.float64)
    inv = ROPE_THETA ** (-np.arange(0, HEAD_DIM, 2, dtype=np.float64) / HEAD_DIM)
    ang = pos[:, None] * inv[None, :]
    cos, sin = np.cos(ang), np.sin(ang)
    cos2 = np.concatenate([cos, cos, cos, cos], axis=-1)
    sin2 = np.concatenate([-sin, sin, -sin, sin], axis=-1)
    return jnp.asarray(cos2, f32), jnp.asarray(sin2, f32)


def kernel(x, mem, w_in, b_in, w_mem, attn_sinks, g_branch, w_out, ln_gain, ln_bias):
    assert w_in.shape[0] == DEPTH
    S = x.shape[1]
    cos2, sin2 = _rope_tables(S)
    for l in range(DEPTH):
        qa, ka, va, qc, gate, *views_b = _in_proj(
            x, w_in[l].astype(bf16), b_in[l][None, :], cos2, sin2, g_branch[l][None, :])
        ob = _dilated(views_b)
        x = _tail(attn_sinks[l], qa, ka, va, qc, mem, w_mem[l].astype(bf16), ob, gate, x,
                  w_out[l].astype(bf16), ln_gain[l][None, :], ln_bias[l][None, :])
    return x
```

```python
import functools
import math

import numpy as np
import jax
import jax.numpy as jnp
from jax import lax
from jax.experimental import pallas as pl
from jax.experimental.pallas import tpu as pltpu

D_MODEL = 1024
HEAD_DIM = 64
SWA_Q_HEADS = 8
SWA_KV_HEADS = 2
SWA_WINDOW = 128
DIL_HEADS = 4
DIL_PAIRS = ((128, 1), (512, 4), (2048, 16))
MEM_HEADS = 4
BLOCK = 128
ROPE_THETA = 10000.0
LN_EPS = 1e-5
RMS_EPS = 1e-6
DEPTH = 1
DEEPNORM_ALPHA = (2 * DEPTH) ** 0.25

W_A = SWA_Q_HEADS * HEAD_DIM
W_KV_A = SWA_KV_HEADS * HEAD_DIM
W_B = DIL_HEADS * HEAD_DIM
W_C = MEM_HEADS * HEAD_DIM
D_MIX = W_A + W_B + W_C
D_IN = W_A + 2 * W_KV_A + 3 * W_B + W_C + D_MIX

V7X_LANES = 128
V7X_VMEM_BYTES = 64 * 2**20
VMEM_HEADROOM_BYTES = 8 * 2**20

PAIR = 2 * HEAD_DIM
assert PAIR == V7X_LANES
LOG2_E = math.log2(math.e)
Q_SCALE = HEAD_DIM ** -0.5 * LOG2_E
MASKED = -1e30

OFF_QA = 0
OFF_KA = OFF_QA + W_A
OFF_VA = OFF_KA + W_KV_A
OFF_QB = OFF_VA + W_KV_A
OFF_KB = OFF_QB + W_B
OFF_VB = OFF_KB + W_B
OFF_QC = OFF_VB + W_B
OFF_Z = OFF_QC + W_C
assert OFF_Z + D_MIX == D_IN

ACT_GATE = 0
ACT_QA = ACT_GATE + D_MIX
ACT_KA = ACT_QA + W_A
ACT_VA = ACT_KA + 2 * W_KV_A
ACT_QC = ACT_VA + 2 * W_KV_A
ACT_WIDTH = ACT_QC + W_C
B_Q, B_K, B_V, B_WIDTH = 0, W_B, 2 * W_B, 3 * W_B

IN_ROW_TILE = 1024
ROW_TILE = 512

f32 = jnp.float32
bf16 = jnp.bfloat16


def _vmem_limit(pipelined_bytes, scratch_bytes=0):
    budget = V7X_VMEM_BYTES - VMEM_HEADROOM_BYTES
    assert 2 * pipelined_bytes + scratch_bytes + VMEM_HEADROOM_BYTES <= budget
    return budget


def _nbytes(shape, dtype):
    return int(np.prod(shape)) * jnp.dtype(dtype).itemsize


def _lane_is_first_head():
    return lax.broadcasted_iota(jnp.int32, (1, V7X_LANES), 1) < HEAD_DIM


def _in_proj_kernel(x_ref, w_ref, b_ref, cos_ref, sin_ref, g_ref,
                    act_ref, b1_ref, b4_ref, b16_ref, *stage_refs):
    tm = x_ref.shape[1]
    xb = x_ref[0].astype(bf16)
    cos = cos_ref[...]
    sin = sin_ref[...]
    lane = lax.broadcasted_iota(jnp.int32, (1, V7X_LANES), 1)
    first_half = (lane % HEAD_DIM) < (HEAD_DIM // 2)
    first_head = lane < HEAD_DIM

    def proj(off, width):
        acc = jnp.dot(xb, w_ref[:, off:off + width], preferred_element_type=f32)
        return acc + b_ref[:, off:off + width]

    def rope(a):
        rot = jnp.where(first_half,
                        pltpu.roll(a, V7X_LANES - HEAD_DIM // 2, 1),
                        pltpu.roll(a, HEAD_DIM // 2, 1))
        return a * cos + rot * sin

    def groups(a):
        return [a[:, i:i + V7X_LANES] for i in range(0, a.shape[1], V7X_LANES)]

    def both_halves(a):
        swapped = pltpu.roll(a, HEAD_DIM, 1)
        return jnp.where(first_head, a, swapped), jnp.where(first_head, swapped, a)

    for c in range(0, D_MIX, 2 * V7X_LANES):
        z = proj(OFF_Z + c, 2 * V7X_LANES)
        gate = z * jax.nn.sigmoid(z) * g_ref[:, c:c + 2 * V7X_LANES]
        act_ref[0, :, ACT_GATE + c:ACT_GATE + c + 2 * V7X_LANES] = gate.astype(bf16)
    for c in range(0, W_A, 2 * V7X_LANES):
        for i, a in enumerate(groups(proj(OFF_QA + c, 2 * V7X_LANES))):
            lo = ACT_QA + c + i * V7X_LANES
            act_ref[0, :, lo:lo + V7X_LANES] = (rope(a) * Q_SCALE).astype(bf16)
    k_a, v_a = groups(proj(OFF_KA, 2 * W_KV_A))
    for lo, a in ((ACT_KA, rope(k_a)), (ACT_VA, v_a)):
        kv0, kv1 = both_halves(a)
        act_ref[0, :, lo:lo + V7X_LANES] = kv0.astype(bf16)
        act_ref[0, :, lo + V7X_LANES:lo + 2 * V7X_LANES] = kv1.astype(bf16)
    mixer_b = ((OFF_QB, lambda a: rope(a) * Q_SCALE, B_Q), (OFF_KB, rope, B_K),
               (OFF_VB, lambda a: a, B_V))
    dil4, dil16 = DIL_PAIRS[1][1], DIL_PAIRS[2][1]
    for t, (off, fn, lane0) in enumerate(mixer_b):
        for g, a in enumerate(groups(proj(off, W_B))):
            a = fn(a)
            lanes = slice(lane0 + g * PAIR, lane0 + (g + 1) * PAIR)
            b1_ref[0, :, lanes] = a.astype(bf16)
            slab = t * (W_B // PAIR) + g
            stage_ref, stage4_ref = stage_refs[2 * slab], stage_refs[2 * slab + 1]
            stage_ref[...] = a
            rows4, rows16 = tm // dil4, tm // dil16
            for j4 in range(dil4):
                stream = stage_ref[pl.ds(j4, rows4, stride=dil4), :]
                b4_ref[0, j4, :, lanes] = stream.astype(bf16)
                stage4_ref[j4 * rows4:(j4 + 1) * rows4] = stream
            for j16 in range(dil16):
                j4, phase = j16 % dil4, j16 // dil4
                stream = stage4_ref[pl.ds(j4 * rows4 + phase, rows16, stride=dil16 // dil4), :]
                b16_ref[0, j16, :, lanes] = stream.astype(bf16)
    act_ref[0, :, ACT_QC:ACT_QC + W_C] = (proj(OFF_QC, W_C) * Q_SCALE).astype(bf16)


def _in_proj(x, w_in, b_in, cos2, sin2, g_branch):
    B, S, _ = x.shape
    tm = IN_ROW_TILE
    row_block = lambda width: pl.BlockSpec((1, tm, width), lambda b, i: (b, i, 0))
    out_specs = [row_block(ACT_WIDTH), row_block(B_WIDTH)]
    out_shape = [jax.ShapeDtypeStruct((B, S, ACT_WIDTH), bf16),
                 jax.ShapeDtypeStruct((B, S, B_WIDTH), bf16)]
    for _, r in DIL_PAIRS[1:]:
        out_specs.append(pl.BlockSpec((1, r, tm // r, B_WIDTH), lambda b, i: (b, 0, i, 0)))
        out_shape.append(jax.ShapeDtypeStruct((B, r, S // r, B_WIDTH), bf16))
    const = lambda shape: pl.BlockSpec(shape, lambda b, i: (0,) * len(shape))
    n_stage, stage = 2 * 3 * (W_B // PAIR), (tm, V7X_LANES)
    pipelined = (_nbytes((tm, D_MODEL), f32) + _nbytes((D_MODEL, D_IN), bf16)
                 + _nbytes((tm, ACT_WIDTH + 3 * B_WIDTH), bf16) + 2 * _nbytes((tm, V7X_LANES), f32))
    return pl.pallas_call(
        _in_proj_kernel,
        grid=(B, S // tm),
        in_specs=[row_block(D_MODEL), const((D_MODEL, D_IN)), const((1, D_IN)),
                  pl.BlockSpec((tm, V7X_LANES), lambda b, i: (i, 0)),
                  pl.BlockSpec((tm, V7X_LANES), lambda b, i: (i, 0)),
                  const((1, D_MIX))],
        out_specs=out_specs,
        out_shape=out_shape,
        scratch_shapes=[pltpu.VMEM(stage, f32)] * n_stage,
        compiler_params=pltpu.CompilerParams(
            dimension_semantics=("arbitrary", "arbitrary"),
            vmem_limit_bytes=_vmem_limit(pipelined, n_stage * _nbytes(stage, f32))),
        name="in_proj",
    )(x, w_in, b_in, cos2, sin2, g_branch)


def _attend(chains, s_ref, p_ref):
    first_head = _lane_is_first_head()
    for c, (q2, k2, _, _) in enumerate(chains):
        zero = jnp.zeros_like(q2)
        stacked = jnp.concatenate([jnp.where(first_head, q2, zero),
                                   jnp.where(first_head, zero, q2)], axis=0)
        s_ref[c, :, 0:k2.shape[0]] = lax.dot_general(
            stacked, k2, (((1,), (1,)), ((), ())), preferred_element_type=f32)
    tops = []
    for c, (q2, k2, _, biases) in enumerate(chains):
        m_rows, n = q2.shape[0], k2.shape[0]
        pair_tops = []
        for h in range(2):
            rows = slice(h * m_rows, (h + 1) * m_rows)
            sh = s_ref[c, rows, 0:n]
            if biases is not None:
                sh = sh + biases[h]
            m = jnp.max(sh, axis=-1, keepdims=True)
            p_ref[c, rows, 0:n] = jnp.exp2(sh - m).astype(bf16)
            pair_tops.append(m)
        tops.append(pair_tops)
    results = []
    for c, (q2, k2, v2, _) in enumerate(chains):
        m_rows, n = q2.shape[0], k2.shape[0]
        v_ones = jnp.concatenate([v2, jnp.ones_like(v2)], axis=1)
        o = jnp.dot(p_ref[c, :, 0:n], v_ones, preferred_element_type=f32)
        acc = jnp.where(first_head, o[:m_rows, :PAIR], o[m_rows:, :PAIR])
        denom = jnp.where(first_head, o[:m_rows, PAIR:], o[m_rows:, PAIR:])
        lse = jnp.where(first_head, tops[c][0], tops[c][1]) + jnp.log(denom) * LOG2_E
        results.append((acc * (1.0 / denom), lse))
    return results


def _window_bias(max_dist):
    qi = np.arange(BLOCK)[:, None]
    kj = np.arange(2 * BLOCK)[None, :]
    dist = np.stack([qi - kj, qi + BLOCK - kj])
    return np.where((dist >= 0) & (dist <= max_dist), 0.0, MASKED).astype(np.float32)


def _window(blk):
    return blk * BLOCK, max(blk - 1, 0) * BLOCK, min(blk, 1)


def _merge(o_a, l_a, o_b, l_b):
    top = jnp.maximum(l_a, l_b)
    e_a = jnp.exp2(l_a - top)
    e_b = jnp.exp2(l_b - top)
    denom = e_a + e_b
    return (e_a * o_a + e_b * o_b) * (1.0 / denom), top + jnp.log(denom) * LOG2_E


def _dilated_kernel(b1_ref, b4_ref, b16_ref, bias_ref, o_ref,
                    o4_ref, l4_ref, o16_ref, l16_ref, s_ref, p_ref):
    seq_len = b1_ref.shape[1]
    n_pairs = W_B // PAIR
    dil4, dil16 = DIL_PAIRS[1][1], DIL_PAIRS[2][1]
    sub = dil16 // dil4
    pair_lanes = [slice(g * PAIR, (g + 1) * PAIR) for g in range(n_pairs)]

    def qkv(ref, index, q_rows, kv_rows, g):
        lanes = [slice(lane0 + g * PAIR, lane0 + (g + 1) * PAIR) for lane0 in (B_Q, B_K, B_V)]
        return (ref[(*index, q_rows, lanes[0])], ref[(*index, kv_rows, lanes[1])],
                ref[(*index, kv_rows, lanes[2])])

    causal = bias_ref[0, :, 0:BLOCK]
    rows = slice(None)
    chains = [(*qkv(b16_ref, (0, j), rows, rows, g), (causal, causal))
              for j in range(dil16) for g in range(n_pairs)]
    for n, (out, lse) in enumerate(_attend(chains, s_ref, p_ref)):
        j, g = n // n_pairs, n % n_pairs
        o16_ref[g, j % dil4, pl.ds(j // dil4, BLOCK, stride=sub), :] = out
        l16_ref[g, j % dil4, pl.ds(j // dil4, BLOCK, stride=sub), :] = lse

    windows = [_window(blk) for blk in range(seq_len // dil4 // BLOCK)]
    chains = [(*qkv(b4_ref, (0, j), pl.ds(u0, BLOCK), pl.ds(k0, 2 * BLOCK), g),
               (bias_ref[first], bias_ref[first]))
              for u0, k0, first in windows for j in range(dil4) for g in range(n_pairs)]
    for n, (out, lse) in enumerate(_attend(chains, s_ref, p_ref)):
        u0 = windows[n // (dil4 * n_pairs)][0]
        j, g = (n // n_pairs) % dil4, n % n_pairs
        out, lse = _merge(out, lse, o16_ref[g, j, pl.ds(u0, BLOCK), :],
                          l16_ref[g, j, pl.ds(u0, BLOCK), :])
        o4_ref[g, pl.ds(u0 * dil4 + j, BLOCK, stride=dil4), :] = out
        l4_ref[g, pl.ds(u0 * dil4 + j, BLOCK, stride=dil4), :] = lse

    windows = [_window(blk) for blk in range(seq_len // BLOCK)]
    chains = [(*qkv(b1_ref, (0,), pl.ds(r0, BLOCK), pl.ds(k0, 2 * BLOCK), g),
               (bias_ref[first], bias_ref[first]))
              for r0, k0, first in windows for g in range(n_pairs)]
    for n, (out, lse) in enumerate(_attend(chains, s_ref, p_ref)):
        r0, g = windows[n // n_pairs][0], n % n_pairs
        out, _ = _merge(out, lse, o4_ref[g, pl.ds(r0, BLOCK), :], l4_ref[g, pl.ds(r0, BLOCK), :])
        o_ref[0, pl.ds(r0, BLOCK), pair_lanes[g]] = out.astype(bf16)


def _dilated(views):
    B, S, _ = views[0].shape
    assert all(w // r == BLOCK for w, r in DIL_PAIRS) and S // DIL_PAIRS[2][1] == BLOCK
    bias = jnp.asarray(_window_bias(BLOCK))
    view_specs = [pl.BlockSpec((1, S, B_WIDTH), lambda b: (b, 0, 0))]
    for _, r in DIL_PAIRS[1:]:
        view_specs.append(pl.BlockSpec((1, r, S // r, B_WIDTH), lambda b: (b, 0, 0, 0)))
    slab = (W_B // PAIR, S, V7X_LANES)
    slab16 = (W_B // PAIR, DIL_PAIRS[1][1], S // DIL_PAIRS[1][1], V7X_LANES)
    stage = (S // BLOCK * (W_B // PAIR), 2 * BLOCK, 2 * BLOCK)
    pipelined = 10 * _nbytes((S, W_B), bf16) + _nbytes(bias.shape, f32)
    return pl.pallas_call(
        _dilated_kernel,
        grid=(B,),
        in_specs=view_specs + [pl.BlockSpec(bias.shape, lambda b: (0, 0, 0))],
        out_specs=pl.BlockSpec((1, S, W_B), lambda b: (b, 0, 0)),
        out_shape=jax.ShapeDtypeStruct((B, S, W_B), bf16),
        scratch_shapes=[pltpu.VMEM(slab, f32), pltpu.VMEM(slab, f32),
                        pltpu.VMEM(slab16, f32), pltpu.VMEM(slab16, f32),
                        pltpu.VMEM(stage, f32), pltpu.VMEM(stage, bf16)],
        compiler_params=pltpu.CompilerParams(
            dimension_semantics=("arbitrary",),
            vmem_limit_bytes=_vmem_limit(
                pipelined, 4 * _nbytes(slab, f32) + _nbytes(stage, f32) + _nbytes(stage, bf16))),
        name="dilated_attention",
    )(*views, bias)


def _tail_kernel(sink_ref, act_ref, kv_prev_ref, mem_ref,
                 w_mem_ref, ob_ref, x_ref, w_out_ref, gain_ref, ln_bias_ref, bias_ref,
                 out_ref, k_win_ref, v_win_ref, mk_ref, mv_ref, y_ref, s_ref, p_ref, r_ref,
                 *, tiles_per_seq, n_tiles):
    step = pl.program_id(0)
    tile = lax.rem(jnp.minimum(step, n_tiles - 1), tiles_per_seq)
    tm = x_ref.shape[1]
    groups_per_kv = (SWA_Q_HEADS // SWA_KV_HEADS) // 2

    @pl.when(step == 0)
    def _():
        r_ref[...] = jnp.zeros_like(r_ref)

    @pl.when(tile == 0)
    def _():
        mkv = jnp.dot(mem_ref[0].astype(bf16), w_mem_ref[...], preferred_element_type=f32)
        mk_ref[...] = mkv[:, :W_C].astype(bf16)
        mv_ref[...] = mkv[:, W_C:].astype(bf16)

    r = r_ref[...]
    mu = jnp.mean(r, axis=-1, keepdims=True)
    d = r - mu
    var = jnp.mean(d * d, axis=-1, keepdims=True)
    out_ref[0] = d * lax.rsqrt(var + LN_EPS) * gain_ref[...] + ln_bias_ref[...]

    kv_w = 2 * W_KV_A
    for win_ref, lane0 in ((k_win_ref, ACT_KA), (v_win_ref, ACT_VA)):
        win_ref[0:BLOCK] = kv_prev_ref[0, :, lane0 - ACT_KA:lane0 - ACT_KA + kv_w]
        win_ref[BLOCK:BLOCK + tm] = act_ref[0, :, lane0:lane0 + kv_w]

    first_row = lax.broadcasted_iota(jnp.int32, (BLOCK, 1), 0) == 0
    first_col = lax.broadcasted_iota(jnp.int32, (1, BLOCK), 1) == 0

    def with_sink_slot(kv):
        return jnp.concatenate([jnp.where(first_row, jnp.zeros_like(kv[:BLOCK]), kv[:BLOCK]),
                                kv[BLOCK:]], axis=0)

    def sink_bias(bias, head):
        sink = sink_ref[head] * LOG2_E
        return jnp.concatenate([jnp.where(first_col, sink, bias[:, :BLOCK]), bias[:, BLOCK:]],
                               axis=1)

    for j in range(tm // BLOCK):
        chains, slots = [], []
        rows = slice(j * BLOCK, (j + 1) * BLOCK)
        bias = bias_ref[jnp.minimum(tile, 1)] if j == 0 else bias_ref[1]
        k = k_win_ref[j * BLOCK:(j + 2) * BLOCK]
        v = v_win_ref[j * BLOCK:(j + 2) * BLOCK]
        for g in range(W_A // PAIR):
            kv = (g // groups_per_kv) * PAIR
            chains.append((act_ref[0, rows, ACT_QA + g * PAIR:ACT_QA + (g + 1) * PAIR],
                           with_sink_slot(k[:, kv:kv + PAIR]), with_sink_slot(v[:, kv:kv + PAIR]),
                           (sink_bias(bias, 2 * g), sink_bias(bias, 2 * g + 1))))
            slots.append((rows, g * PAIR))
        for g in range(W_C // PAIR):
            c = g * PAIR
            chains.append((act_ref[0, rows, ACT_QC + c:ACT_QC + c + PAIR],
                           mk_ref[:, c:c + PAIR], mv_ref[:, c:c + PAIR], None))
            slots.append((rows, W_A + W_B + c))
        for (rows, c), (out, _) in zip(slots, _attend(chains, s_ref, p_ref)):
            y_ref[rows, c:c + PAIR] = out

    def rms_normalize(t):
        return t * lax.rsqrt(jnp.mean(t * t, axis=-1, keepdims=True) + RMS_EPS)

    y = jnp.concatenate([rms_normalize(y_ref[:, 0:W_A]),
                         rms_normalize(ob_ref[0].astype(f32)),
                         rms_normalize(y_ref[:, W_A + W_B:D_MIX])], axis=-1)
    y = (y * act_ref[0, :, ACT_GATE:ACT_GATE + D_MIX].astype(f32)).astype(bf16)
    r_ref[...] = DEEPNORM_ALPHA * x_ref[0] + jnp.dot(y, w_out_ref[...],
                                                     preferred_element_type=f32)


def _tail_bias():
    qi = np.arange(BLOCK)[:, None]
    kj = np.arange(2 * BLOCK)[None, :]
    dist = qi + BLOCK - kj
    band = (dist >= 0) & (dist <= SWA_WINDOW - 1)
    return np.where(np.stack([band & (kj >= BLOCK), band]), 0.0, MASKED).astype(np.float32)


def _tail(sinks, act, mem, w_mem, ob, x, w_out, ln_gain, ln_bias):
    B, S, _ = x.shape
    M = mem.shape[1]
    tm = ROW_TILE
    blocks_per_tile = tm // BLOCK
    tiles_per_seq = S // tm
    n_tiles = B * tiles_per_seq
    bias = jnp.asarray(_tail_bias())

    def seq_tile(step):
        t = jnp.minimum(step, n_tiles - 1)
        return t // tiles_per_seq, t % tiles_per_seq

    def out_tile(step):
        t = jnp.maximum(step - 1, 0)
        return t // tiles_per_seq, t % tiles_per_seq

    row_block = lambda width: pl.BlockSpec((1, tm, width), lambda s: (*seq_tile(s), 0))
    kv_w = 2 * W_KV_A
    assert ACT_VA == ACT_KA + kv_w and ACT_KA % (2 * kv_w) == 0
    kv_prev_block = pl.BlockSpec(
        (1, BLOCK, 2 * kv_w),
        lambda s: (seq_tile(s)[0], jnp.maximum(seq_tile(s)[1] * blocks_per_tile - 1, 0),
                   ACT_KA // (2 * kv_w)))
    const = lambda shape: pl.BlockSpec(shape, lambda s: (0,) * len(shape))
    stage = ((W_A + W_C) // PAIR, 2 * BLOCK, 2 * BLOCK)
    pipelined = (_nbytes((tm, ACT_WIDTH + W_B), bf16)
                 + 2 * _nbytes((BLOCK, kv_w), bf16) + _nbytes((M, D_MODEL), f32)
                 + _nbytes((D_MODEL, 2 * W_C), bf16) + 2 * _nbytes((tm, D_MODEL), f32)
                 + _nbytes((D_MIX, D_MODEL), bf16) + _nbytes(bias.shape, f32))
    scratch = [pltpu.VMEM((BLOCK + tm, kv_w), bf16), pltpu.VMEM((BLOCK + tm, kv_w), bf16),
               pltpu.VMEM((M, W_C), bf16), pltpu.VMEM((M, W_C), bf16),
               pltpu.VMEM((tm, D_MIX), f32), pltpu.VMEM(stage, f32), pltpu.VMEM(stage, bf16),
               pltpu.VMEM((tm, D_MODEL), f32)]
    scratch_bytes = (2 * _nbytes((BLOCK + tm, kv_w), bf16) + 2 * _nbytes((M, W_C), bf16)
                     + 5 * _nbytes((tm, D_MIX), f32) + _nbytes(stage, f32) + _nbytes(stage, bf16))
    return pl.pallas_call(
        functools.partial(_tail_kernel, tiles_per_seq=tiles_per_seq, n_tiles=n_tiles),
        grid=(n_tiles + 1,),
        in_specs=[pl.BlockSpec(memory_space=pltpu.SMEM),
                  row_block(ACT_WIDTH), kv_prev_block,
                  pl.BlockSpec((1, M, D_MODEL), lambda s: (seq_tile(s)[0], 0, 0)),
                  const((D_MODEL, 2 * W_C)), row_block(W_B),
                  row_block(D_MODEL), const((D_MIX, D_MODEL)),
                  const((1, D_MODEL)), const((1, D_MODEL)), const(bias.shape)],
        out_specs=pl.BlockSpec((1, tm, D_MODEL), lambda s: (*out_tile(s), 0)),
        out_shape=jax.ShapeDtypeStruct((B, S, D_MODEL), f32),
        scratch_shapes=scratch,
        compiler_params=pltpu.CompilerParams(
            dimension_semantics=("arbitrary",),
            vmem_limit_bytes=_vmem_limit(pipelined, scratch_bytes)),
        name="mix_out",
    )(sinks, act, act, mem, w_mem, ob, x, w_out, ln_gain, ln_bias, bias)


def _rope_tables(seq_len):
    pos = np.arange(seq_len, dtype=np.float64)
    inv = ROPE_THETA ** (-np.arange(0, HEAD_DIM, 2, dtype=np.float64) / HEAD_DIM)
    ang = pos[:, None] * inv[None, :]
    cos, sin = np.cos(ang), np.sin(ang)
    cos2 = np.concatenate([cos, cos, cos, cos], axis=-1)
    sin2 = np.concatenate([-sin, sin, -sin, sin], axis=-1)
    return jnp.asarray(cos2, f32), jnp.asarray(sin2, f32)


def kernel(x, mem, w_in, b_in, w_mem, attn_sinks, g_branch, w_out, ln_gain, ln_bias):
    assert w_in.shape[0] == DEPTH
    S = x.shape[1]
    cos2, sin2 = _rope_tables(S)
    for l in range(DEPTH):
        act, *views_b = _in_proj(
            x, w_in[l].astype(bf16), b_in[l][None, :], cos2, sin2, g_branch[l][None, :])
        ob = _dilated(views_b)
        x = _tail(attn_sinks[l], act, mem, w_mem[l].astype(bf16), ob, x,
                  w_out[l].astype(bf16), ln_gain[l][None, :], ln_bias[l][None, :])
    return x
```

```python
import functools
import math

import numpy as np
import jax
import jax.numpy as jnp
from jax import lax
from jax.experimental import pallas as pl
from jax.experimental.pallas import tpu as pltpu

D_MODEL = 1024
HEAD_DIM = 64
SWA_Q_HEADS = 8
SWA_KV_HEADS = 2
SWA_WINDOW = 128
DIL_HEADS = 4
DIL_PAIRS = ((128, 1), (512, 4), (2048, 16))
MEM_HEADS = 4
BLOCK = 128
ROPE_THETA = 10000.0
LN_EPS = 1e-5
RMS_EPS = 1e-6
DEPTH = 1
DEEPNORM_ALPHA = (2 * DEPTH) ** 0.25

W_A = SWA_Q_HEADS * HEAD_DIM
W_KV_A = SWA_KV_HEADS * HEAD_DIM
W_B = DIL_HEADS * HEAD_DIM
W_C = MEM_HEADS * HEAD_DIM
D_MIX = W_A + W_B + W_C
D_IN = W_A + 2 * W_KV_A + 3 * W_B + W_C + D_MIX

V7X_LANES = 128
V7X_VMEM_BYTES = 64 * 2**20
VMEM_HEADROOM_BYTES = 8 * 2**20

PAIR = 2 * HEAD_DIM
assert PAIR == V7X_LANES
LOG2_E = math.log2(math.e)
Q_SCALE = HEAD_DIM ** -0.5 * LOG2_E
MASKED = -1e30

OFF_QA = 0
OFF_KA = OFF_QA + W_A
OFF_VA = OFF_KA + W_KV_A
OFF_QB = OFF_VA + W_KV_A
OFF_KB = OFF_QB + W_B
OFF_VB = OFF_KB + W_B
OFF_QC = OFF_VB + W_B
OFF_Z = OFF_QC + W_C
assert OFF_Z + D_MIX == D_IN

ACT_GATE = 0
ACT_QA = ACT_GATE + D_MIX
ACT_KA = ACT_QA + W_A
ACT_VA = ACT_KA + 2 * W_KV_A
ACT_QC = ACT_VA + 2 * W_KV_A
ACT_WIDTH = ACT_QC + W_C
B_Q, B_K, B_V, B_WIDTH = 0, W_B, 2 * W_B, 3 * W_B

IN_ROW_TILE = 1024
ROW_TILE = 512

f32 = jnp.float32
bf16 = jnp.bfloat16


def _vmem_limit(pipelined_bytes, scratch_bytes=0):
    budget = V7X_VMEM_BYTES - VMEM_HEADROOM_BYTES
    assert 2 * pipelined_bytes + scratch_bytes + VMEM_HEADROOM_BYTES <= budget
    return budget


def _nbytes(shape, dtype):
    return int(np.prod(shape)) * jnp.dtype(dtype).itemsize


def _lane_is_first_head():
    return lax.broadcasted_iota(jnp.int32, (1, V7X_LANES), 1) < HEAD_DIM


def _in_proj_kernel(x_ref, w_ref, b_ref, cos_ref, sin_ref, g_ref,
                    act_ref, b1_ref, b4_ref, b16_ref, *stage_refs):
    tm = x_ref.shape[1]
    xb = x_ref[0].astype(bf16)
    cos = cos_ref[...]
    sin = sin_ref[...]
    lane = lax.broadcasted_iota(jnp.int32, (1, V7X_LANES), 1)
    first_half = (lane % HEAD_DIM) < (HEAD_DIM // 2)
    first_head = lane < HEAD_DIM

    def proj(off, width):
        acc = jnp.dot(xb, w_ref[:, off:off + width], preferred_element_type=f32)
        return acc + b_ref[:, off:off + width]

    def rope(a):
        rot = jnp.where(first_half,
                        pltpu.roll(a, V7X_LANES - HEAD_DIM // 2, 1),
                        pltpu.roll(a, HEAD_DIM // 2, 1))
        return a * cos + rot * sin

    def groups(a):
        return [a[:, i:i + V7X_LANES] for i in range(0, a.shape[1], V7X_LANES)]

    def both_halves(a):
        swapped = pltpu.roll(a, HEAD_DIM, 1)
        return jnp.where(first_head, a, swapped), jnp.where(first_head, swapped, a)

    for c in range(0, D_MIX, 2 * V7X_LANES):
        z = proj(OFF_Z + c, 2 * V7X_LANES)
        gate = z * jax.nn.sigmoid(z) * g_ref[:, c:c + 2 * V7X_LANES]
        act_ref[0, :, ACT_GATE + c:ACT_GATE + c + 2 * V7X_LANES] = gate.astype(bf16)
    for c in range(0, W_A, 2 * V7X_LANES):
        for i, a in enumerate(groups(proj(OFF_QA + c, 2 * V7X_LANES))):
            lo = ACT_QA + c + i * V7X_LANES
            act_ref[0, :, lo:lo + V7X_LANES] = (rope(a) * Q_SCALE).astype(bf16)
    k_a, v_a = groups(proj(OFF_KA, 2 * W_KV_A))
    for lo, a in ((ACT_KA, rope(k_a)), (ACT_VA, v_a)):
        kv0, kv1 = both_halves(a)
        act_ref[0, :, lo:lo + V7X_LANES] = kv0.astype(bf16)
        act_ref[0, :, lo + V7X_LANES:lo + 2 * V7X_LANES] = kv1.astype(bf16)
    mixer_b = ((OFF_QB, lambda a: rope(a) * Q_SCALE, B_Q), (OFF_KB, rope, B_K),
               (OFF_VB, lambda a: a, B_V))
    dil4, dil16 = DIL_PAIRS[1][1], DIL_PAIRS[2][1]
    for t, (off, fn, lane0) in enumerate(mixer_b):
        for g, a in enumerate(groups(proj(off, W_B))):
            a = fn(a)
            lanes = slice(lane0 + g * PAIR, lane0 + (g + 1) * PAIR)
            b1_ref[0, :, lanes] = a.astype(bf16)
            slab = t * (W_B // PAIR) + g
            stage_ref, stage4_ref = stage_refs[2 * slab], stage_refs[2 * slab + 1]
            stage_ref[...] = a
            rows4, rows16 = tm // dil4, tm // dil16
            for j4 in range(dil4):
                stream = stage_ref[pl.ds(j4, rows4, stride=dil4), :]
                b4_ref[0, j4, :, lanes] = stream.astype(bf16)
                stage4_ref[j4 * rows4:(j4 + 1) * rows4] = stream
            for j16 in range(dil16):
                j4, phase = j16 % dil4, j16 // dil4
                stream = stage4_ref[pl.ds(j4 * rows4 + phase, rows16, stride=dil16 // dil4), :]
                b16_ref[0, j16, :, lanes] = stream.astype(bf16)
    act_ref[0, :, ACT_QC:ACT_QC + W_C] = (proj(OFF_QC, W_C) * Q_SCALE).astype(bf16)


def _in_proj(x, w_in, b_in, cos2, sin2, g_branch):
    B, S, _ = x.shape
    tm = IN_ROW_TILE
    row_block = lambda width: pl.BlockSpec((1, tm, width), lambda b, i: (b, i, 0))
    out_specs = [row_block(ACT_WIDTH), row_block(B_WIDTH)]
    out_shape = [jax.ShapeDtypeStruct((B, S, ACT_WIDTH), bf16),
                 jax.ShapeDtypeStruct((B, S, B_WIDTH), bf16)]
    for _, r in DIL_PAIRS[1:]:
        out_specs.append(pl.BlockSpec((1, r, tm // r, B_WIDTH), lambda b, i: (b, 0, i, 0)))
        out_shape.append(jax.ShapeDtypeStruct((B, r, S // r, B_WIDTH), bf16))
    const = lambda shape: pl.BlockSpec(shape, lambda b, i: (0,) * len(shape))
    n_stage, stage = 2 * 3 * (W_B // PAIR), (tm, V7X_LANES)
    pipelined = (_nbytes((tm, D_MODEL), f32) + _nbytes((D_MODEL, D_IN), bf16)
                 + _nbytes((tm, ACT_WIDTH + 3 * B_WIDTH), bf16) + 2 * _nbytes((tm, V7X_LANES), f32))
    return pl.pallas_call(
        _in_proj_kernel,
        grid=(B, S // tm),
        in_specs=[row_block(D_MODEL), const((D_MODEL, D_IN)), const((1, D_IN)),
                  pl.BlockSpec((tm, V7X_LANES), lambda b, i: (i, 0)),
                  pl.BlockSpec((tm, V7X_LANES), lambda b, i: (i, 0)),
                  const((1, D_MIX))],
        out_specs=out_specs,
        out_shape=out_shape,
        scratch_shapes=[pltpu.VMEM(stage, f32)] * n_stage,
        compiler_params=pltpu.CompilerParams(
            dimension_semantics=("arbitrary", "arbitrary"),
            vmem_limit_bytes=_vmem_limit(pipelined, n_stage * _nbytes(stage, f32))),
        name="in_proj",
    )(x, w_in, b_in, cos2, sin2, g_branch)


def _attend(chains, s_ref, p_ref):
    first_head = _lane_is_first_head()
    for c, (q2, k2, _, _) in enumerate(chains):
        zero = jnp.zeros_like(q2)
        stacked = jnp.concatenate([jnp.where(first_head, q2, zero),
                                   jnp.where(first_head, zero, q2)], axis=0)
        s_ref[c, :, 0:k2.shape[0]] = lax.dot_general(
            stacked, k2, (((1,), (1,)), ((), ())), preferred_element_type=f32)
    tops = []
    for c, (q2, k2, _, biases) in enumerate(chains):
        m_rows, n = q2.shape[0], k2.shape[0]
        pair_tops = []
        for h in range(2):
            rows = slice(h * m_rows, (h + 1) * m_rows)
            sh = s_ref[c, rows, 0:n]
            if biases is not None:
                sh = sh + biases[h]
            m = jnp.max(sh, axis=-1, keepdims=True)
            p_ref[c, rows, 0:n] = jnp.exp2((sh - m).astype(bf16))
            pair_tops.append(m)
        tops.append(pair_tops)
    results = []
    for c, (q2, k2, v2, _) in enumerate(chains):
        m_rows, n = q2.shape[0], k2.shape[0]
        v_ones = jnp.concatenate([v2, jnp.ones_like(v2)], axis=1)
        o = jnp.dot(p_ref[c, :, 0:n], v_ones, preferred_element_type=f32)
        acc = jnp.where(first_head, o[:m_rows, :PAIR], o[m_rows:, :PAIR])
        denom = jnp.where(first_head, o[:m_rows, PAIR:], o[m_rows:, PAIR:])
        lse = jnp.where(first_head, tops[c][0], tops[c][1]) + jnp.log(denom) * LOG2_E
        results.append((acc * (1.0 / denom), lse))
    return results


def _window_bias(max_dist):
    qi = np.arange(BLOCK)[:, None]
    kj = np.arange(2 * BLOCK)[None, :]
    dist = np.stack([qi - kj, qi + BLOCK - kj])
    return np.where((dist >= 0) & (dist <= max_dist), 0.0, MASKED).astype(np.float32)


def _window(blk):
    return blk * BLOCK, max(blk - 1, 0) * BLOCK, min(blk, 1)


def _merge(o_a, l_a, o_b, l_b):
    top = jnp.maximum(l_a, l_b)
    e_a = jnp.exp2(l_a - top)
    e_b = jnp.exp2(l_b - top)
    denom = e_a + e_b
    return (e_a * o_a + e_b * o_b) * (1.0 / denom), top + jnp.log(denom) * LOG2_E


def _dilated_kernel(b1_ref, b4_ref, b16_ref, bias_ref, o_ref,
                    o4_ref, l4_ref, o16_ref, l16_ref, s_ref, p_ref):
    seq_len = b1_ref.shape[1]
    n_pairs = W_B // PAIR
    dil4, dil16 = DIL_PAIRS[1][1], DIL_PAIRS[2][1]
    sub = dil16 // dil4
    pair_lanes = [slice(g * PAIR, (g + 1) * PAIR) for g in range(n_pairs)]

    def qkv(ref, index, q_rows, kv_rows, g):
        lanes = [slice(lane0 + g * PAIR, lane0 + (g + 1) * PAIR) for lane0 in (B_Q, B_K, B_V)]
        return (ref[(*index, q_rows, lanes[0])], ref[(*index, kv_rows, lanes[1])],
                ref[(*index, kv_rows, lanes[2])])

    causal = bias_ref[0, :, 0:BLOCK]
    rows = slice(None)
    chains = [(*qkv(b16_ref, (0, j), rows, rows, g), (causal, causal))
              for j in range(dil16) for g in range(n_pairs)]
    for n, (out, lse) in enumerate(_attend(chains, s_ref, p_ref)):
        j, g = n // n_pairs, n % n_pairs
        o16_ref[g, j % dil4, pl.ds(j // dil4, BLOCK, stride=sub), :] = out
        l16_ref[g, j % dil4, pl.ds(j // dil4, BLOCK, stride=sub), :] = lse

    windows = [_window(blk) for blk in range(seq_len // dil4 // BLOCK)]
    chains = [(*qkv(b4_ref, (0, j), pl.ds(u0, BLOCK), pl.ds(k0, 2 * BLOCK), g),
               (bias_ref[first], bias_ref[first]))
              for u0, k0, first in windows for j in range(dil4) for g in range(n_pairs)]
    for n, (out, lse) in enumerate(_attend(chains, s_ref, p_ref)):
        u0 = windows[n // (dil4 * n_pairs)][0]
        j, g = (n // n_pairs) % dil4, n % n_pairs
        out, lse = _merge(out, lse, o16_ref[g, j, pl.ds(u0, BLOCK), :],
                          l16_ref[g, j, pl.ds(u0, BLOCK), :])
        o4_ref[g, pl.ds(u0 * dil4 + j, BLOCK, stride=dil4), :] = out
        l4_ref[g, pl.ds(u0 * dil4 + j, BLOCK, stride=dil4), :] = lse

    windows = [_window(blk) for blk in range(seq_len // BLOCK)]
    chains = [(*qkv(b1_ref, (0,), pl.ds(r0, BLOCK), pl.ds(k0, 2 * BLOCK), g),
               (bias_ref[first], bias_ref[first]))
              for r0, k0, first in windows for g in range(n_pairs)]
    for n, (out, lse) in enumerate(_attend(chains, s_ref, p_ref)):
        r0, g = windows[n // n_pairs][0], n % n_pairs
        out, _ = _merge(out, lse, o4_ref[g, pl.ds(r0, BLOCK), :], l4_ref[g, pl.ds(r0, BLOCK), :])
        o_ref[0, pl.ds(r0, BLOCK), pair_lanes[g]] = out.astype(bf16)


def _dilated(views):
    B, S, _ = views[0].shape
    assert all(w // r == BLOCK for w, r in DIL_PAIRS) and S // DIL_PAIRS[2][1] == BLOCK
    bias = jnp.asarray(_window_bias(BLOCK))
    view_specs = [pl.BlockSpec((1, S, B_WIDTH), lambda b: (b, 0, 0))]
    for _, r in DIL_PAIRS[1:]:
        view_specs.append(pl.BlockSpec((1, r, S // r, B_WIDTH), lambda b: (b, 0, 0, 0)))
    slab = (W_B // PAIR, S, V7X_LANES)
    slab16 = (W_B // PAIR, DIL_PAIRS[1][1], S // DIL_PAIRS[1][1], V7X_LANES)
    stage = (S // BLOCK * (W_B // PAIR), 2 * BLOCK, 2 * BLOCK)
    pipelined = 10 * _nbytes((S, W_B), bf16) + _nbytes(bias.shape, f32)
    return pl.pallas_call(
        _dilated_kernel,
        grid=(B,),
        in_specs=view_specs + [pl.BlockSpec(bias.shape, lambda b: (0, 0, 0))],
        out_specs=pl.BlockSpec((1, S, W_B), lambda b: (b, 0, 0)),
        out_shape=jax.ShapeDtypeStruct((B, S, W_B), bf16),
        scratch_shapes=[pltpu.VMEM(slab, f32), pltpu.VMEM(slab, f32),
                        pltpu.VMEM(slab16, f32), pltpu.VMEM(slab16, f32),
                        pltpu.VMEM(stage, f32), pltpu.VMEM(stage, bf16)],
        compiler_params=pltpu.CompilerParams(
            dimension_semantics=("arbitrary",),
            vmem_limit_bytes=_vmem_limit(
                pipelined, 4 * _nbytes(slab, f32) + _nbytes(stage, f32) + _nbytes(stage, bf16))),
        name="dilated_attention",
    )(*views, bias)


def _tail_kernel(sink_ref, act_ref, kv_prev_ref, mem_ref,
                 w_mem_ref, ob_ref, x_ref, w_out_ref, gain_ref, ln_bias_ref, bias_ref,
                 out_ref, k_win_ref, v_win_ref, mk_ref, mv_ref, y_ref, s_ref, p_ref, r_ref,
                 *, tiles_per_seq, n_tiles):
    step = pl.program_id(0)
    tile = lax.rem(jnp.minimum(step, n_tiles - 1), tiles_per_seq)
    tm = x_ref.shape[1]
    groups_per_kv = (SWA_Q_HEADS // SWA_KV_HEADS) // 2

    @pl.when(step == 0)
    def _():
        r_ref[...] = jnp.zeros_like(r_ref)

    @pl.when(tile == 0)
    def _():
        mkv = jnp.dot(mem_ref[0].astype(bf16), w_mem_ref[...], preferred_element_type=f32)
        mk_ref[...] = mkv[:, :W_C].astype(bf16)
        mv_ref[...] = mkv[:, W_C:].astype(bf16)

    r = r_ref[...]
    mu = jnp.mean(r, axis=-1, keepdims=True)
    d = r - mu
    var = jnp.mean(d * d, axis=-1, keepdims=True)
    out_ref[0] = d * lax.rsqrt(var + LN_EPS) * gain_ref[...] + ln_bias_ref[...]

    kv_w = 2 * W_KV_A
    for win_ref, lane0 in ((k_win_ref, ACT_KA), (v_win_ref, ACT_VA)):
        win_ref[0:BLOCK] = kv_prev_ref[0, :, lane0 - ACT_KA:lane0 - ACT_KA + kv_w]
        win_ref[BLOCK:BLOCK + tm] = act_ref[0, :, lane0:lane0 + kv_w]

    first_row = lax.broadcasted_iota(jnp.int32, (BLOCK, 1), 0) == 0
    first_col = lax.broadcasted_iota(jnp.int32, (1, BLOCK), 1) == 0

    def with_sink_slot(kv):
        return jnp.concatenate([jnp.where(first_row, jnp.zeros_like(kv[:BLOCK]), kv[:BLOCK]),
                                kv[BLOCK:]], axis=0)

    def sink_bias(bias, head):
        sink = sink_ref[head] * LOG2_E
        return jnp.concatenate([jnp.where(first_col, sink, bias[:, :BLOCK]), bias[:, BLOCK:]],
                               axis=1)

    for j in range(tm // BLOCK):
        chains, slots = [], []
        rows = slice(j * BLOCK, (j + 1) * BLOCK)
        bias = bias_ref[jnp.minimum(tile, 1)] if j == 0 else bias_ref[1]
        k = k_win_ref[j * BLOCK:(j + 2) * BLOCK]
        v = v_win_ref[j * BLOCK:(j + 2) * BLOCK]
        for g in range(W_A // PAIR):
            kv = (g // groups_per_kv) * PAIR
            chains.append((act_ref[0, rows, ACT_QA + g * PAIR:ACT_QA + (g + 1) * PAIR],
                           with_sink_slot(k[:, kv:kv + PAIR]), with_sink_slot(v[:, kv:kv + PAIR]),
                           (sink_bias(bias, 2 * g), sink_bias(bias, 2 * g + 1))))
            slots.append((rows, g * PAIR))
        for g in range(W_C // PAIR):
            c = g * PAIR
            chains.append((act_ref[0, rows, ACT_QC + c:ACT_QC + c + PAIR],
                           mk_ref[:, c:c + PAIR], mv_ref[:, c:c + PAIR], None))
            slots.append((rows, W_A + W_B + c))
        for (rows, c), (out, _) in zip(slots, _attend(chains, s_ref, p_ref)):
            y_ref[rows, c:c + PAIR] = out

    def rms_normalize(t):
        return t * lax.rsqrt(jnp.mean(t * t, axis=-1, keepdims=True) + RMS_EPS)

    y = jnp.concatenate([rms_normalize(y_ref[:, 0:W_A]),
                         rms_normalize(ob_ref[0].astype(f32)),
                         rms_normalize(y_ref[:, W_A + W_B:D_MIX])], axis=-1)
    y = (y * act_ref[0, :, ACT_GATE:ACT_GATE + D_MIX].astype(f32)).astype(bf16)
    r_ref[...] = DEEPNORM_ALPHA * x_ref[0] + jnp.dot(y, w_out_ref[...],
                                                     preferred_element_type=f32)


def _tail_bias():
    qi = np.arange(BLOCK)[:, None]
    kj = np.arange(2 * BLOCK)[None, :]
    dist = qi + BLOCK - kj
    band = (dist >= 0) & (dist <= SWA_WINDOW - 1)
    return np.where(np.stack([band & (kj >= BLOCK), band]), 0.0, MASKED).astype(np.float32)


def _tail(sinks, act, mem, w_mem, ob, x, w_out, ln_gain, ln_bias):
    B, S, _ = x.shape
    M = mem.shape[1]
    tm = ROW_TILE
    blocks_per_tile = tm // BLOCK
    tiles_per_seq = S // tm
    n_tiles = B * tiles_per_seq
    bias = jnp.asarray(_tail_bias())

    def seq_tile(step):
        t = jnp.minimum(step, n_tiles - 1)
        return t // tiles_per_seq, t % tiles_per_seq

    def out_tile(step):
        t = jnp.maximum(step - 1, 0)
        return t // tiles_per_seq, t % tiles_per_seq

    row_block = lambda width: pl.BlockSpec((1, tm, width), lambda s: (*seq_tile(s), 0))
    kv_w = 2 * W_KV_A
    assert ACT_VA == ACT_KA + kv_w and ACT_KA % (2 * kv_w) == 0
    kv_prev_block = pl.BlockSpec(
        (1, BLOCK, 2 * kv_w),
        lambda s: (seq_tile(s)[0], jnp.maximum(seq_tile(s)[1] * blocks_per_tile - 1, 0),
                   ACT_KA // (2 * kv_w)))
    const = lambda shape: pl.BlockSpec(shape, lambda s: (0,) * len(shape))
    stage = ((W_A + W_C) // PAIR, 2 * BLOCK, 2 * BLOCK)
    pipelined = (_nbytes((tm, ACT_WIDTH + W_B), bf16)
                 + 2 * _nbytes((BLOCK, kv_w), bf16) + _nbytes((M, D_MODEL), f32)
                 + _nbytes((D_MODEL, 2 * W_C), bf16) + 2 * _nbytes((tm, D_MODEL), f32)
                 + _nbytes((D_MIX, D_MODEL), bf16) + _nbytes(bias.shape, f32))
    scratch = [pltpu.VMEM((BLOCK + tm, kv_w), bf16), pltpu.VMEM((BLOCK + tm, kv_w), bf16),
               pltpu.VMEM((M, W_C), bf16), pltpu.VMEM((M, W_C), bf16),
               pltpu.VMEM((tm, D_MIX), f32), pltpu.VMEM(stage, f32), pltpu.VMEM(stage, bf16),
               pltpu.VMEM((tm, D_MODEL), f32)]
    scratch_bytes = (2 * _nbytes((BLOCK + tm, kv_w), bf16) + 2 * _nbytes((M, W_C), bf16)
                     + 5 * _nbytes((tm, D_MIX), f32) + _nbytes(stage, f32) + _nbytes(stage, bf16))
    return pl.pallas_call(
        functools.partial(_tail_kernel, tiles_per_seq=tiles_per_seq, n_tiles=n_tiles),
        grid=(n_tiles + 1,),
        in_specs=[pl.BlockSpec(memory_space=pltpu.SMEM),
                  row_block(ACT_WIDTH), kv_prev_block,
                  pl.BlockSpec((1, M, D_MODEL), lambda s: (seq_tile(s)[0], 0, 0)),
                  const((D_MODEL, 2 * W_C)), row_block(W_B),
                  row_block(D_MODEL), const((D_MIX, D_MODEL)),
                  const((1, D_MODEL)), const((1, D_MODEL)), const(bias.shape)],
        out_specs=pl.BlockSpec((1, tm, D_MODEL), lambda s: (*out_tile(s), 0)),
        out_shape=jax.ShapeDtypeStruct((B, S, D_MODEL), f32),
        scratch_shapes=scratch,
        compiler_params=pltpu.CompilerParams(
            dimension_semantics=("arbitrary",),
            vmem_limit_bytes=_vmem_limit(pipelined, scratch_bytes)),
        name="mix_out",
    )(sinks, act, act, mem, w_mem, ob, x, w_out, ln_gain, ln_bias, bias)


def _rope_tables(seq_len):
    pos = np.arange(seq_len, dtype=np.float64)
    inv = ROPE_THETA ** (-np.arange(0, HEAD_DIM, 2, dtype=np.float64) / HEAD_DIM)
    ang = pos[:, None] * inv[None, :]
    cos, sin = np.cos(ang), np.sin(ang)
    cos2 = np.concatenate([cos, cos, cos, cos], axis=-1)
    sin2 = np.concatenate([-sin, sin, -sin, sin], axis=-1)
    return jnp.asarray(cos2, f32), jnp.asarray(sin2, f32)


def kernel(x, mem, w_in, b_in, w_mem, attn_sinks, g_branch, w_out, ln_gain, ln_bias):
    assert w_in.shape[0] == DEPTH
    S = x.shape[1]
    cos2, sin2 = _rope_tables(S)
    for l in range(DEPTH):
        act, *views_b = _in_proj(
            x, w_in[l].astype(bf16), b_in[l][None, :], cos2, sin2, g_branch[l][None, :])
        ob = _dilated(views_b)
        x = _tail(attn_sinks[l], act, mem, w_mem[l].astype(bf16), ob, x,
                  w_out[l].astype(bf16), ln_gain[l][None, :], ln_bias[l][None, :])
    return x
```

```python
import functools
import math
from typing import Any, NamedTuple

import numpy as np
import jax
import jax.numpy as jnp
from jax import lax
from jax.experimental import pallas as pl
from jax.experimental.pallas import tpu as pltpu

D_MODEL = 1024
HEAD_DIM = 64
SWA_Q_HEADS = 8
SWA_KV_HEADS = 2
SWA_WINDOW = 128
DIL_HEADS = 4
DIL_PAIRS = ((128, 1), (512, 4), (2048, 16))
MEM_HEADS = 4
BLOCK = 128
ROPE_THETA = 10000.0
LN_EPS = 1e-5
RMS_EPS = 1e-6
DEPTH = 1
DEEPNORM_ALPHA = (2 * DEPTH) ** 0.25

W_A = SWA_Q_HEADS * HEAD_DIM
W_KV_A = SWA_KV_HEADS * HEAD_DIM
W_B = DIL_HEADS * HEAD_DIM
W_C = MEM_HEADS * HEAD_DIM
D_MIX = W_A + W_B + W_C
D_IN = W_A + 2 * W_KV_A + 3 * W_B + W_C + D_MIX

V7X_LANES = 128
V7X_VMEM_BYTES = 64 * 2**20
VMEM_HEADROOM_BYTES = 8 * 2**20

PAIR = 2 * HEAD_DIM
assert PAIR == V7X_LANES
LOG2_E = math.log2(math.e)
Q_SCALE = HEAD_DIM ** -0.5 * LOG2_E
MASKED = -1e30

OFF_QA = 0
OFF_KA = OFF_QA + W_A
OFF_VA = OFF_KA + W_KV_A
OFF_QB = OFF_VA + W_KV_A
OFF_KB = OFF_QB + W_B
OFF_VB = OFF_KB + W_B
OFF_QC = OFF_VB + W_B
OFF_Z = OFF_QC + W_C
assert OFF_Z + D_MIX == D_IN

ACT_GATE = 0
ACT_QA = ACT_GATE + D_MIX
ACT_KA = ACT_QA + W_A
ACT_VA = ACT_KA + 2 * W_KV_A
ACT_QC = ACT_VA + 2 * W_KV_A
ACT_WIDTH = ACT_QC + W_C
B_Q, B_K, B_V, B_WIDTH = 0, W_B, 2 * W_B, 3 * W_B

IN_ROW_TILE = 1024
ROW_TILE = 512

f32 = jnp.float32
bf16 = jnp.bfloat16


def _vmem_limit(pipelined_bytes, scratch_bytes=0):
    budget = V7X_VMEM_BYTES - VMEM_HEADROOM_BYTES
    assert 2 * pipelined_bytes + scratch_bytes + VMEM_HEADROOM_BYTES <= budget
    return budget


def _nbytes(shape, dtype):
    return int(np.prod(shape)) * jnp.dtype(dtype).itemsize


def _lane_is_first_head():
    return lax.broadcasted_iota(jnp.int32, (1, V7X_LANES), 1) < HEAD_DIM


def _in_proj_kernel(x_ref, w_ref, b_ref, cos_ref, sin_ref, g_ref,
                    act_ref, b1_ref, b4_ref, b16_ref, *stage_refs):
    tm = x_ref.shape[1]
    xb = x_ref[0].astype(bf16)
    cos = cos_ref[...]
    sin = sin_ref[...]
    lane = lax.broadcasted_iota(jnp.int32, (1, V7X_LANES), 1)
    first_half = (lane % HEAD_DIM) < (HEAD_DIM // 2)
    first_head = lane < HEAD_DIM

    def proj(off, width):
        acc = jnp.dot(xb, w_ref[:, off:off + width], preferred_element_type=f32)
        return acc + b_ref[:, off:off + width]

    def rope(a):
        rot = jnp.where(first_half,
                        pltpu.roll(a, V7X_LANES - HEAD_DIM // 2, 1),
                        pltpu.roll(a, HEAD_DIM // 2, 1))
        return a * cos + rot * sin

    def groups(a):
        return [a[:, i:i + V7X_LANES] for i in range(0, a.shape[1], V7X_LANES)]

    def both_halves(a):
        swapped = pltpu.roll(a, HEAD_DIM, 1)
        return jnp.where(first_head, a, swapped), jnp.where(first_head, swapped, a)

    for c in range(0, D_MIX, 2 * V7X_LANES):
        z = proj(OFF_Z + c, 2 * V7X_LANES)
        gate = z * jax.nn.sigmoid(z) * g_ref[:, c:c + 2 * V7X_LANES]
        act_ref[0, :, ACT_GATE + c:ACT_GATE + c + 2 * V7X_LANES] = gate.astype(bf16)
    for c in range(0, W_A, 2 * V7X_LANES):
        for i, a in enumerate(groups(proj(OFF_QA + c, 2 * V7X_LANES))):
            lo = ACT_QA + c + i * V7X_LANES
            act_ref[0, :, lo:lo + V7X_LANES] = (rope(a) * Q_SCALE).astype(bf16)
    k_a, v_a = groups(proj(OFF_KA, 2 * W_KV_A))
    for lo, a in ((ACT_KA, rope(k_a)), (ACT_VA, v_a)):
        kv0, kv1 = both_halves(a)
        act_ref[0, :, lo:lo + V7X_LANES] = kv0.astype(bf16)
        act_ref[0, :, lo + V7X_LANES:lo + 2 * V7X_LANES] = kv1.astype(bf16)
    mixer_b = ((OFF_QB, lambda a: rope(a) * Q_SCALE, B_Q), (OFF_KB, rope, B_K),
               (OFF_VB, lambda a: a, B_V))
    dil4, dil16 = DIL_PAIRS[1][1], DIL_PAIRS[2][1]
    for t, (off, fn, lane0) in enumerate(mixer_b):
        for g, a in enumerate(groups(proj(off, W_B))):
            a = fn(a)
            lanes = slice(lane0 + g * PAIR, lane0 + (g + 1) * PAIR)
            b1_ref[0, :, lanes] = a.astype(bf16)
            slab = t * (W_B // PAIR) + g
            stage_ref, stage4_ref = stage_refs[2 * slab], stage_refs[2 * slab + 1]
            stage_ref[...] = a
            rows4, rows16 = tm // dil4, tm // dil16
            for j4 in range(dil4):
                stream = stage_ref[pl.ds(j4, rows4, stride=dil4), :]
                b4_ref[0, j4, :, lanes] = stream.astype(bf16)
                stage4_ref[j4 * rows4:(j4 + 1) * rows4] = stream
            for j16 in range(dil16):
                j4, phase = j16 % dil4, j16 // dil4
                stream = stage4_ref[pl.ds(j4 * rows4 + phase, rows16, stride=dil16 // dil4), :]
                b16_ref[0, j16, :, lanes] = stream.astype(bf16)
    act_ref[0, :, ACT_QC:ACT_QC + W_C] = (proj(OFF_QC, W_C) * Q_SCALE).astype(bf16)


def _in_proj(x, w_in, b_in, cos2, sin2, g_branch):
    B, S, _ = x.shape
    tm = IN_ROW_TILE
    row_block = lambda width: pl.BlockSpec((1, tm, width), lambda b, i: (b, i, 0))
    out_specs = [row_block(ACT_WIDTH), row_block(B_WIDTH)]
    out_shape = [jax.ShapeDtypeStruct((B, S, ACT_WIDTH), bf16),
                 jax.ShapeDtypeStruct((B, S, B_WIDTH), bf16)]
    for _, r in DIL_PAIRS[1:]:
        out_specs.append(pl.BlockSpec((1, r, tm // r, B_WIDTH), lambda b, i: (b, 0, i, 0)))
        out_shape.append(jax.ShapeDtypeStruct((B, r, S // r, B_WIDTH), bf16))
    const = lambda shape: pl.BlockSpec(shape, lambda b, i: (0,) * len(shape))
    n_stage, stage = 2 * 3 * (W_B // PAIR), (tm, V7X_LANES)
    pipelined = (_nbytes((tm, D_MODEL), f32) + _nbytes((D_MODEL, D_IN), bf16)
                 + _nbytes((tm, ACT_WIDTH + 3 * B_WIDTH), bf16) + 2 * _nbytes((tm, V7X_LANES), f32))
    return pl.pallas_call(
        _in_proj_kernel,
        grid=(B, S // tm),
        in_specs=[row_block(D_MODEL), const((D_MODEL, D_IN)), const((1, D_IN)),
                  pl.BlockSpec((tm, V7X_LANES), lambda b, i: (i, 0)),
                  pl.BlockSpec((tm, V7X_LANES), lambda b, i: (i, 0)),
                  const((1, D_MIX))],
        out_specs=out_specs,
        out_shape=out_shape,
        scratch_shapes=[pltpu.VMEM(stage, f32)] * n_stage,
        compiler_params=pltpu.CompilerParams(
            dimension_semantics=("arbitrary", "arbitrary"),
            vmem_limit_bytes=_vmem_limit(pipelined, n_stage * _nbytes(stage, f32))),
        name="in_proj",
    )(x, w_in, b_in, cos2, sin2, g_branch)


class _Band(NamedTuple):
    sinks: tuple
    prev_valid: Any


def _attend(chains, s_ref, p_ref):
    first_head = _lane_is_first_head()
    for c, (q2, k2, _, _) in enumerate(chains):
        zero = jnp.zeros_like(q2)
        stacked = jnp.concatenate([jnp.where(first_head, q2, zero),
                                   jnp.where(first_head, zero, q2)], axis=0)
        s_ref[c, :, 0:k2.shape[0]] = lax.dot_general(
            stacked, k2, (((1,), (1,)), ((), ())), preferred_element_type=f32)
    own_keys = (lax.broadcasted_iota(jnp.int32, (BLOCK, BLOCK), 1)
                <= lax.broadcasted_iota(jnp.int32, (BLOCK, BLOCK), 0))
    tops, sink_terms = [], []
    for c, (q2, k2, _, biases) in enumerate(chains):
        m_rows, n = q2.shape[0], k2.shape[0]
        pair_tops = []
        for h in range(2):
            rows = slice(h * m_rows, (h + 1) * m_rows)
            if isinstance(biases, _Band):
                assert (m_rows, n) == (BLOCK, 2 * BLOCK)
                prev = s_ref[c, rows, 0:BLOCK]
                if biases.prev_valid is not None:
                    prev = jnp.where(biases.prev_valid, prev, MASKED)
                sh = jnp.where(own_keys, s_ref[c, rows, BLOCK:n], prev)
                m = jnp.maximum(jnp.max(sh, axis=-1, keepdims=True), biases.sinks[h])
                e = jnp.exp2((sh - m).astype(bf16))
                zero = jnp.zeros_like(e)
                p_ref[c, rows, 0:BLOCK] = jnp.where(own_keys, zero, e)
                p_ref[c, rows, BLOCK:n] = jnp.where(own_keys, e, zero)
                pair_tops.append(m)
                continue
            sh = s_ref[c, rows, 0:n]
            if biases is not None:
                sh = sh + biases[h]
            m = jnp.max(sh, axis=-1, keepdims=True)
            p_ref[c, rows, 0:n] = jnp.exp2((sh - m).astype(bf16))
            pair_tops.append(m)
        tops.append(pair_tops)
        sink_terms.append([jnp.exp2(sink - m) for sink, m in zip(biases.sinks, pair_tops)]
                          if isinstance(biases, _Band) else None)
    results = []
    for c, (q2, k2, v2, _) in enumerate(chains):
        m_rows, n = q2.shape[0], k2.shape[0]
        v_ones = jnp.concatenate([v2, jnp.ones_like(v2)], axis=1)
        o = jnp.dot(p_ref[c, :, 0:n], v_ones, preferred_element_type=f32)
        acc = jnp.where(first_head, o[:m_rows, :PAIR], o[m_rows:, :PAIR])
        denom = jnp.where(first_head, o[:m_rows, PAIR:], o[m_rows:, PAIR:])
        if sink_terms[c] is not None:
            denom = denom + jnp.where(first_head, sink_terms[c][0], sink_terms[c][1])
        lse = jnp.where(first_head, tops[c][0], tops[c][1]) + jnp.log(denom) * LOG2_E
        results.append((acc * (1.0 / denom), lse))
    return results


def _window_bias(max_dist):
    qi = np.arange(BLOCK)[:, None]
    kj = np.arange(2 * BLOCK)[None, :]
    dist = np.stack([qi - kj, qi + BLOCK - kj])
    return np.where((dist >= 0) & (dist <= max_dist), 0.0, MASKED).astype(np.float32)


def _window(blk):
    return blk * BLOCK, max(blk - 1, 0) * BLOCK, min(blk, 1)


def _merge(o_a, l_a, o_b, l_b):
    top = jnp.maximum(l_a, l_b)
    e_a = jnp.exp2(l_a - top)
    e_b = jnp.exp2(l_b - top)
    denom = e_a + e_b
    return (e_a * o_a + e_b * o_b) * (1.0 / denom), top + jnp.log(denom) * LOG2_E


def _dilated_kernel(b1_ref, b4_ref, b16_ref, bias_ref, o_ref,
                    o4_ref, l4_ref, o16_ref, l16_ref, s_ref, p_ref):
    seq_len = b1_ref.shape[1]
    n_pairs = W_B // PAIR
    dil4, dil16 = DIL_PAIRS[1][1], DIL_PAIRS[2][1]
    sub = dil16 // dil4
    pair_lanes = [slice(g * PAIR, (g + 1) * PAIR) for g in range(n_pairs)]

    def qkv(ref, index, q_rows, kv_rows, g):
        lanes = [slice(lane0 + g * PAIR, lane0 + (g + 1) * PAIR) for lane0 in (B_Q, B_K, B_V)]
        return (ref[(*index, q_rows, lanes[0])], ref[(*index, kv_rows, lanes[1])],
                ref[(*index, kv_rows, lanes[2])])

    causal = bias_ref[0, :, 0:BLOCK]
    rows = slice(None)
    chains = [(*qkv(b16_ref, (0, j), rows, rows, g), (causal, causal))
              for j in range(dil16) for g in range(n_pairs)]
    for n, (out, lse) in enumerate(_attend(chains, s_ref, p_ref)):
        j, g = n // n_pairs, n % n_pairs
        o16_ref[g, j % dil4, pl.ds(j // dil4, BLOCK, stride=sub), :] = out
        l16_ref[g, j % dil4, pl.ds(j // dil4, BLOCK, stride=sub), :] = lse

    windows = [_window(blk) for blk in range(seq_len // dil4 // BLOCK)]
    chains = [(*qkv(b4_ref, (0, j), pl.ds(u0, BLOCK), pl.ds(k0, 2 * BLOCK), g),
               (bias_ref[first], bias_ref[first]))
              for u0, k0, first in windows for j in range(dil4) for g in range(n_pairs)]
    for n, (out, lse) in enumerate(_attend(chains, s_ref, p_ref)):
        u0 = windows[n // (dil4 * n_pairs)][0]
        j, g = (n // n_pairs) % dil4, n % n_pairs
        out, lse = _merge(out, lse, o16_ref[g, j, pl.ds(u0, BLOCK), :],
                          l16_ref[g, j, pl.ds(u0, BLOCK), :])
        o4_ref[g, pl.ds(u0 * dil4 + j, BLOCK, stride=dil4), :] = out
        l4_ref[g, pl.ds(u0 * dil4 + j, BLOCK, stride=dil4), :] = lse

    windows = [_window(blk) for blk in range(seq_len // BLOCK)]
    chains = [(*qkv(b1_ref, (0,), pl.ds(r0, BLOCK), pl.ds(k0, 2 * BLOCK), g),
               (bias_ref[first], bias_ref[first]))
              for r0, k0, first in windows for g in range(n_pairs)]
    for n, (out, lse) in enumerate(_attend(chains, s_ref, p_ref)):
        r0, g = windows[n // n_pairs][0], n % n_pairs
        out, _ = _merge(out, lse, o4_ref[g, pl.ds(r0, BLOCK), :], l4_ref[g, pl.ds(r0, BLOCK), :])
        o_ref[0, pl.ds(r0, BLOCK), pair_lanes[g]] = out.astype(bf16)


def _dilated(views):
    B, S, _ = views[0].shape
    assert all(w // r == BLOCK for w, r in DIL_PAIRS) and S // DIL_PAIRS[2][1] == BLOCK
    bias = jnp.asarray(_window_bias(BLOCK))
    view_specs = [pl.BlockSpec((1, S, B_WIDTH), lambda b: (b, 0, 0))]
    for _, r in DIL_PAIRS[1:]:
        view_specs.append(pl.BlockSpec((1, r, S // r, B_WIDTH), lambda b: (b, 0, 0, 0)))
    slab = (W_B // PAIR, S, V7X_LANES)
    slab16 = (W_B // PAIR, DIL_PAIRS[1][1], S // DIL_PAIRS[1][1], V7X_LANES)
    stage = (S // BLOCK * (W_B // PAIR), 2 * BLOCK, 2 * BLOCK)
    pipelined = 10 * _nbytes((S, W_B), bf16) + _nbytes(bias.shape, f32)
    return pl.pallas_call(
        _dilated_kernel,
        grid=(B,),
        in_specs=view_specs + [pl.BlockSpec(bias.shape, lambda b: (0, 0, 0))],
        out_specs=pl.BlockSpec((1, S, W_B), lambda b: (b, 0, 0)),
        out_shape=jax.ShapeDtypeStruct((B, S, W_B), bf16),
        scratch_shapes=[pltpu.VMEM(slab, f32), pltpu.VMEM(slab, f32),
                        pltpu.VMEM(slab16, f32), pltpu.VMEM(slab16, f32),
                        pltpu.VMEM(stage, f32), pltpu.VMEM(stage, bf16)],
        compiler_params=pltpu.CompilerParams(
            dimension_semantics=("arbitrary",),
            vmem_limit_bytes=_vmem_limit(
                pipelined, 4 * _nbytes(slab, f32) + _nbytes(stage, f32) + _nbytes(stage, bf16))),
        name="dilated_attention",
    )(*views, bias)


def _tail_kernel(sink_ref, act_ref, kv_prev_ref, mem_ref,
                 w_mem_ref, ob_ref, x_ref, w_out_ref, gain_ref, ln_bias_ref,
                 out_ref, k_win_ref, v_win_ref, mk_ref, mv_ref, y_ref, s_ref, p_ref, r_ref,
                 *, tiles_per_seq, n_tiles):
    step = pl.program_id(0)
    tile = lax.rem(jnp.minimum(step, n_tiles - 1), tiles_per_seq)
    tm = x_ref.shape[1]
    groups_per_kv = (SWA_Q_HEADS // SWA_KV_HEADS) // 2

    @pl.when(step == 0)
    def _():
        r_ref[...] = jnp.zeros_like(r_ref)

    @pl.when(tile == 0)
    def _():
        mkv = jnp.dot(mem_ref[0].astype(bf16), w_mem_ref[...], preferred_element_type=f32)
        mk_ref[...] = mkv[:, :W_C].astype(bf16)
        mv_ref[...] = mkv[:, W_C:].astype(bf16)

    r = r_ref[...]
    mu = jnp.mean(r, axis=-1, keepdims=True)
    d = r - mu
    var = jnp.mean(d * d, axis=-1, keepdims=True)
    out_ref[0] = d * lax.rsqrt(var + LN_EPS) * gain_ref[...] + ln_bias_ref[...]

    kv_w = 2 * W_KV_A
    for win_ref, lane0 in ((k_win_ref, ACT_KA), (v_win_ref, ACT_VA)):
        win_ref[0:BLOCK] = kv_prev_ref[0, :, lane0 - ACT_KA:lane0 - ACT_KA + kv_w]
        win_ref[BLOCK:BLOCK + tm] = act_ref[0, :, lane0:lane0 + kv_w]

    for j in range(tm // BLOCK):
        chains, slots = [], []
        rows = slice(j * BLOCK, (j + 1) * BLOCK)
        prev_valid = tile > 0 if j == 0 else None
        k = k_win_ref[j * BLOCK:(j + 2) * BLOCK]
        v = v_win_ref[j * BLOCK:(j + 2) * BLOCK]
        for g in range(W_A // PAIR):
            kv = (g // groups_per_kv) * PAIR
            chains.append((act_ref[0, rows, ACT_QA + g * PAIR:ACT_QA + (g + 1) * PAIR],
                           k[:, kv:kv + PAIR], v[:, kv:kv + PAIR],
                           _Band((sink_ref[2 * g] * LOG2_E, sink_ref[2 * g + 1] * LOG2_E),
                                 prev_valid)))
            slots.append((rows, g * PAIR))
        for g in range(W_C // PAIR):
            c = g * PAIR
            chains.append((act_ref[0, rows, ACT_QC + c:ACT_QC + c + PAIR],
                           mk_ref[:, c:c + PAIR], mv_ref[:, c:c + PAIR], None))
            slots.append((rows, W_A + W_B + c))
        for (rows, c), (out, _) in zip(slots, _attend(chains, s_ref, p_ref)):
            y_ref[rows, c:c + PAIR] = out

    def rms_normalize(t):
        return t * lax.rsqrt(jnp.mean(t * t, axis=-1, keepdims=True) + RMS_EPS)

    y = jnp.concatenate([rms_normalize(y_ref[:, 0:W_A]),
                         rms_normalize(ob_ref[0].astype(f32)),
                         rms_normalize(y_ref[:, W_A + W_B:D_MIX])], axis=-1)
    y = (y * act_ref[0, :, ACT_GATE:ACT_GATE + D_MIX].astype(f32)).astype(bf16)
    r_ref[...] = DEEPNORM_ALPHA * x_ref[0] + jnp.dot(y, w_out_ref[...],
                                                     preferred_element_type=f32)


def _tail(sinks, act, mem, w_mem, ob, x, w_out, ln_gain, ln_bias):
    B, S, _ = x.shape
    M = mem.shape[1]
    tm = ROW_TILE
    blocks_per_tile = tm // BLOCK
    tiles_per_seq = S // tm
    n_tiles = B * tiles_per_seq
    assert SWA_WINDOW == BLOCK

    def seq_tile(step):
        t = jnp.minimum(step, n_tiles - 1)
        return t // tiles_per_seq, t % tiles_per_seq

    def out_tile(step):
        t = jnp.maximum(step - 1, 0)
        return t // tiles_per_seq, t % tiles_per_seq

    row_block = lambda width: pl.BlockSpec((1, tm, width), lambda s: (*seq_tile(s), 0))
    kv_w = 2 * W_KV_A
    assert ACT_VA == ACT_KA + kv_w and ACT_KA % (2 * kv_w) == 0
    kv_prev_block = pl.BlockSpec(
        (1, BLOCK, 2 * kv_w),
        lambda s: (seq_tile(s)[0], jnp.maximum(seq_tile(s)[1] * blocks_per_tile - 1, 0),
                   ACT_KA // (2 * kv_w)))
    const = lambda shape: pl.BlockSpec(shape, lambda s: (0,) * len(shape))
    stage = ((W_A + W_C) // PAIR, 2 * BLOCK, 2 * BLOCK)
    pipelined = (_nbytes((tm, ACT_WIDTH + W_B), bf16)
                 + 2 * _nbytes((BLOCK, kv_w), bf16) + _nbytes((M, D_MODEL), f32)
                 + _nbytes((D_MODEL, 2 * W_C), bf16) + 2 * _nbytes((tm, D_MODEL), f32)
                 + _nbytes((D_MIX, D_MODEL), bf16))
    scratch = [pltpu.VMEM((BLOCK + tm, kv_w), bf16), pltpu.VMEM((BLOCK + tm, kv_w), bf16),
               pltpu.VMEM((M, W_C), bf16), pltpu.VMEM((M, W_C), bf16),
               pltpu.VMEM((tm, D_MIX), f32), pltpu.VMEM(stage, f32), pltpu.VMEM(stage, bf16),
               pltpu.VMEM((tm, D_MODEL), f32)]
    scratch_bytes = (2 * _nbytes((BLOCK + tm, kv_w), bf16) + 2 * _nbytes((M, W_C), bf16)
                     + 5 * _nbytes((tm, D_MIX), f32) + _nbytes(stage, f32) + _nbytes(stage, bf16))
    return pl.pallas_call(
        functools.partial(_tail_kernel, tiles_per_seq=tiles_per_seq, n_tiles=n_tiles),
        grid=(n_tiles + 1,),
        in_specs=[pl.BlockSpec(memory_space=pltpu.SMEM),
                  row_block(ACT_WIDTH), kv_prev_block,
                  pl.BlockSpec((1, M, D_MODEL), lambda s: (seq_tile(s)[0], 0, 0)),
                  const((D_MODEL, 2 * W_C)), row_block(W_B),
                  row_block(D_MODEL), const((D_MIX, D_MODEL)),
                  const((1, D_MODEL)), const((1, D_MODEL))],
        out_specs=pl.BlockSpec((1, tm, D_MODEL), lambda s: (*out_tile(s), 0)),
        out_shape=jax.ShapeDtypeStruct((B, S, D_MODEL), f32),
        scratch_shapes=scratch,
        compiler_params=pltpu.CompilerParams(
            dimension_semantics=("arbitrary",),
            vmem_limit_bytes=_vmem_limit(pipelined, scratch_bytes)),
        name="mix_out",
    )(sinks, act, act, mem, w_mem, ob, x, w_out, ln_gain, ln_bias)


def _rope_tables(seq_len):
    pos = np.arange(seq_len, dtype=np.float64)
    inv = ROPE_THETA ** (-np.arange(0, HEAD_DIM, 2, dtype=np.float64) / HEAD_DIM)
    ang = pos[:, None] * inv[None, :]
    cos, sin = np.cos(ang), np.sin(ang)
    cos2 = np.concatenate([cos, cos, cos, cos], axis=-1)
    sin2 = np.concatenate([-sin, sin, -sin, sin], axis=-1)
    return jnp.asarray(cos2, f32), jnp.asarray(sin2, f32)


def kernel(x, mem, w_in, b_in, w_mem, attn_sinks, g_branch, w_out, ln_gain, ln_bias):
    assert w_in.shape[0] == DEPTH
    S = x.shape[1]
    cos2, sin2 = _rope_tables(S)
    for l in range(DEPTH):
        act, *views_b = _in_proj(
            x, w_in[l].astype(bf16), b_in[l][None, :], cos2, sin2, g_branch[l][None, :])
        ob = _dilated(views_b)
        x = _tail(attn_sinks[l], act, mem, w_mem[l].astype(bf16), ob, x,
                  w_out[l].astype(bf16), ln_gain[l][None, :], ln_bias[l][None, :])
    return x
```
